```python
import math
import jax, jax.numpy as jnp
from jax import lax
import numpy as np

D_MODEL = 2048
BATCH = 4
SEQ = 4096
DEPTH = 2

HEAD_DIM = 128
N_SB_HEADS = D_MODEL // (2 * HEAD_DIM)
N_DIFF_HEADS = D_MODEL // (4 * HEAD_DIM)
SB_WIDTH = N_SB_HEADS * HEAD_DIM
DIFF_WIDTH = N_DIFF_HEADS * 2 * HEAD_DIM
MIX_WIDTH = SB_WIDTH + DIFF_WIDTH
IN_WIDTH = 3 * SB_WIDTH + 3 * DIFF_WIDTH
D_FF = 4 * D_MODEL
N_BUCKETS = 32
MAX_DISTANCE = 128
Q_BLOCK = 128
LN_EPS = 1e-5
RMS_EPS = 1e-5
NEG_BIG = -1e30
ALPHA = (2 * DEPTH) ** 0.25
INIT_BETA = (8 * DEPTH) ** -0.25

kernel_name = "hybrid_sb_diff_attn_deepnorm"


def layernorm(x, g, b):
    xf = x.astype(jnp.float32)
    mu = jnp.mean(xf, axis=-1, keepdims=True)
    var = jnp.mean(jnp.square(xf - mu), axis=-1, keepdims=True)
    y = (xf - mu) * lax.rsqrt(var + LN_EPS) * g.astype(jnp.float32) + b.astype(jnp.float32)
    return y.astype(x.dtype)


def rmsnorm(x, g):
    xf = x.astype(jnp.float32)
    y = xf * lax.rsqrt(jnp.mean(jnp.square(xf), axis=-1, keepdims=True) + RMS_EPS)
    return y * g.astype(jnp.float32)


def t5_causal_bucket(dist):
    n = jnp.maximum(dist, 0)
    max_exact = N_BUCKETS // 2
    nf = jnp.maximum(n, 1).astype(jnp.float32)
    large = max_exact + (jnp.log(nf / max_exact) / math.log(MAX_DISTANCE / max_exact)
                         * (N_BUCKETS - max_exact)).astype(jnp.int32)
    large = jnp.minimum(large, N_BUCKETS - 1)
    return jnp.where(n < max_exact, n, large)


def stick_breaking_attention(q, k, v):
    B, H, S, d = q.shape
    nb = S // Q_BLOCK
    scale = 1.0 / math.sqrt(d)
    kf = k.astype(jnp.float32)
    vf = v.astype(jnp.float32)
    q_blocks = jnp.moveaxis(q.reshape(B, H, nb, Q_BLOCK, d), 2, 0)
    s_pos = jnp.arange(S)

    def block(args):
        i, qb = args
        z = jnp.einsum('bhqd,bhkd->bhqk', qb.astype(jnp.float32), kf) * scale
        t_pos = i * Q_BLOCK + jnp.arange(Q_BLOCK)
        causal = s_pos[None, :] < t_pos[:, None]
        log_fail = jnp.where(causal, jax.nn.log_sigmoid(-z), 0.0)
        suffix = lax.cumsum(log_fail, axis=3, reverse=True) - log_fail
        w = jnp.where(causal, jnp.exp(jax.nn.log_sigmoid(z) + suffix), 0.0)
        return jnp.einsum('bhqk,bhkd->bhqd', w, vf)

    out = lax.map(block, (jnp.arange(nb), q_blocks))
    out = jnp.moveaxis(out, 0, 2).reshape(B, H, S, d)
    return jnp.transpose(out, (0, 2, 1, 3))


def differential_attention(q, k, v, lam, rel_bias):
    B, H, _, S, d = q.shape
    nb = S // Q_BLOCK
    scale = 1.0 / math.sqrt(d)
    kf = k.astype(jnp.float32)
    vf = v.astype(jnp.float32)
    q_blocks = jnp.moveaxis(q.reshape(B, H, 2, nb, Q_BLOCK, d), 3, 0)
    s_pos = jnp.arange(S)
    table = rel_bias.astype(jnp.float32)

    def block(args):
        i, qb = args
        t_pos = i * Q_BLOCK + jnp.arange(Q_BLOCK)
        dist = t_pos[:, None] - s_pos[None, :]
        bias = jnp.transpose(table[t5_causal_bucket(dist)], (2, 0, 1))
        logits = jnp.einsum('bhmqd,bhmkd->bhmqk', qb.astype(jnp.float32), kf) * scale
        logits = logits + bias[None, :, None]
        logits = jnp.where((dist >= 0)[None, None, None], logits, NEG_BIG)
        p = jax.nn.softmax(logits, axis=-1)
        a = p[:, :, 0] - lam * p[:, :, 1]
        return jnp.einsum('bhqk,bhkd->bhqd', a, vf)

    out = lax.map(block, (jnp.arange(nb), q_blocks))
    out = jnp.moveaxis(out, 0, 2).reshape(B, H, S, 2 * d)
    return jnp.transpose(out, (0, 2, 1, 3))


def hybrid_mixer(h, w_in, w_out, sb_norm_g, lam_q1, lam_k1, lam_q2, lam_k2, diff_norm_g,
                 rel_bias, layer_idx):
    B, S, _ = h.shape
    proj = jnp.einsum('bsd,de->bse', h, w_in)
    cuts = list(np.cumsum([SB_WIDTH] * 3 + [DIFF_WIDTH] * 2))
    sb_q, sb_k, sb_v, df_q, df_k, df_v = jnp.split(proj, cuts, axis=-1)

    def sb_heads(t):
        return jnp.transpose(t.reshape(B, S, N_SB_HEADS, HEAD_DIM), (0, 2, 1, 3))
    sb_out = stick_breaking_attention(sb_heads(sb_q), sb_heads(sb_k), sb_heads(sb_v))
    sb_out = rmsnorm(sb_out, sb_norm_g).reshape(B, S, SB_WIDTH)

    def qk_heads(t):
        return jnp.transpose(t.reshape(B, S, N_DIFF_HEADS, 2, HEAD_DIM), (0, 2, 3, 1, 4))
    v_d = jnp.transpose(df_v.reshape(B, S, N_DIFF_HEADS, 2 * HEAD_DIM), (0, 2, 1, 3))
    lam_init = 0.8 - 0.6 * math.exp(-0.3 * layer_idx)
    lam = (jnp.exp(jnp.sum(lam_q1.astype(jnp.float32) * lam_k1.astype(jnp.float32)))
           - jnp.exp(jnp.sum(lam_q2.astype(jnp.float32) * lam_k2.astype(jnp.float32)))
           + lam_init)
    df_out = differential_attention(qk_heads(df_q), qk_heads(df_k), v_d, lam, rel_bias)
    df_out = (rmsnorm(df_out, diff_norm_g) * (1.0 - lam_init)).reshape(B, S, DIFF_WIDTH)

    merged = jnp.concatenate([sb_out, df_out], axis=-1).astype(h.dtype)
    return jnp.einsum('bse,ed->bsd', merged, w_out)


def squared_relu_mlp(h, w_up, w_down):
    u = jnp.einsum('bsd,df->bsf', h, w_up)
    return jnp.einsum('bsf,fd->bsd', jnp.square(jax.nn.relu(u)), w_down)


def setup_inputs(seed: int = 0) -> dict:
    key = jax.random.key(seed)
    ks = jax.random.split(key, 20)
    f32 = jnp.float32

    def nrm(k, shape, scale):
        return jax.random.normal(k, shape, f32) * scale

    x = jax.random.normal(ks[0], (BATCH, SEQ, D_MODEL), f32)
    ln0_g = 1.0 + nrm(ks[1], (D_MODEL,), 0.02)
    ln0_b = nrm(ks[2], (D_MODEL,), 0.02)
    col_scale = np.concatenate([
        np.ones(2 * SB_WIDTH), np.full(SB_WIDTH, INIT_BETA),
        np.ones(2 * DIFF_WIDTH), np.full(DIFF_WIDTH, INIT_BETA)]).astype(np.float32)
    w_in = nrm(ks[3], (DEPTH, D_MODEL, IN_WIDTH), D_MODEL ** -0.5) * jnp.asarray(col_scale)
    w_out = nrm(ks[4], (DEPTH, MIX_WIDTH, D_MODEL), MIX_WIDTH ** -0.5 * INIT_BETA)
    sb_norm_g = 1.0 + nrm(ks[5], (DEPTH, HEAD_DIM), 0.02)
    lam_q1 = nrm(ks[6], (DEPTH, HEAD_DIM), 0.1)
    lam_k1 = nrm(ks[7], (DEPTH, HEAD_DIM), 0.1)
    lam_q2 = nrm(ks[8], (DEPTH, HEAD_DIM), 0.1)
    lam_k2 = nrm(ks[9], (DEPTH, HEAD_DIM), 0.1)
    diff_norm_g = 1.0 + nrm(ks[10], (DEPTH, 2 * HEAD_DIM), 0.02)
    rel_bias = nrm(ks[11], (N_BUCKETS, N_DIFF_HEADS), 0.5)
    ln1_g = 1.0 + nrm(ks[12], (DEPTH, D_MODEL), 0.02)
    ln1_b = nrm(ks[13], (DEPTH, D_MODEL), 0.02)
    w_up = nrm(ks[14], (DEPTH, D_MODEL, D_FF), D_MODEL ** -0.5 * INIT_BETA)
    w_down = nrm(ks[15], (DEPTH, D_FF, D_MODEL), D_FF ** -0.5 * INIT_BETA)
    ln2_g = 1.0 + nrm(ks[16], (DEPTH, D_MODEL), 0.02)
    ln2_b = nrm(ks[17], (DEPTH, D_MODEL), 0.02)
    return {"x": x, "ln0_g": ln0_g, "ln0_b": ln0_b, "w_in": w_in, "w_out": w_out,
            "sb_norm_g": sb_norm_g, "lam_q1": lam_q1, "lam_k1": lam_k1,
            "lam_q2": lam_q2, "lam_k2": lam_k2, "diff_norm_g": diff_norm_g,
            "rel_bias": rel_bias, "ln1_g": ln1_g, "ln1_b": ln1_b,
            "w_up": w_up, "w_down": w_down, "ln2_g": ln2_g, "ln2_b": ln2_b}


def reference(x, ln0_g, ln0_b, w_in, w_out, sb_norm_g, lam_q1, lam_k1, lam_q2, lam_k2,
              diff_norm_g, rel_bias, ln1_g, ln1_b, w_up, w_down, ln2_g, ln2_b):
    h = layernorm(x, ln0_g, ln0_b)
    for l in range(DEPTH):
        mix = hybrid_mixer(h, w_in[l], w_out[l], sb_norm_g[l], lam_q1[l], lam_k1[l],
                           lam_q2[l], lam_k2[l], diff_norm_g[l], rel_bias, l)
        h = layernorm(ALPHA * h + mix, ln1_g[l], ln1_b[l])
        ff = squared_relu_mlp(h, w_up[l], w_down[l])
        h = layernorm(ALPHA * h + ff, ln2_g[l], ln2_b[l])
    return h
```

```python
import functools
import math

import jax
import jax.numpy as jnp
from jax import lax
from jax.experimental import pallas as pl
from jax.experimental.pallas import tpu as pltpu

HEAD_DIM = 128
N_BUCKETS = 32
MAX_DISTANCE = 128
LN_EPS = 1e-5
RMS_EPS = 1e-5
NEG_BIG = -1e30

LANES = 128
V7X_VMEM_BYTES = 64 * 1024 * 1024

F32 = jnp.float32
BF16 = jnp.bfloat16


def _vmem_limit(block_bytes, scratch_bytes, temp_bytes):
    need = 2 * block_bytes + scratch_bytes + temp_bytes
    return int(min(max(need, 16 * 1024 * 1024), V7X_VMEM_BYTES - 6 * 1024 * 1024))


def _params(semantics, vmem_bytes):
    return pltpu.CompilerParams(dimension_semantics=semantics, vmem_limit_bytes=vmem_bytes)


def _layernorm(v, g, b):
    mu = jnp.mean(v, axis=-1, keepdims=True)
    c = v - mu
    var = jnp.mean(c * c, axis=-1, keepdims=True)
    return c * lax.rsqrt(var + LN_EPS) * g + b


def _dot(a, b):
    return jnp.dot(a, b, preferred_element_type=F32)


def _dot_nt(a, b):
    return lax.dot_general(a, b, (((1,), (1,)), ((), ())), preferred_element_type=F32)


def _ln_kernel(x_ref, g_ref, b_ref, o32_ref, o16_ref):
    y = _layernorm(x_ref[...], g_ref[...], b_ref[...])
    o32_ref[...] = y
    o16_ref[...] = y.astype(BF16)


def _ln_call(x, g, b, tm):
    m, d = x.shape
    blk = tm * d * (4 + 4 + 2)
    return pl.pallas_call(
        _ln_kernel,
        grid=(m // tm,),
        in_specs=[pl.BlockSpec((tm, d), lambda i: (i, 0)),
                  pl.BlockSpec((1, d), lambda i: (0, 0)),
                  pl.BlockSpec((1, d), lambda i: (0, 0))],
        out_specs=[pl.BlockSpec((tm, d), lambda i: (i, 0)),
                   pl.BlockSpec((tm, d), lambda i: (i, 0))],
        out_shape=[jax.ShapeDtypeStruct((m, d), F32), jax.ShapeDtypeStruct((m, d), BF16)],
        compiler_params=_params(("parallel",), _vmem_limit(blk, 0, 4 * tm * d * 4)),
        name="ln_in",
    )(x, g.reshape(1, d), b.reshape(1, d))


def _matmul_kernel(x_ref, w_ref, o_ref):
    o_ref[...] = _dot(x_ref[...], w_ref[...]).astype(o_ref.dtype)


def _matmul_call(x, w, tm, tn):
    m, k = x.shape
    n = w.shape[1]
    blk = (tm * k + k * tn + tm * tn) * 2
    return pl.pallas_call(
        _matmul_kernel,
        grid=(m // tm, n // tn),
        in_specs=[pl.BlockSpec((tm, k), lambda i, j: (i, 0)),
                  pl.BlockSpec((k, tn), lambda i, j: (0, j))],
        out_specs=pl.BlockSpec((tm, tn), lambda i, j: (i, j)),
        out_shape=jax.ShapeDtypeStruct((m, n), BF16),
        compiler_params=_params(("parallel", "parallel"), _vmem_limit(blk, 0, 2 * tm * tn * 4)),
        name="in_proj",
    )(x, w)


def _sb_kernel(q_ref, k_ref, v_ref, uu_ref, g_ref, o_ref, acc_ref, carry_ref, *, bq, scale):
    i = pl.program_id(2)
    nsub = bq // LANES
    q = q_ref[0]
    uu = uu_ref[...]
    acc_ref[...] = jnp.zeros_like(acc_ref)
    carry_ref[...] = jnp.zeros_like(carry_ref)
    row = lax.broadcasted_iota(jnp.int32, (bq, LANES), 0)
    col = lax.broadcasted_iota(jnp.int32, (bq, LANES), 1)

    def chunk(c, masked):
        start = pl.multiple_of(c * LANES, LANES)
        ks = k_ref[0, pl.ds(start, LANES), :]
        vs = v_ref[0, pl.ds(start, LANES), :]
        z = _dot_nt(q, ks) * scale
        sp = jnp.maximum(z, 0.0) + jnp.log(1.0 + jnp.exp(-jnp.abs(z)))
        lf = -sp
        if masked:
            causal = (col + (c * LANES - i * bq)) < row
            lf = jnp.where(causal, lf, 0.0)
        hi = lf.astype(BF16)
        lo = (lf - hi.astype(F32)).astype(BF16)
        cs = _dot(jnp.concatenate([hi, lo], axis=1), uu)
        w = jnp.exp((z - sp) + cs[:, :LANES] + carry_ref[...])
        if masked:
            w = jnp.where(causal, w, 0.0)
        acc_ref[...] += _dot(w.astype(BF16), vs)
        carry_ref[...] += cs[:, LANES:]

    for j in range(nsub):
        chunk(i * nsub + (nsub - 1 - j), True)

    def body(t, _):
        chunk(i * nsub - 1 - t, False)
        return 0

    lax.fori_loop(0, i * nsub, body, 0)

    out = acc_ref[...]
    y = out * lax.rsqrt(jnp.mean(out * out, axis=-1, keepdims=True) + RMS_EPS) * g_ref[...]
    o_ref[0] = y.astype(o_ref.dtype)


def _sb_call(proj, uu, g, n_sb, bq):
    b, s, _ = proj.shape
    blk = (2 * bq * HEAD_DIM + 2 * s * HEAD_DIM) * 2 + uu.size * 2
    kern = functools.partial(_sb_kernel, bq=bq, scale=1.0 / math.sqrt(HEAD_DIM))
    return pl.pallas_call(
        kern,
        grid=(b, n_sb, s // bq),
        in_specs=[pl.BlockSpec((1, bq, HEAD_DIM), lambda bi, h, i: (bi, i, h)),
                  pl.BlockSpec((1, s, HEAD_DIM), lambda bi, h, i: (bi, 0, n_sb + h)),
                  pl.BlockSpec((1, s, HEAD_DIM), lambda bi, h, i: (bi, 0, 2 * n_sb + h)),
                  pl.BlockSpec(uu.shape, lambda bi, h, i: (0, 0)),
                  pl.BlockSpec((1, HEAD_DIM), lambda bi, h, i: (0, 0))],
        out_specs=pl.BlockSpec((1, bq, HEAD_DIM), lambda bi, h, i: (bi, i, h)),
        out_shape=jax.ShapeDtypeStruct((b, s, n_sb * HEAD_DIM), BF16),
        scratch_shapes=[pltpu.VMEM((bq, HEAD_DIM), F32), pltpu.VMEM((bq, LANES), F32)],
        compiler_params=_params(("parallel", "parallel", "arbitrary"),
                                _vmem_limit(blk, 2 * bq * LANES * 4, 24 * bq * LANES * 4)),
        name="sb_attn",
    )(proj, proj, proj, uu, g.reshape(1, HEAD_DIM))


def _cumsum_matrix():
    j = jnp.arange(LANES)[:, None]
    s = jnp.arange(LANES)[None, :]
    u = (j > s).astype(BF16)
    blk = jnp.concatenate([u, jnp.ones((LANES, LANES), BF16)], axis=1)
    return jnp.concatenate([blk, blk], axis=0)


def _bias_kernel(rb_ref, o_ref, *, bq):
    h = pl.program_id(0)
    t = lax.broadcasted_iota(jnp.int32, (bq, bq), 0)
    s = lax.broadcasted_iota(jnp.int32, (bq, bq), 1)
    max_exact = N_BUCKETS // 2
    for which in range(2):
        dist = t - s + which * bq
        n = jnp.maximum(dist, 0)
        nf = jnp.maximum(n, 1).astype(F32)
        large = max_exact + (jnp.log(nf / max_exact) / math.log(MAX_DISTANCE / max_exact)
                             * (N_BUCKETS - max_exact)).astype(jnp.int32)
        large = jnp.minimum(large, N_BUCKETS - 1)
        bucket = jnp.where(n < max_exact, n, large)
        bias = jnp.zeros((bq, bq), F32)
        for bkt in range(N_BUCKETS):
            bias = jnp.where(bucket == bkt, rb_ref[bkt, h], bias)
        o_ref[0, which] = bias


def _bias_call(rel_bias, bq):
    n_df = rel_bias.shape[1]
    return pl.pallas_call(
        functools.partial(_bias_kernel, bq=bq),
        grid=(n_df,),
        in_specs=[pl.BlockSpec(memory_space=pltpu.SMEM)],
        out_specs=pl.BlockSpec((1, 2, bq, bq), lambda h: (h, 0, 0, 0)),
        out_shape=jax.ShapeDtypeStruct((n_df, 2, bq, bq), F32),
        compiler_params=_params(("arbitrary",), _vmem_limit(2 * bq * bq * 4, 0, 8 * bq * bq * 4)),
        name="t5_bias",
    )(rel_bias)


def _df_kernel(rb_ref, lq1_ref, lk1_ref, lq2_ref, lk2_ref, q_ref, k_ref, v_ref, bias_ref, g_ref,
               o_ref, m_ref, l_ref, acc_ref, *, bq, scale, lam_init):
    h = pl.program_id(1)
    i = pl.program_id(2)
    q = q_ref[0]
    far_bias = rb_ref[N_BUCKETS - 1, h]
    m_ref[...] = jnp.full_like(m_ref, -jnp.inf)
    l_ref[...] = jnp.zeros_like(l_ref)
    acc_ref[...] = jnp.zeros_like(acc_ref)
    row = lax.broadcasted_iota(jnp.int32, (bq, bq), 0)
    col = lax.broadcasted_iota(jnp.int32, (bq, bq), 1)

    def block(j, mode):
        start = pl.multiple_of(j * bq, bq)
        kk = k_ref[0, pl.ds(start, bq), :]
        vv = v_ref[0, pl.ds(start, bq), :]
        for mp in range(2):
            qm = q[:, mp * HEAD_DIM:(mp + 1) * HEAD_DIM]
            km = kk[:, mp * HEAD_DIM:(mp + 1) * HEAD_DIM]
            s = _dot_nt(qm, km) * scale
            if mode == "far":
                s = s + far_bias
            elif mode == "near":
                s = s + bias_ref[0, 1]
            else:
                s = jnp.where(row >= col, s + bias_ref[0, 0], NEG_BIG)
            m_old = m_ref[mp]
            m_new = jnp.maximum(m_old, jnp.max(s, axis=-1, keepdims=True))
            alpha = jnp.exp(m_old - m_new)
            p = jnp.exp(s - m_new)
            l_ref[mp] = alpha * l_ref[mp] + jnp.sum(p, axis=-1, keepdims=True)
            acc_ref[mp] = alpha * acc_ref[mp] + _dot(p.astype(BF16), vv)
            m_ref[mp] = m_new

    def far_body(j, _):
        block(j, "far")
        return 0

    lax.fori_loop(0, jnp.maximum(i - 1, 0), far_body, 0)

    @pl.when(i >= 1)
    def _():
        block(i - 1, "near")

    block(i, "diag")

    lam = (jnp.exp(jnp.sum(lq1_ref[...] * lk1_ref[...], axis=-1, keepdims=True))
           - jnp.exp(jnp.sum(lq2_ref[...] * lk2_ref[...], axis=-1, keepdims=True)) + lam_init)
    out = acc_ref[0] / l_ref[0] - lam * (acc_ref[1] / l_ref[1])
    y = out * lax.rsqrt(jnp.mean(out * out, axis=-1, keepdims=True) + RMS_EPS) * g_ref[...]
    o_ref[0] = (y * (1.0 - lam_init)).astype(o_ref.dtype)


def _df_call(proj, rel_bias, lam_vecs, bias_tiles, g, n_sb, n_df, bq, lam_init):
    b, s, _ = proj.shape
    dv = 2 * HEAD_DIM
    q_off = 3 * n_sb * HEAD_DIM // dv
    k_off = q_off + n_df
    v_off = k_off + n_df
    blk = (2 * bq * dv + 2 * s * dv) * 2 + 2 * bq * bq * 4
    kern = functools.partial(_df_kernel, bq=bq, scale=1.0 / math.sqrt(HEAD_DIM), lam_init=lam_init)
    vec_spec = pl.BlockSpec((1, HEAD_DIM), lambda bi, h, i: (0, 0))
    return pl.pallas_call(
        kern,
        grid=(b, n_df, s // bq),
        in_specs=[pl.BlockSpec(memory_space=pltpu.SMEM),
                  vec_spec, vec_spec, vec_spec, vec_spec,
                  pl.BlockSpec((1, bq, dv), lambda bi, h, i: (bi, i, q_off + h)),
                  pl.BlockSpec((1, s, dv), lambda bi, h, i: (bi, 0, k_off + h)),
                  pl.BlockSpec((1, s, dv), lambda bi, h, i: (bi, 0, v_off + h)),
                  pl.BlockSpec((1, 2, bq, bq), lambda bi, h, i: (h, 0, 0, 0)),
                  pl.BlockSpec((1, dv), lambda bi, h, i: (0, 0))],
        out_specs=pl.BlockSpec((1, bq, dv), lambda bi, h, i: (bi, i, h)),
        out_shape=jax.ShapeDtypeStruct((b, s, n_df * dv), BF16),
        scratch_shapes=[pltpu.VMEM((2, bq, 1), F32), pltpu.VMEM((2, bq, 1), F32),
                        pltpu.VMEM((2, bq, dv), F32)],
        compiler_params=_params(("parallel", "parallel", "arbitrary"),
                                _vmem_limit(blk, 2 * bq * (dv + 2 * LANES) * 4, 16 * bq * bq * 4)),
        name="df_attn",
    )(rel_bias, *[v.reshape(1, HEAD_DIM) for v in lam_vecs], proj, proj, proj, bias_tiles,
      g.reshape(1, dv))


def _outproj_kernel(sb_ref, df_ref, w_ref, h_ref, g_ref, b_ref, o32_ref, o16_ref, *, alpha):
    half = sb_ref.shape[1]
    mix = _dot(sb_ref[...], w_ref[:half, :]) + _dot(df_ref[...], w_ref[half:, :])
    y = _layernorm(alpha * h_ref[...] + mix, g_ref[...], b_ref[...])
    o32_ref[...] = y
    o16_ref[...] = y.astype(BF16)


def _outproj_call(sb_o, df_o, w, h, g, b, tm, alpha):
    m, d = h.shape
    half = sb_o.shape[1]
    blk = tm * half * 2 * 2 + tm * d * (4 + 4 + 2) + w.size * 2
    row = lambda i: (i, 0)
    fixed = lambda i: (0, 0)
    return pl.pallas_call(
        functools.partial(_outproj_kernel, alpha=alpha),
        grid=(m // tm,),
        in_specs=[pl.BlockSpec((tm, half), row), pl.BlockSpec((tm, half), row),
                  pl.BlockSpec(w.shape, fixed), pl.BlockSpec((tm, d), row),
                  pl.BlockSpec((1, d), fixed), pl.BlockSpec((1, d), fixed)],
        out_specs=[pl.BlockSpec((tm, d), row), pl.BlockSpec((tm, d), row)],
        out_shape=[jax.ShapeDtypeStruct((m, d), F32), jax.ShapeDtypeStruct((m, d), BF16)],
        compiler_params=_params(("parallel",), _vmem_limit(blk, 0, 4 * tm * d * 4)),
        name="out_proj_ln",
    )(sb_o, df_o, w, h, g.reshape(1, d), b.reshape(1, d))


def _mlp_kernel(x16_ref, x32_ref, wu_ref, wd_ref, g_ref, b_ref, o32_ref, o16_ref, acc_ref, *, alpha):
    f = pl.program_id(1)
    u = _dot(x16_ref[...], wu_ref[...])
    a = jnp.square(jnp.maximum(u, 0.0)).astype(BF16)
    part = _dot(a, wd_ref[...])

    @pl.when(f == 0)
    def _():
        acc_ref[...] = part

    @pl.when(f > 0)
    def _():
        acc_ref[...] += part

    @pl.when(f == pl.num_programs(1) - 1)
    def _():
        y = _layernorm(alpha * x32_ref[...] + acc_ref[...], g_ref[...], b_ref[...])
        o32_ref[...] = y
        o16_ref[...] = y.astype(BF16)


def _mlp_call(x16, x32, wu, wd, g, b, tm, tf, alpha):
    m, d = x32.shape
    ff = wu.shape[1]
    blk = tm * d * (2 + 4 + 4 + 2) + 2 * d * tf * 2
    row = lambda i, f: (i, 0)
    fixed = lambda i, f: (0, 0)
    return pl.pallas_call(
        functools.partial(_mlp_kernel, alpha=alpha),
        grid=(m // tm, ff // tf),
        in_specs=[pl.BlockSpec((tm, d), row), pl.BlockSpec((tm, d), row),
                  pl.BlockSpec((d, tf), lambda i, f: (0, f)),
                  pl.BlockSpec((tf, d), lambda i, f: (f, 0)),
                  pl.BlockSpec((1, d), fixed), pl.BlockSpec((1, d), fixed)],
        out_specs=[pl.BlockSpec((tm, d), row), pl.BlockSpec((tm, d), row)],
        out_shape=[jax.ShapeDtypeStruct((m, d), F32), jax.ShapeDtypeStruct((m, d), BF16)],
        scratch_shapes=[pltpu.VMEM((tm, d), F32)],
        compiler_params=_params(("parallel", "arbitrary"),
                                _vmem_limit(blk, tm * d * 4, 2 * tm * tf * 4 + 3 * tm * d * 4)),
        name="mlp_ln",
    )(x16, x32, wu, wd, g.reshape(1, d), b.reshape(1, d))


def _tile(n, want):
    t = min(n, want)
    while n % t or t % LANES:
        t -= LANES
        assert t > 0, (n, want)
    return t


def kernel(x, ln0_g, ln0_b, w_in, w_out, sb_norm_g, lam_q1, lam_k1, lam_q2, lam_k2, diff_norm_g,
           rel_bias, ln1_g, ln1_b, w_up, w_down, ln2_g, ln2_b):
    b, s, d = x.shape
    depth = w_in.shape[0]
    n_sb = d // (2 * HEAD_DIM)
    n_df = d // (4 * HEAD_DIM)
    assert w_in.shape[2] == 3 * n_sb * HEAD_DIM + 3 * n_df * 2 * HEAD_DIM
    assert rel_bias.shape == (N_BUCKETS, n_df)
    alpha = (2 * depth) ** 0.25
    m = b * s

    tm_ln = _tile(m, 512)
    tm_proj, tn_proj = _tile(m, 1024), _tile(w_in.shape[2], 1024)
    tm_out = _tile(m, 512)
    tm_mlp, tf_mlp = _tile(m, 512), _tile(w_up.shape[2], 512)
    bq_sb = _tile(s, 256)
    bq_df = _tile(s, 256)
    assert bq_df >= MAX_DISTANCE

    w_in16, w_out16 = w_in.astype(BF16), w_out.astype(BF16)
    w_up16, w_down16 = w_up.astype(BF16), w_down.astype(BF16)
    uu = _cumsum_matrix()
    bias_tiles = _bias_call(rel_bias, bq_df)

    h32, h16 = _ln_call(x.reshape(m, d), ln0_g, ln0_b, tm_ln)
    for l in range(depth):
        lam_init = 0.8 - 0.6 * math.exp(-0.3 * l)
        proj = _matmul_call(h16, w_in16[l], tm_proj, tn_proj).reshape(b, s, -1)
        sb_o = _sb_call(proj, uu, sb_norm_g[l], n_sb, bq_sb)
        df_o = _df_call(proj, rel_bias, (lam_q1[l], lam_k1[l], lam_q2[l], lam_k2[l]), bias_tiles,
                        diff_norm_g[l], n_sb, n_df, bq_df, lam_init)
        h32, h16 = _outproj_call(sb_o.reshape(m, -1), df_o.reshape(m, -1), w_out16[l], h32,
                                 ln1_g[l], ln1_b[l], tm_out, alpha)
        h32, h16 = _mlp_call(h16, h32, w_up16[l], w_down16[l], ln2_g[l], ln2_b[l],
                             tm_mlp, tf_mlp, alpha)
    return h32.reshape(b, s, d)
```

```python
import functools
import math

import jax
import jax.numpy as jnp
from jax import lax
from jax.experimental import pallas as pl
from jax.experimental.pallas import tpu as pltpu

HEAD_DIM = 128
N_BUCKETS = 32
MAX_DISTANCE = 128
LN_EPS = 1e-5
RMS_EPS = 1e-5
NEG_BIG = -1e30
F32_EXP_ZERO_BELOW = -104.0

LANES = 128
V7X_VMEM_BYTES = 64 * 1024 * 1024

F32 = jnp.float32
BF16 = jnp.bfloat16


def _vmem_limit(block_bytes, scratch_bytes, temp_bytes):
    need = 2 * block_bytes + scratch_bytes + temp_bytes
    return int(min(max(need, 16 * 1024 * 1024), V7X_VMEM_BYTES - 6 * 1024 * 1024))


def _params(semantics, vmem_bytes):
    return pltpu.CompilerParams(dimension_semantics=semantics, vmem_limit_bytes=vmem_bytes)


def _layernorm(v, g, b):
    mu = jnp.mean(v, axis=-1, keepdims=True)
    c = v - mu
    var = jnp.mean(c * c, axis=-1, keepdims=True)
    return c * lax.rsqrt(var + LN_EPS) * g + b


def _dot(a, b):
    return jnp.dot(a, b, preferred_element_type=F32)


def _dot_nt(a, b):
    return lax.dot_general(a, b, (((1,), (1,)), ((), ())), preferred_element_type=F32)


def _ln_kernel(x_ref, g_ref, b_ref, o32_ref, o16_ref):
    y = _layernorm(x_ref[...], g_ref[...], b_ref[...])
    o32_ref[...] = y
    o16_ref[...] = y.astype(BF16)


def _ln_call(x, g, b, tm):
    m, d = x.shape
    blk = tm * d * (4 + 4 + 2)
    return pl.pallas_call(
        _ln_kernel,
        grid=(m // tm,),
        in_specs=[pl.BlockSpec((tm, d), lambda i: (i, 0)),
                  pl.BlockSpec((1, d), lambda i: (0, 0)),
                  pl.BlockSpec((1, d), lambda i: (0, 0))],
        out_specs=[pl.BlockSpec((tm, d), lambda i: (i, 0)),
                   pl.BlockSpec((tm, d), lambda i: (i, 0))],
        out_shape=[jax.ShapeDtypeStruct((m, d), F32), jax.ShapeDtypeStruct((m, d), BF16)],
        compiler_params=_params(("parallel",), _vmem_limit(blk, 0, 4 * tm * d * 4)),
        name="ln_in",
    )(x, g.reshape(1, d), b.reshape(1, d))


def _matmul_kernel(x_ref, w_ref, o_ref):
    o_ref[...] = _dot(x_ref[...], w_ref[...]).astype(o_ref.dtype)


def _matmul_call(x, w, tm, tn):
    m, k = x.shape
    n = w.shape[1]
    blk = (tm * k + k * tn + tm * tn) * 2
    return pl.pallas_call(
        _matmul_kernel,
        grid=(m // tm, n // tn),
        in_specs=[pl.BlockSpec((tm, k), lambda i, j: (i, 0)),
                  pl.BlockSpec((k, tn), lambda i, j: (0, j))],
        out_specs=pl.BlockSpec((tm, tn), lambda i, j: (i, j)),
        out_shape=jax.ShapeDtypeStruct((m, n), BF16),
        compiler_params=_params(("parallel", "parallel"), _vmem_limit(blk, 0, 2 * tm * tn * 4)),
        name="in_proj",
    )(x, w)


def _sb_kernel(q_ref, k_ref, v_ref, uu_ref, g_ref, o_ref, acc_ref, carry_ref, *, bq, n_heads, scale):
    i = pl.program_id(2)
    nsub = bq // LANES
    uu = uu_ref[...]
    acc_ref[...] = jnp.zeros_like(acc_ref)
    carry_ref[...] = jnp.zeros_like(carry_ref)
    row = lax.broadcasted_iota(jnp.int32, (bq, bq), 0)
    col = lax.broadcasted_iota(jnp.int32, (bq, bq), 1)
    causal = col < row

    def block(j, masked):
        start = pl.multiple_of(j * bq, bq)
        for hd in range(n_heads):
            lanes = slice(hd * HEAD_DIM, (hd + 1) * HEAD_DIM)
            q = q_ref[0, :, lanes]
            ks = k_ref[0, pl.ds(start, bq), lanes]
            vs = v_ref[0, pl.ds(start, bq), lanes]
            z = _dot_nt(q, ks) * scale
            sp = jnp.maximum(z, 0.0) + jnp.log(1.0 + jnp.exp(-jnp.abs(z)))
            lf = -sp
            if masked:
                lf = jnp.where(causal, lf, 0.0)
            hi = lf.astype(BF16)
            lo = (lf - hi.astype(F32)).astype(BF16)
            carry = carry_ref[hd]
            suffix = [None] * nsub
            for c in reversed(range(nsub)):
                sl = slice(c * LANES, (c + 1) * LANES)
                cs = _dot(jnp.concatenate([hi[:, sl], lo[:, sl]], axis=1), uu)
                suffix[c] = cs[:, :LANES] + carry
                carry = carry + cs[:, LANES:]
            w = jnp.exp((z - sp) + jnp.concatenate(suffix, axis=1))
            if masked:
                w = jnp.where(causal, w, 0.0)
            acc_ref[hd] += _dot(w.astype(BF16), vs)
            carry_ref[hd] = carry

    block(i, True)

    def live(state):
        j, carry_max = state
        return jnp.logical_and(j >= 0, carry_max > F32_EXP_ZERO_BELOW)

    def step(state):
        j, _ = state
        block(j, False)
        m = carry_ref[0]
        for hd in range(1, n_heads):
            m = jnp.maximum(m, carry_ref[hd])
        return j - 1, jnp.max(m)

    lax.while_loop(live, step, (i - 1, jnp.float32(0.0)))

    for hd in range(n_heads):
        out = acc_ref[hd]
        y = out * lax.rsqrt(jnp.mean(out * out, axis=-1, keepdims=True) + RMS_EPS) * g_ref[...]
        o_ref[0, :, hd * HEAD_DIM:(hd + 1) * HEAD_DIM] = y.astype(o_ref.dtype)


def _sb_call(proj, uu, g, n_sb, bq, n_heads):
    b, s, _ = proj.shape
    width = n_heads * HEAD_DIM
    groups = n_sb // n_heads
    blk = (2 * bq * width + 2 * s * width) * 2 + uu.size * 2
    kern = functools.partial(_sb_kernel, bq=bq, n_heads=n_heads, scale=1.0 / math.sqrt(HEAD_DIM))
    return pl.pallas_call(
        kern,
        grid=(b, groups, s // bq),
        in_specs=[pl.BlockSpec((1, bq, width), lambda bi, h, i: (bi, i, h)),
                  pl.BlockSpec((1, s, width), lambda bi, h, i: (bi, 0, groups + h)),
                  pl.BlockSpec((1, s, width), lambda bi, h, i: (bi, 0, 2 * groups + h)),
                  pl.BlockSpec(uu.shape, lambda bi, h, i: (0, 0)),
                  pl.BlockSpec((1, HEAD_DIM), lambda bi, h, i: (0, 0))],
        out_specs=pl.BlockSpec((1, bq, width), lambda bi, h, i: (bi, i, h)),
        out_shape=jax.ShapeDtypeStruct((b, s, n_sb * HEAD_DIM), BF16),
        scratch_shapes=[pltpu.VMEM((n_heads, bq, HEAD_DIM), F32), pltpu.VMEM((n_heads, bq, LANES), F32)],
        compiler_params=_params(("parallel", "parallel", "arbitrary"),
                                _vmem_limit(blk, 2 * n_heads * bq * LANES * 4, 16 * n_heads * bq * bq * 4)),
        name="sb_attn",
    )(proj, proj, proj, uu, g.reshape(1, HEAD_DIM))


def _cumsum_matrix():
    j = jnp.arange(LANES)[:, None]
    s = jnp.arange(LANES)[None, :]
    u = (j > s).astype(BF16)
    blk = jnp.concatenate([u, jnp.ones((LANES, LANES), BF16)], axis=1)
    return jnp.concatenate([blk, blk], axis=0)


def _bias_kernel(rb_ref, o_ref, *, bq):
    h = pl.program_id(0)
    t = lax.broadcasted_iota(jnp.int32, (bq, bq), 0)
    s = lax.broadcasted_iota(jnp.int32, (bq, bq), 1)
    max_exact = N_BUCKETS // 2
    for which in range(2):
        dist = t - s + which * bq
        n = jnp.maximum(dist, 0)
        nf = jnp.maximum(n, 1).astype(F32)
        large = max_exact + (jnp.log(nf / max_exact) / math.log(MAX_DISTANCE / max_exact)
                             * (N_BUCKETS - max_exact)).astype(jnp.int32)
        large = jnp.minimum(large, N_BUCKETS - 1)
        bucket = jnp.where(n < max_exact, n, large)
        bias = jnp.zeros((bq, bq), F32)
        for bkt in range(N_BUCKETS):
            bias = jnp.where(bucket == bkt, rb_ref[bkt, h], bias)
        o_ref[0, which] = bias


def _bias_call(rel_bias, bq):
    n_df = rel_bias.shape[1]
    return pl.pallas_call(
        functools.partial(_bias_kernel, bq=bq),
        grid=(n_df,),
        in_specs=[pl.BlockSpec(memory_space=pltpu.SMEM)],
        out_specs=pl.BlockSpec((1, 2, bq, bq), lambda h: (h, 0, 0, 0)),
        out_shape=jax.ShapeDtypeStruct((n_df, 2, bq, bq), F32),
        compiler_params=_params(("arbitrary",), _vmem_limit(2 * bq * bq * 4, 0, 8 * bq * bq * 4)),
        name="t5_bias",
    )(rel_bias)


def _df_kernel(rb_ref, lq1_ref, lk1_ref, lq2_ref, lk2_ref, q_ref, k_ref, v_ref, bias_ref, g_ref,
               o_ref, m_ref, l_ref, acc_ref, *, bq, scale, lam_init):
    h = pl.program_id(1)
    i = pl.program_id(2)
    q = q_ref[0]
    far_bias = rb_ref[N_BUCKETS - 1, h]
    m_ref[...] = jnp.full_like(m_ref, -jnp.inf)
    l_ref[...] = jnp.zeros_like(l_ref)
    acc_ref[...] = jnp.zeros_like(acc_ref)
    row = lax.broadcasted_iota(jnp.int32, (bq, bq), 0)
    col = lax.broadcasted_iota(jnp.int32, (bq, bq), 1)

    def block(j, mode):
        start = pl.multiple_of(j * bq, bq)
        kk = k_ref[0, pl.ds(start, bq), :]
        vv = v_ref[0, pl.ds(start, bq), :]
        for mp in range(2):
            qm = q[:, mp * HEAD_DIM:(mp + 1) * HEAD_DIM]
            km = kk[:, mp * HEAD_DIM:(mp + 1) * HEAD_DIM]
            s = _dot_nt(qm, km) * scale
            if mode == "far":
                s = s + far_bias
            elif mode == "near":
                s = s + bias_ref[0, 1]
            else:
                s = jnp.where(row >= col, s + bias_ref[0, 0], NEG_BIG)
            m_old = m_ref[mp]
            m_new = jnp.maximum(m_old, jnp.max(s, axis=-1, keepdims=True))
            alpha = jnp.exp(m_old - m_new)
            p = jnp.exp(s - m_new)
            l_ref[mp] = alpha * l_ref[mp] + jnp.sum(p, axis=-1, keepdims=True)
            acc_ref[mp] = alpha * acc_ref[mp] + _dot(p.astype(BF16), vv)
            m_ref[mp] = m_new

    def far_body(j, _):
        block(j, "far")
        return 0

    lax.fori_loop(0, jnp.maximum(i - 1, 0), far_body, 0)

    @pl.when(i >= 1)
    def _():
        block(i - 1, "near")

    block(i, "diag")

    lam = (jnp.exp(jnp.sum(lq1_ref[...] * lk1_ref[...], axis=-1, keepdims=True))
           - jnp.exp(jnp.sum(lq2_ref[...] * lk2_ref[...], axis=-1, keepdims=True)) + lam_init)
    out = acc_ref[0] / l_ref[0] - lam * (acc_ref[1] / l_ref[1])
    y = out * lax.rsqrt(jnp.mean(out * out, axis=-1, keepdims=True) + RMS_EPS) * g_ref[...]
    o_ref[0] = (y * (1.0 - lam_init)).astype(o_ref.dtype)


def _df_call(proj, rel_bias, lam_vecs, bias_tiles, g, n_sb, n_df, bq, lam_init):
    b, s, _ = proj.shape
    dv = 2 * HEAD_DIM
    q_off = 3 * n_sb * HEAD_DIM // dv
    k_off = q_off + n_df
    v_off = k_off + n_df
    blk = (2 * bq * dv + 2 * s * dv) * 2 + 2 * bq * bq * 4
    kern = functools.partial(_df_kernel, bq=bq, scale=1.0 / math.sqrt(HEAD_DIM), lam_init=lam_init)
    vec_spec = pl.BlockSpec((1, HEAD_DIM), lambda bi, h, i: (0, 0))
    return pl.pallas_call(
        kern,
        grid=(b, n_df, s // bq),
        in_specs=[pl.BlockSpec(memory_space=pltpu.SMEM),
                  vec_spec, vec_spec, vec_spec, vec_spec,
                  pl.BlockSpec((1, bq, dv), lambda bi, h, i: (bi, i, q_off + h)),
                  pl.BlockSpec((1, s, dv), lambda bi, h, i: (bi, 0, k_off + h)),
                  pl.BlockSpec((1, s, dv), lambda bi, h, i: (bi, 0, v_off + h)),
                  pl.BlockSpec((1, 2, bq, bq), lambda bi, h, i: (h, 0, 0, 0)),
                  pl.BlockSpec((1, dv), lambda bi, h, i: (0, 0))],
        out_specs=pl.BlockSpec((1, bq, dv), lambda bi, h, i: (bi, i, h)),
        out_shape=jax.ShapeDtypeStruct((b, s, n_df * dv), BF16),
        scratch_shapes=[pltpu.VMEM((2, bq, 1), F32), pltpu.VMEM((2, bq, 1), F32),
                        pltpu.VMEM((2, bq, dv), F32)],
        compiler_params=_params(("parallel", "parallel", "arbitrary"),
                                _vmem_limit(blk, 2 * bq * (dv + 2 * LANES) * 4, 16 * bq * bq * 4)),
        name="df_attn",
    )(rel_bias, *[v.reshape(1, HEAD_DIM) for v in lam_vecs], proj, proj, proj, bias_tiles,
      g.reshape(1, dv))


def _outproj_kernel(sb_ref, df_ref, w_ref, h_ref, g_ref, b_ref, o32_ref, o16_ref, *, alpha):
    half = sb_ref.shape[1]
    mix = _dot(sb_ref[...], w_ref[:half, :]) + _dot(df_ref[...], w_ref[half:, :])
    y = _layernorm(alpha * h_ref[...] + mix, g_ref[...], b_ref[...])
    o32_ref[...] = y
    o16_ref[...] = y.astype(BF16)


def _outproj_call(sb_o, df_o, w, h, g, b, tm, alpha):
    m, d = h.shape
    half = sb_o.shape[1]
    blk = tm * half * 2 * 2 + tm * d * (4 + 4 + 2) + w.size * 2
    row = lambda i: (i, 0)
    fixed = lambda i: (0, 0)
    return pl.pallas_call(
        functools.partial(_outproj_kernel, alpha=alpha),
        grid=(m // tm,),
        in_specs=[pl.BlockSpec((tm, half), row), pl.BlockSpec((tm, half), row),
                  pl.BlockSpec(w.shape, fixed), pl.BlockSpec((tm, d), row),
                  pl.BlockSpec((1, d), fixed), pl.BlockSpec((1, d), fixed)],
        out_specs=[pl.BlockSpec((tm, d), row), pl.BlockSpec((tm, d), row)],
        out_shape=[jax.ShapeDtypeStruct((m, d), F32), jax.ShapeDtypeStruct((m, d), BF16)],
        compiler_params=_params(("parallel",), _vmem_limit(blk, 0, 4 * tm * d * 4)),
        name="out_proj_ln",
    )(sb_o, df_o, w, h, g.reshape(1, d), b.reshape(1, d))


def _mlp_kernel(x16_ref, x32_ref, wu_ref, wd_ref, g_ref, b_ref, o32_ref, o16_ref, acc_ref, *, alpha):
    f = pl.program_id(1)
    u = _dot(x16_ref[...], wu_ref[...])
    a = jnp.square(jnp.maximum(u, 0.0)).astype(BF16)
    part = _dot(a, wd_ref[...])

    @pl.when(f == 0)
    def _():
        acc_ref[...] = part

    @pl.when(f > 0)
    def _():
        acc_ref[...] += part

    @pl.when(f == pl.num_programs(1) - 1)
    def _():
        y = _layernorm(alpha * x32_ref[...] + acc_ref[...], g_ref[...], b_ref[...])
        o32_ref[...] = y
        o16_ref[...] = y.astype(BF16)


def _mlp_call(x16, x32, wu, wd, g, b, tm, tf, alpha):
    m, d = x32.shape
    ff = wu.shape[1]
    blk = tm * d * (2 + 4 + 4 + 2) + 2 * d * tf * 2
    row = lambda i, f: (i, 0)
    fixed = lambda i, f: (0, 0)
    return pl.pallas_call(
        functools.partial(_mlp_kernel, alpha=alpha),
        grid=(m // tm, ff // tf),
        in_specs=[pl.BlockSpec((tm, d), row), pl.BlockSpec((tm, d), row),
                  pl.BlockSpec((d, tf), lambda i, f: (0, f)),
                  pl.BlockSpec((tf, d), lambda i, f: (f, 0)),
                  pl.BlockSpec((1, d), fixed), pl.BlockSpec((1, d), fixed)],
        out_specs=[pl.BlockSpec((tm, d), row), pl.BlockSpec((tm, d), row)],
        out_shape=[jax.ShapeDtypeStruct((m, d), F32), jax.ShapeDtypeStruct((m, d), BF16)],
        scratch_shapes=[pltpu.VMEM((tm, d), F32)],
        compiler_params=_params(("parallel", "arbitrary"),
                                _vmem_limit(blk, tm * d * 4, 2 * tm * tf * 4 + 3 * tm * d * 4)),
        name="mlp_ln",
    )(x16, x32, wu, wd, g.reshape(1, d), b.reshape(1, d))


def _tile(n, want):
    t = min(n, want)
    while n % t or t % LANES:
        t -= LANES
        assert t > 0, (n, want)
    return t


def kernel(x, ln0_g, ln0_b, w_in, w_out, sb_norm_g, lam_q1, lam_k1, lam_q2, lam_k2, diff_norm_g,
           rel_bias, ln1_g, ln1_b, w_up, w_down, ln2_g, ln2_b):
    b, s, d = x.shape
    depth = w_in.shape[0]
    n_sb = d // (2 * HEAD_DIM)
    n_df = d // (4 * HEAD_DIM)
    assert w_in.shape[2] == 3 * n_sb * HEAD_DIM + 3 * n_df * 2 * HEAD_DIM
    assert rel_bias.shape == (N_BUCKETS, n_df)
    alpha = (2 * depth) ** 0.25
    m = b * s

    tm_ln = _tile(m, 512)
    tm_proj, tn_proj = _tile(m, 1024), _tile(w_in.shape[2], 1024)
    tm_out = _tile(m, 512)
    tm_mlp, tf_mlp = _tile(m, 512), _tile(w_up.shape[2], 512)
    bq_sb = _tile(s, 256)
    sb_heads = math.gcd(n_sb, 4)
    bq_df = _tile(s, 256)
    assert bq_df >= MAX_DISTANCE

    w_in16, w_out16 = w_in.astype(BF16), w_out.astype(BF16)
    w_up16, w_down16 = w_up.astype(BF16), w_down.astype(BF16)
    uu = _cumsum_matrix()
    bias_tiles = _bias_call(rel_bias, bq_df)

    h32, h16 = _ln_call(x.reshape(m, d), ln0_g, ln0_b, tm_ln)
    for l in range(depth):
        lam_init = 0.8 - 0.6 * math.exp(-0.3 * l)
        proj = _matmul_call(h16, w_in16[l], tm_proj, tn_proj).reshape(b, s, -1)
        sb_o = _sb_call(proj, uu, sb_norm_g[l], n_sb, bq_sb, sb_heads)
        df_o = _df_call(proj, rel_bias, (lam_q1[l], lam_k1[l], lam_q2[l], lam_k2[l]), bias_tiles,
                        diff_norm_g[l], n_sb, n_df, bq_df, lam_init)
        h32, h16 = _outproj_call(sb_o.reshape(m, -1), df_o.reshape(m, -1), w_out16[l], h32,
                                 ln1_g[l], ln1_b[l], tm_out, alpha)
        h32, h16 = _mlp_call(h16, h32, w_up16[l], w_down16[l], ln2_g[l], ln2_b[l],
                             tm_mlp, tf_mlp, alpha)
    return h32.reshape(b, s, d)
```

```python
import functools
import math

import jax
import jax.numpy as jnp
from jax import lax
from jax.experimental import pallas as pl
from jax.experimental.pallas import tpu as pltpu

HEAD_DIM = 128
N_BUCKETS = 32
MAX_DISTANCE = 128
LN_EPS = 1e-5
RMS_EPS = 1e-5
NEG_BIG = -1e30
F32_EXP_ZERO_BELOW = -104.0

LANES = 128
V7X_VMEM_BYTES = 64 * 1024 * 1024

F32 = jnp.float32
BF16 = jnp.bfloat16


def _vmem_limit(block_bytes, scratch_bytes, temp_bytes):
    need = 2 * block_bytes + scratch_bytes + temp_bytes
    return int(min(max(need, 16 * 1024 * 1024), V7X_VMEM_BYTES - 6 * 1024 * 1024))


def _params(semantics, vmem_bytes):
    return pltpu.CompilerParams(dimension_semantics=semantics, vmem_limit_bytes=vmem_bytes)


def _layernorm(v, g, b):
    mu = jnp.mean(v, axis=-1, keepdims=True)
    c = v - mu
    var = jnp.mean(c * c, axis=-1, keepdims=True)
    return c * lax.rsqrt(var + LN_EPS) * g + b


def _dot(a, b):
    return jnp.dot(a, b, preferred_element_type=F32)


def _dot_nt(a, b):
    return lax.dot_general(a, b, (((1,), (1,)), ((), ())), preferred_element_type=F32)


def _ln_kernel(x_ref, g_ref, b_ref, o32_ref, o16_ref):
    y = _layernorm(x_ref[...], g_ref[...], b_ref[...])
    o32_ref[...] = y
    o16_ref[...] = y.astype(BF16)


def _ln_call(x, g, b, tm):
    m, d = x.shape
    blk = tm * d * (4 + 4 + 2)
    return pl.pallas_call(
        _ln_kernel,
        grid=(m // tm,),
        in_specs=[pl.BlockSpec((tm, d), lambda i: (i, 0)),
                  pl.BlockSpec((1, d), lambda i: (0, 0)),
                  pl.BlockSpec((1, d), lambda i: (0, 0))],
        out_specs=[pl.BlockSpec((tm, d), lambda i: (i, 0)),
                   pl.BlockSpec((tm, d), lambda i: (i, 0))],
        out_shape=[jax.ShapeDtypeStruct((m, d), F32), jax.ShapeDtypeStruct((m, d), BF16)],
        compiler_params=_params(("parallel",), _vmem_limit(blk, 0, 4 * tm * d * 4)),
        name="ln_in",
    )(x, g.reshape(1, d), b.reshape(1, d))


def _matmul_kernel(x_ref, w_ref, o_ref):
    o_ref[...] = _dot(x_ref[...], w_ref[...]).astype(o_ref.dtype)


def _matmul_call(x, w, layer, tm, tn):
    m, k = x.shape
    n = w.shape[2]
    blk = (tm * k + k * tn + tm * tn) * 2
    return pl.pallas_call(
        _matmul_kernel,
        grid=(m // tm, n // tn),
        in_specs=[pl.BlockSpec((tm, k), lambda i, j: (i, 0)),
                  pl.BlockSpec((None, k, tn), lambda i, j: (layer, 0, j))],
        out_specs=pl.BlockSpec((tm, tn), lambda i, j: (i, j)),
        out_shape=jax.ShapeDtypeStruct((m, n), BF16),
        compiler_params=_params(("parallel", "parallel"), _vmem_limit(blk, 0, 2 * tm * tn * 4)),
        name="in_proj",
    )(x, w)


def _sb_kernel(q_ref, k_ref, v_ref, uu_ref, g_ref, o_ref, acc_ref, carry_ref, *, bq, n_heads, scale):
    i = pl.program_id(2)
    nsub = bq // LANES
    uu = uu_ref[...]
    acc_ref[...] = jnp.zeros_like(acc_ref)
    carry_ref[...] = jnp.zeros_like(carry_ref)
    row = lax.broadcasted_iota(jnp.int32, (bq, bq), 0)
    col = lax.broadcasted_iota(jnp.int32, (bq, bq), 1)
    causal = col < row

    def block(j, masked):
        start = pl.multiple_of(j * bq, bq)
        for hd in range(n_heads):
            lanes = slice(hd * HEAD_DIM, (hd + 1) * HEAD_DIM)
            q = q_ref[0, :, lanes]
            ks = k_ref[0, pl.ds(start, bq), lanes]
            vs = v_ref[0, pl.ds(start, bq), lanes]
            z = _dot_nt(q, ks) * scale
            sp = jnp.maximum(z, 0.0) + jnp.log(1.0 + jnp.exp(-jnp.abs(z)))
            lf = -sp
            if masked:
                lf = jnp.where(causal, lf, 0.0)
            hi = lf.astype(BF16)
            lo = (lf - hi.astype(F32)).astype(BF16)
            carry = carry_ref[hd]
            suffix = [None] * nsub
            for c in reversed(range(nsub)):
                sl = slice(c * LANES, (c + 1) * LANES)
                cs = _dot(jnp.concatenate([hi[:, sl], lo[:, sl]], axis=1), uu)
                suffix[c] = cs[:, :LANES] + carry
                carry = carry + cs[:, LANES:]
            w = jnp.exp((z - sp) + jnp.concatenate(suffix, axis=1))
            if masked:
                w = jnp.where(causal, w, 0.0)
            acc_ref[hd] += _dot(w.astype(BF16), vs)
            carry_ref[hd] = carry

    block(i, True)

    def live(state):
        j, carry_max = state
        return jnp.logical_and(j >= 0, carry_max > F32_EXP_ZERO_BELOW)

    def step(state):
        j, _ = state
        block(j, False)
        m = carry_ref[0]
        for hd in range(1, n_heads):
            m = jnp.maximum(m, carry_ref[hd])
        return j - 1, jnp.max(m)

    lax.while_loop(live, step, (i - 1, jnp.float32(0.0)))

    for hd in range(n_heads):
        out = acc_ref[hd]
        y = out * lax.rsqrt(jnp.mean(out * out, axis=-1, keepdims=True) + RMS_EPS) * g_ref[...]
        o_ref[0, :, hd * HEAD_DIM:(hd + 1) * HEAD_DIM] = y.astype(o_ref.dtype)


def _sb_call(proj, uu, g, n_sb, bq, n_heads):
    b, s, _ = proj.shape
    width = n_heads * HEAD_DIM
    groups = n_sb // n_heads
    blk = (2 * bq * width + 2 * s * width) * 2 + uu.size * 2
    kern = functools.partial(_sb_kernel, bq=bq, n_heads=n_heads, scale=1.0 / math.sqrt(HEAD_DIM))
    return pl.pallas_call(
        kern,
        grid=(b, groups, s // bq),
        in_specs=[pl.BlockSpec((1, bq, width), lambda bi, h, i: (bi, i, h)),
                  pl.BlockSpec((1, s, width), lambda bi, h, i: (bi, 0, groups + h)),
                  pl.BlockSpec((1, s, width), lambda bi, h, i: (bi, 0, 2 * groups + h)),
                  pl.BlockSpec(uu.shape, lambda bi, h, i: (0, 0)),
                  pl.BlockSpec((1, HEAD_DIM), lambda bi, h, i: (0, 0))],
        out_specs=pl.BlockSpec((1, bq, width), lambda bi, h, i: (bi, i, h)),
        out_shape=jax.ShapeDtypeStruct((b, s, n_sb * HEAD_DIM), BF16),
        scratch_shapes=[pltpu.VMEM((n_heads, bq, HEAD_DIM), F32), pltpu.VMEM((n_heads, bq, LANES), F32)],
        compiler_params=_params(("parallel", "parallel", "arbitrary"),
                                _vmem_limit(blk, 2 * n_heads * bq * LANES * 4, 16 * n_heads * bq * bq * 4)),
        name="sb_attn",
    )(proj, proj, proj, uu, g.reshape(1, HEAD_DIM))


def _cumsum_matrix():
    j = jnp.arange(LANES)[:, None]
    s = jnp.arange(LANES)[None, :]
    u = (j > s).astype(BF16)
    blk = jnp.concatenate([u, jnp.ones((LANES, LANES), BF16)], axis=1)
    return jnp.concatenate([blk, blk], axis=0)


def _bias_kernel(rb_ref, o_ref, *, bq):
    h = pl.program_id(0)
    t = lax.broadcasted_iota(jnp.int32, (bq, bq), 0)
    s = lax.broadcasted_iota(jnp.int32, (bq, bq), 1)
    max_exact = N_BUCKETS // 2
    for which in range(2):
        dist = t - s + which * bq
        n = jnp.maximum(dist, 0)
        nf = jnp.maximum(n, 1).astype(F32)
        large = max_exact + (jnp.log(nf / max_exact) / math.log(MAX_DISTANCE / max_exact)
                             * (N_BUCKETS - max_exact)).astype(jnp.int32)
        large = jnp.minimum(large, N_BUCKETS - 1)
        bucket = jnp.where(n < max_exact, n, large)
        bias = jnp.zeros((bq, bq), F32)
        for bkt in range(N_BUCKETS):
            bias = jnp.where(bucket == bkt, rb_ref[bkt, h], bias)
        o_ref[0, which] = bias


def _bias_call(rel_bias, bq):
    n_df = rel_bias.shape[1]
    return pl.pallas_call(
        functools.partial(_bias_kernel, bq=bq),
        grid=(n_df,),
        in_specs=[pl.BlockSpec(memory_space=pltpu.SMEM)],
        out_specs=pl.BlockSpec((1, 2, bq, bq), lambda h: (h, 0, 0, 0)),
        out_shape=jax.ShapeDtypeStruct((n_df, 2, bq, bq), F32),
        compiler_params=_params(("arbitrary",), _vmem_limit(2 * bq * bq * 4, 0, 8 * bq * bq * 4)),
        name="t5_bias",
    )(rel_bias)


def _df_kernel(rb_ref, lq1_ref, lk1_ref, lq2_ref, lk2_ref, q_ref, k_ref, v_ref, bias_ref, g_ref,
               o_ref, s_ref, mx_ref, ls_ref, acc_ref, *, bq, scale, lam_init):
    h = pl.program_id(1)
    i = pl.program_id(2)
    far_bias = rb_ref[N_BUCKETS - 1, h]
    mx_ref[...] = jnp.full_like(mx_ref, -jnp.inf)
    ls_ref[...] = jnp.zeros_like(ls_ref)
    acc_ref[...] = jnp.zeros_like(acc_ref)
    row = lax.broadcasted_iota(jnp.int32, (bq, bq), 0)
    col = lax.broadcasted_iota(jnp.int32, (bq, bq), 1)

    def scores(start, width, mode):
        kk = k_ref[0, pl.ds(start, width), :]
        for mp in range(2):
            lanes = slice(mp * HEAD_DIM, (mp + 1) * HEAD_DIM)
            s = _dot_nt(q_ref[0, :, lanes], kk[:, lanes]) * scale
            if mode == "far":
                s = s + far_bias
            elif mode == "near":
                s = s + bias_ref[0, 1]
            else:
                s = jnp.where(row >= col, s + bias_ref[0, 0], NEG_BIG)
            s_ref[mp, :, pl.ds(start, width)] = s
            mx = mx_ref[mp]
            for c in range(width // LANES):
                mx = jnp.maximum(mx, s[:, c * LANES:(c + 1) * LANES])
            mx_ref[mp] = mx

    def values(start, width):
        vv = v_ref[0, pl.ds(start, width), :]
        for mp in range(2):
            s = s_ref[mp, :, pl.ds(start, width)]
            row_max = mx_ref[mp]
            ls = ls_ref[mp]
            ps = []
            for c in range(width // LANES):
                p = jnp.exp(s[:, c * LANES:(c + 1) * LANES] - row_max)
                ls = ls + p
                ps.append(p.astype(BF16))
            ls_ref[mp] = ls
            acc_ref[mp] += _dot(jnp.concatenate(ps, axis=1), vv)

    def looped(n_blocks, fn):
        def body(t, _):
            fn(pl.multiple_of(t * 2 * bq, 2 * bq), 2 * bq)
            return 0

        lax.fori_loop(0, lax.shift_right_logical(n_blocks, 1), body, 0)

        @pl.when((n_blocks & 1) == 1)
        def _():
            fn(pl.multiple_of((n_blocks - 1) * bq, bq), bq)

    looped(jnp.maximum(i - 1, 0), functools.partial(scores, mode="far"))

    @pl.when(i >= 1)
    def _():
        scores(pl.multiple_of((i - 1) * bq, bq), bq, "near")

    scores(pl.multiple_of(i * bq, bq), bq, "diag")

    for mp in range(2):
        m = jnp.max(mx_ref[mp], axis=-1, keepdims=True)
        mx_ref[mp] = jnp.broadcast_to(m, (bq, LANES))

    looped(i + 1, values)

    lam = (jnp.exp(jnp.sum(lq1_ref[...] * lk1_ref[...], axis=-1, keepdims=True))
           - jnp.exp(jnp.sum(lq2_ref[...] * lk2_ref[...], axis=-1, keepdims=True)) + lam_init)
    l0 = jnp.sum(ls_ref[0], axis=-1, keepdims=True)
    l1 = jnp.sum(ls_ref[1], axis=-1, keepdims=True)
    out = acc_ref[0] / l0 - lam * (acc_ref[1] / l1)
    y = out * lax.rsqrt(jnp.mean(out * out, axis=-1, keepdims=True) + RMS_EPS) * g_ref[...]
    o_ref[0] = (y * (1.0 - lam_init)).astype(o_ref.dtype)


def _df_call(proj, rel_bias, lam_vecs, bias_tiles, g, n_sb, n_df, bq, lam_init):
    b, s, _ = proj.shape
    dv = 2 * HEAD_DIM
    q_off = 3 * n_sb * HEAD_DIM // dv
    k_off = q_off + n_df
    v_off = k_off + n_df
    blk = (2 * bq * dv + 2 * s * dv) * 2 + 2 * bq * bq * 4
    kern = functools.partial(_df_kernel, bq=bq, scale=1.0 / math.sqrt(HEAD_DIM), lam_init=lam_init)
    vec_spec = pl.BlockSpec((1, HEAD_DIM), lambda bi, h, i: (0, 0))
    return pl.pallas_call(
        kern,
        grid=(b, n_df, s // bq),
        in_specs=[pl.BlockSpec(memory_space=pltpu.SMEM),
                  vec_spec, vec_spec, vec_spec, vec_spec,
                  pl.BlockSpec((1, bq, dv), lambda bi, h, i: (bi, i, q_off + h)),
                  pl.BlockSpec((1, s, dv), lambda bi, h, i: (bi, 0, k_off + h)),
                  pl.BlockSpec((1, s, dv), lambda bi, h, i: (bi, 0, v_off + h)),
                  pl.BlockSpec((1, 2, bq, bq), lambda bi, h, i: (h, 0, 0, 0)),
                  pl.BlockSpec((1, dv), lambda bi, h, i: (0, 0))],
        out_specs=pl.BlockSpec((1, bq, dv), lambda bi, h, i: (bi, i, h)),
        out_shape=jax.ShapeDtypeStruct((b, s, n_df * dv), BF16),
        scratch_shapes=[pltpu.VMEM((2, bq, s), F32), pltpu.VMEM((2, bq, LANES), F32),
                        pltpu.VMEM((2, bq, LANES), F32), pltpu.VMEM((2, bq, dv), F32)],
        compiler_params=_params(("parallel", "parallel", "arbitrary"),
                                _vmem_limit(blk, 2 * bq * (s + dv + 2 * LANES) * 4, 24 * bq * bq * 4)),
        name="df_attn",
    )(rel_bias, *[v.reshape(1, HEAD_DIM) for v in lam_vecs], proj, proj, proj, bias_tiles,
      g.reshape(1, dv))


def _outproj_kernel(sb_ref, df_ref, w_ref, h_ref, g_ref, b_ref, o32_ref, o16_ref, *, alpha):
    half = sb_ref.shape[1]
    mix = _dot(sb_ref[...], w_ref[:half, :]) + _dot(df_ref[...], w_ref[half:, :])
    y = _layernorm(alpha * h_ref[...] + mix, g_ref[...], b_ref[...])
    o32_ref[...] = y
    o16_ref[...] = y.astype(BF16)


def _outproj_call(sb_o, df_o, w, layer, h, g, b, tm, alpha):
    m, d = h.shape
    half = sb_o.shape[1]
    blk = tm * half * 2 * 2 + tm * d * (4 + 4 + 2) + w.shape[1] * w.shape[2] * 2
    row = lambda i: (i, 0)
    fixed = lambda i: (0, 0)
    return pl.pallas_call(
        functools.partial(_outproj_kernel, alpha=alpha),
        grid=(m // tm,),
        in_specs=[pl.BlockSpec((tm, half), row), pl.BlockSpec((tm, half), row),
                  pl.BlockSpec((None,) + w.shape[1:], lambda i: (layer, 0, 0)),
                  pl.BlockSpec((tm, d), row),
                  pl.BlockSpec((1, d), fixed), pl.BlockSpec((1, d), fixed)],
        out_specs=[pl.BlockSpec((tm, d), row), pl.BlockSpec((tm, d), row)],
        out_shape=[jax.ShapeDtypeStruct((m, d), F32), jax.ShapeDtypeStruct((m, d), BF16)],
        compiler_params=_params(("parallel",), _vmem_limit(blk, 0, 4 * tm * d * 4)),
        name="out_proj_ln",
    )(sb_o, df_o, w, h, g.reshape(1, d), b.reshape(1, d))


def _mlp_kernel(x16_ref, x32_ref, wu_ref, wd_ref, g_ref, b_ref, o32_ref, o16_ref, acc_ref, *, alpha):
    f = pl.program_id(1)
    u = _dot(x16_ref[...], wu_ref[...])
    a = jnp.square(jnp.maximum(u, 0.0)).astype(BF16)
    part = _dot(a, wd_ref[...])

    @pl.when(f == 0)
    def _():
        acc_ref[...] = part

    @pl.when(f > 0)
    def _():
        acc_ref[...] += part

    @pl.when(f == pl.num_programs(1) - 1)
    def _():
        y = _layernorm(alpha * x32_ref[...] + acc_ref[...], g_ref[...], b_ref[...])
        o32_ref[...] = y
        o16_ref[...] = y.astype(BF16)


def _mlp_call(x16, x32, wu, wd, layer, g, b, tm, tf, alpha):
    m, d = x32.shape
    ff = wu.shape[2]
    blk = tm * d * (2 + 4 + 4 + 2) + 2 * d * tf * 2
    row = lambda i, f: (i, 0)
    fixed = lambda i, f: (0, 0)
    return pl.pallas_call(
        functools.partial(_mlp_kernel, alpha=alpha),
        grid=(m // tm, ff // tf),
        in_specs=[pl.BlockSpec((tm, d), row), pl.BlockSpec((tm, d), row),
                  pl.BlockSpec((None, d, tf), lambda i, f: (layer, 0, f)),
                  pl.BlockSpec((None, tf, d), lambda i, f: (layer, f, 0)),
                  pl.BlockSpec((1, d), fixed), pl.BlockSpec((1, d), fixed)],
        out_specs=[pl.BlockSpec((tm, d), row), pl.BlockSpec((tm, d), row)],
        out_shape=[jax.ShapeDtypeStruct((m, d), F32), jax.ShapeDtypeStruct((m, d), BF16)],
        scratch_shapes=[pltpu.VMEM((tm, d), F32)],
        compiler_params=_params(("parallel", "arbitrary"),
                                _vmem_limit(blk, tm * d * 4, 2 * tm * tf * 4 + 3 * tm * d * 4)),
        name="mlp_ln",
    )(x16, x32, wu, wd, g.reshape(1, d), b.reshape(1, d))


def _tile(n, want):
    t = min(n, want)
    while n % t or t % LANES:
        t -= LANES
        assert t > 0, (n, want)
    return t


def kernel(x, ln0_g, ln0_b, w_in, w_out, sb_norm_g, lam_q1, lam_k1, lam_q2, lam_k2, diff_norm_g,
           rel_bias, ln1_g, ln1_b, w_up, w_down, ln2_g, ln2_b):
    b, s, d = x.shape
    depth = w_in.shape[0]
    n_sb = d // (2 * HEAD_DIM)
    n_df = d // (4 * HEAD_DIM)
    assert w_in.shape[2] == 3 * n_sb * HEAD_DIM + 3 * n_df * 2 * HEAD_DIM
    assert rel_bias.shape == (N_BUCKETS, n_df)
    alpha = (2 * depth) ** 0.25
    m = b * s

    tm_ln = _tile(m, 512)
    tm_proj, tn_proj = _tile(m, 1024), _tile(w_in.shape[2], 1024)
    tm_out = _tile(m, 512)
    tm_mlp, tf_mlp = _tile(m, 512), _tile(w_up.shape[2], 1024)
    bq_sb = _tile(s, 256)
    sb_heads = math.gcd(n_sb, 4)
    bq_df = _tile(s, 256)
    assert bq_df >= MAX_DISTANCE

    w_in16, w_out16 = w_in.astype(BF16), w_out.astype(BF16)
    w_up16, w_down16 = w_up.astype(BF16), w_down.astype(BF16)
    uu = _cumsum_matrix()
    bias_tiles = _bias_call(rel_bias, bq_df)

    h32, h16 = _ln_call(x.reshape(m, d), ln0_g, ln0_b, tm_ln)
    for l in range(depth):
        lam_init = 0.8 - 0.6 * math.exp(-0.3 * l)
        proj = _matmul_call(h16, w_in16, l, tm_proj, tn_proj).reshape(b, s, -1)
        sb_o = _sb_call(proj, uu, sb_norm_g[l], n_sb, bq_sb, sb_heads)
        df_o = _df_call(proj, rel_bias, (lam_q1[l], lam_k1[l], lam_q2[l], lam_k2[l]), bias_tiles,
                        diff_norm_g[l], n_sb, n_df, bq_df, lam_init)
        h32, h16 = _outproj_call(sb_o.reshape(m, -1), df_o.reshape(m, -1), w_out16, l, h32,
                                 ln1_g[l], ln1_b[l], tm_out, alpha)
        h32, h16 = _mlp_call(h16, h32, w_up16, w_down16, l, ln2_g[l], ln2_b[l],
                             tm_mlp, tf_mlp, alpha)
    return h32.reshape(b, s, d)
```

```python
import functools
import math

import jax
import jax.numpy as jnp
from jax import lax
from jax.experimental import pallas as pl
from jax.experimental.pallas import tpu as pltpu

HEAD_DIM = 128
N_BUCKETS = 32
MAX_DISTANCE = 128
LN_EPS = 1e-5
RMS_EPS = 1e-5
NEG_BIG = -1e30
F32_EXP2_ZERO_BELOW = -150.5
LOG2_E = 1.4426950408889634

LANES = 128
V7X_VMEM_BYTES = 64 * 1024 * 1024

F32 = jnp.float32
BF16 = jnp.bfloat16


def _vmem_limit(block_bytes, scratch_bytes, temp_bytes):
    need = 2 * block_bytes + scratch_bytes + temp_bytes
    return int(min(max(need, 16 * 1024 * 1024), V7X_VMEM_BYTES - 6 * 1024 * 1024))


def _params(semantics, vmem_bytes):
    return pltpu.CompilerParams(dimension_semantics=semantics, vmem_limit_bytes=vmem_bytes)


def _layernorm(v, g, b):
    mu = jnp.mean(v, axis=-1, keepdims=True)
    c = v - mu
    var = jnp.mean(c * c, axis=-1, keepdims=True)
    return c * lax.rsqrt(var + LN_EPS) * g + b


def _dot(a, b):
    return jnp.dot(a, b, preferred_element_type=F32)


def _dot_nt(a, b):
    return lax.dot_general(a, b, (((1,), (1,)), ((), ())), preferred_element_type=F32)


def _ln_kernel(x_ref, g_ref, b_ref, o32_ref, o16_ref):
    y = _layernorm(x_ref[...], g_ref[...], b_ref[...])
    o32_ref[...] = y
    o16_ref[...] = y.astype(BF16)


def _ln_call(x, g, b, tm):
    m, d = x.shape
    blk = tm * d * (4 + 4 + 2)
    return pl.pallas_call(
        _ln_kernel,
        grid=(m // tm,),
        in_specs=[pl.BlockSpec((tm, d), lambda i: (i, 0)),
                  pl.BlockSpec((1, d), lambda i: (0, 0)),
                  pl.BlockSpec((1, d), lambda i: (0, 0))],
        out_specs=[pl.BlockSpec((tm, d), lambda i: (i, 0)),
                   pl.BlockSpec((tm, d), lambda i: (i, 0))],
        out_shape=[jax.ShapeDtypeStruct((m, d), F32), jax.ShapeDtypeStruct((m, d), BF16)],
        compiler_params=_params(("parallel",), _vmem_limit(blk, 0, 4 * tm * d * 4)),
        name="ln_in",
    )(x, g.reshape(1, d), b.reshape(1, d))


def _matmul_kernel(x_ref, w_ref, o_ref):
    o_ref[...] = _dot(x_ref[...], w_ref[...]).astype(o_ref.dtype)


def _matmul_call(x, w, layer, tm, tn):
    m, k = x.shape
    n = w.shape[2]
    blk = (tm * k + k * tn + tm * tn) * 2
    return pl.pallas_call(
        _matmul_kernel,
        grid=(m // tm, n // tn),
        in_specs=[pl.BlockSpec((tm, k), lambda i, j: (i, 0)),
                  pl.BlockSpec((None, k, tn), lambda i, j: (layer, 0, j))],
        out_specs=pl.BlockSpec((tm, tn), lambda i, j: (i, j)),
        out_shape=jax.ShapeDtypeStruct((m, n), BF16),
        compiler_params=_params(("parallel", "parallel"), _vmem_limit(blk, 0, 2 * tm * tn * 4)),
        name="in_proj",
    )(x, w)


def _sb_kernel(q_ref, k_ref, v_ref, uu_ref, g_ref, o_ref, acc_ref, carry_ref, *, bq, n_heads, scale):
    i = pl.program_id(2)
    nsub = bq // LANES
    uu = uu_ref[...]
    acc_ref[...] = jnp.zeros_like(acc_ref)
    carry_ref[...] = jnp.zeros_like(carry_ref)
    row = lax.broadcasted_iota(jnp.int32, (bq, bq), 0)
    col = lax.broadcasted_iota(jnp.int32, (bq, bq), 1)
    causal = col < row

    def block(j, masked):
        start = pl.multiple_of(j * bq, bq)
        for hd in range(n_heads):
            lanes = slice(hd * HEAD_DIM, (hd + 1) * HEAD_DIM)
            q = q_ref[0, :, lanes]
            ks = k_ref[0, pl.ds(start, bq), lanes]
            vs = v_ref[0, pl.ds(start, bq), lanes]
            z = _dot_nt(q, ks) * (scale * LOG2_E)
            sp = jnp.maximum(z, 0.0) + jnp.log2(1.0 + jnp.exp2(-jnp.abs(z)))
            log2_beta = z - sp
            if masked:
                sp = jnp.where(causal, sp, 0.0)
            hi = sp.astype(BF16)
            lo = (sp - hi.astype(F32)).astype(BF16)
            carry = carry_ref[hd]
            between = [None] * nsub
            for c in reversed(range(nsub)):
                sl = slice(c * LANES, (c + 1) * LANES)
                cs = _dot(jnp.concatenate([hi[:, sl], lo[:, sl]], axis=1), uu)
                between[c] = cs[:, :LANES] + carry
                carry = carry + cs[:, LANES:]
            w = jnp.exp2(log2_beta - jnp.concatenate(between, axis=1))
            if masked:
                w = jnp.where(causal, w, 0.0)
            acc_ref[hd] += _dot(w.astype(BF16), vs)
            carry_ref[hd] = carry

    block(i, True)

    def live(state):
        j, carry_min = state
        return jnp.logical_and(j >= 0, carry_min < -F32_EXP2_ZERO_BELOW)

    def step(state):
        j, _ = state
        block(j, False)
        m = carry_ref[0]
        for hd in range(1, n_heads):
            m = jnp.minimum(m, carry_ref[hd])
        return j - 1, jnp.min(m)

    lax.while_loop(live, step, (i - 1, jnp.float32(0.0)))

    for hd in range(n_heads):
        out = acc_ref[hd]
        y = out * lax.rsqrt(jnp.mean(out * out, axis=-1, keepdims=True) + RMS_EPS) * g_ref[...]
        o_ref[0, :, hd * HEAD_DIM:(hd + 1) * HEAD_DIM] = y.astype(o_ref.dtype)


def _sb_call(proj, uu, g, n_sb, bq, n_heads):
    b, s, _ = proj.shape
    width = n_heads * HEAD_DIM
    groups = n_sb // n_heads
    blk = (2 * bq * width + 2 * s * width) * 2 + uu.size * 2
    kern = functools.partial(_sb_kernel, bq=bq, n_heads=n_heads, scale=1.0 / math.sqrt(HEAD_DIM))
    return pl.pallas_call(
        kern,
        grid=(b, groups, s // bq),
        in_specs=[pl.BlockSpec((1, bq, width), lambda bi, h, i: (bi, i, h)),
                  pl.BlockSpec((1, s, width), lambda bi, h, i: (bi, 0, groups + h)),
                  pl.BlockSpec((1, s, width), lambda bi, h, i: (bi, 0, 2 * groups + h)),
                  pl.BlockSpec(uu.shape, lambda bi, h, i: (0, 0)),
                  pl.BlockSpec((1, HEAD_DIM), lambda bi, h, i: (0, 0))],
        out_specs=pl.BlockSpec((1, bq, width), lambda bi, h, i: (bi, i, h)),
        out_shape=jax.ShapeDtypeStruct((b, s, n_sb * HEAD_DIM), BF16),
        scratch_shapes=[pltpu.VMEM((n_heads, bq, HEAD_DIM), F32), pltpu.VMEM((n_heads, bq, LANES), F32)],
        compiler_params=_params(("parallel", "parallel", "arbitrary"),
                                _vmem_limit(blk, 2 * n_heads * bq * LANES * 4, 16 * n_heads * bq * bq * 4)),
        name="sb_attn",
    )(proj, proj, proj, uu, g.reshape(1, HEAD_DIM))


def _cumsum_matrix():
    j = jnp.arange(LANES)[:, None]
    s = jnp.arange(LANES)[None, :]
    u = (j > s).astype(BF16)
    blk = jnp.concatenate([u, jnp.ones((LANES, LANES), BF16)], axis=1)
    return jnp.concatenate([blk, blk], axis=0)


def _bias_kernel(rb_ref, o_ref, *, bq):
    h = pl.program_id(0)
    t = lax.broadcasted_iota(jnp.int32, (bq, bq), 0)
    s = lax.broadcasted_iota(jnp.int32, (bq, bq), 1)
    max_exact = N_BUCKETS // 2
    for which in range(2):
        dist = t - s + which * bq
        n = jnp.maximum(dist, 0)
        nf = jnp.maximum(n, 1).astype(F32)
        large = max_exact + (jnp.log(nf / max_exact) / math.log(MAX_DISTANCE / max_exact)
                             * (N_BUCKETS - max_exact)).astype(jnp.int32)
        large = jnp.minimum(large, N_BUCKETS - 1)
        bucket = jnp.where(n < max_exact, n, large)
        bias = jnp.zeros((bq, bq), F32)
        for bkt in range(N_BUCKETS):
            bias = jnp.where(bucket == bkt, rb_ref[bkt, h] * LOG2_E, bias)
        o_ref[0, which] = bias


def _bias_call(rel_bias, bq):
    n_df = rel_bias.shape[1]
    return pl.pallas_call(
        functools.partial(_bias_kernel, bq=bq),
        grid=(n_df,),
        in_specs=[pl.BlockSpec(memory_space=pltpu.SMEM)],
        out_specs=pl.BlockSpec((1, 2, bq, bq), lambda h: (h, 0, 0, 0)),
        out_shape=jax.ShapeDtypeStruct((n_df, 2, bq, bq), F32),
        compiler_params=_params(("arbitrary",), _vmem_limit(2 * bq * bq * 4, 0, 8 * bq * bq * 4)),
        name="t5_bias",
    )(rel_bias)


def _df_kernel(rb_ref, lq1_ref, lk1_ref, lq2_ref, lk2_ref, q_ref, k_ref, v_ref, bias_ref, g_ref,
               o_ref, s_ref, mx_ref, ls_ref, acc_ref, *, bq, n_heads, scale, lam_init):
    hg = pl.program_id(1)
    i = pl.program_id(2)
    dv = 2 * HEAD_DIM
    mx_ref[...] = jnp.full_like(mx_ref, -jnp.inf)
    ls_ref[...] = jnp.zeros_like(ls_ref)
    acc_ref[...] = jnp.zeros_like(acc_ref)
    row = lax.broadcasted_iota(jnp.int32, (bq, bq), 0)
    col = lax.broadcasted_iota(jnp.int32, (bq, bq), 1)

    def scores(start, width, mode):
        for hd in range(n_heads):
            far_bias = rb_ref[N_BUCKETS - 1, hg * n_heads + hd] * LOG2_E
            for mp in range(2):
                u = 2 * hd + mp
                lanes = slice(u * HEAD_DIM, (u + 1) * HEAD_DIM)
                s = _dot_nt(q_ref[0, :, lanes], k_ref[0, pl.ds(start, width), lanes]) * (scale * LOG2_E)
                if mode == "far":
                    s = s + far_bias
                elif mode == "near":
                    s = s + bias_ref[hd, 1]
                else:
                    s = jnp.where(row >= col, s + bias_ref[hd, 0], NEG_BIG)
                s_ref[u, :, pl.ds(start, width)] = s
                mx = mx_ref[u]
                for c in range(width // LANES):
                    mx = jnp.maximum(mx, s[:, c * LANES:(c + 1) * LANES])
                mx_ref[u] = mx

    def values(start, width):
        for hd in range(n_heads):
            vv = v_ref[0, pl.ds(start, width), hd * dv:(hd + 1) * dv]
            for mp in range(2):
                u = 2 * hd + mp
                s = s_ref[u, :, pl.ds(start, width)]
                row_max = mx_ref[u]
                ls = ls_ref[u]
                ps = []
                for c in range(width // LANES):
                    p = jnp.exp2(s[:, c * LANES:(c + 1) * LANES] - row_max)
                    ls = ls + p
                    ps.append(p.astype(BF16))
                ls_ref[u] = ls
                acc_ref[u] += _dot(jnp.concatenate(ps, axis=1), vv)

    def looped(n_blocks, fn):
        def body(t, _):
            fn(pl.multiple_of(t * 2 * bq, 2 * bq), 2 * bq)
            return 0

        lax.fori_loop(0, lax.shift_right_logical(n_blocks, 1), body, 0)

        @pl.when((n_blocks & 1) == 1)
        def _():
            fn(pl.multiple_of((n_blocks - 1) * bq, bq), bq)

    looped(jnp.maximum(i - 1, 0), functools.partial(scores, mode="far"))

    @pl.when(i >= 1)
    def _():
        scores(pl.multiple_of((i - 1) * bq, bq), bq, "near")

    scores(pl.multiple_of(i * bq, bq), bq, "diag")

    for u in range(2 * n_heads):
        m = jnp.max(mx_ref[u], axis=-1, keepdims=True)
        mx_ref[u] = jnp.broadcast_to(m, (bq, LANES))

    looped(i + 1, values)

    lam = (jnp.exp(jnp.sum(lq1_ref[...] * lk1_ref[...], axis=-1, keepdims=True))
           - jnp.exp(jnp.sum(lq2_ref[...] * lk2_ref[...], axis=-1, keepdims=True)) + lam_init)
    for hd in range(n_heads):
        l0 = jnp.sum(ls_ref[2 * hd], axis=-1, keepdims=True)
        l1 = jnp.sum(ls_ref[2 * hd + 1], axis=-1, keepdims=True)
        out = acc_ref[2 * hd] / l0 - lam * (acc_ref[2 * hd + 1] / l1)
        y = out * lax.rsqrt(jnp.mean(out * out, axis=-1, keepdims=True) + RMS_EPS) * g_ref[...]
        o_ref[0, :, hd * dv:(hd + 1) * dv] = (y * (1.0 - lam_init)).astype(o_ref.dtype)


def _df_call(proj, rel_bias, lam_vecs, bias_tiles, g, n_sb, n_df, bq, n_heads, lam_init):
    b, s, _ = proj.shape
    dv = 2 * HEAD_DIM
    width = n_heads * dv
    groups = n_df // n_heads
    q_start = 3 * n_sb * HEAD_DIM
    assert q_start % width == 0 and n_df % n_heads == 0
    q_off = q_start // width
    k_off = q_off + groups
    v_off = k_off + groups
    blk = (2 * bq * width + 2 * s * width) * 2 + n_heads * 2 * bq * bq * 4
    kern = functools.partial(_df_kernel, bq=bq, n_heads=n_heads, scale=1.0 / math.sqrt(HEAD_DIM),
                             lam_init=lam_init)
    vec_spec = pl.BlockSpec((1, HEAD_DIM), lambda bi, h, i: (0, 0))
    n_soft = 2 * n_heads
    return pl.pallas_call(
        kern,
        grid=(b, groups, s // bq),
        in_specs=[pl.BlockSpec(memory_space=pltpu.SMEM),
                  vec_spec, vec_spec, vec_spec, vec_spec,
                  pl.BlockSpec((1, bq, width), lambda bi, h, i: (bi, i, q_off + h)),
                  pl.BlockSpec((1, s, width), lambda bi, h, i: (bi, 0, k_off + h)),
                  pl.BlockSpec((1, s, width), lambda bi, h, i: (bi, 0, v_off + h)),
                  pl.BlockSpec((n_heads, 2, bq, bq), lambda bi, h, i: (h, 0, 0, 0)),
                  pl.BlockSpec((1, dv), lambda bi, h, i: (0, 0))],
        out_specs=pl.BlockSpec((1, bq, width), lambda bi, h, i: (bi, i, h)),
        out_shape=jax.ShapeDtypeStruct((b, s, n_df * dv), BF16),
        scratch_shapes=[pltpu.VMEM((n_soft, bq, s), F32), pltpu.VMEM((n_soft, bq, LANES), F32),
                        pltpu.VMEM((n_soft, bq, LANES), F32), pltpu.VMEM((n_soft, bq, dv), F32)],
        compiler_params=_params(("parallel", "parallel", "arbitrary"),
                                _vmem_limit(blk, n_soft * bq * (s + dv + 2 * LANES) * 4,
                                            16 * n_heads * bq * bq * 4)),
        name="df_attn",
    )(rel_bias, *[v.reshape(1, HEAD_DIM) for v in lam_vecs], proj, proj, proj, bias_tiles,
      g.reshape(1, dv))


def _outproj_kernel(sb_ref, df_ref, w_ref, h_ref, g_ref, b_ref, o32_ref, o16_ref, *, alpha):
    tm, half = sb_ref.shape
    sub = tm // 4 if tm % 32 == 0 else tm
    for r in range(tm // sub):
        rows = slice(r * sub, (r + 1) * sub)
        mix = _dot(sb_ref[rows, :], w_ref[:half, :]) + _dot(df_ref[rows, :], w_ref[half:, :])
        y = _layernorm(alpha * h_ref[rows, :] + mix, g_ref[...], b_ref[...])
        o32_ref[rows, :] = y
        o16_ref[rows, :] = y.astype(BF16)


def _outproj_call(sb_o, df_o, w, layer, h, g, b, tm, alpha):
    m, d = h.shape
    half = sb_o.shape[1]
    blk = tm * half * 2 * 2 + tm * d * (4 + 4 + 2) + w.shape[1] * w.shape[2] * 2
    row = lambda i: (i, 0)
    fixed = lambda i: (0, 0)
    return pl.pallas_call(
        functools.partial(_outproj_kernel, alpha=alpha),
        grid=(m // tm,),
        in_specs=[pl.BlockSpec((tm, half), row), pl.BlockSpec((tm, half), row),
                  pl.BlockSpec((None,) + w.shape[1:], lambda i: (layer, 0, 0)),
                  pl.BlockSpec((tm, d), row),
                  pl.BlockSpec((1, d), fixed), pl.BlockSpec((1, d), fixed)],
        out_specs=[pl.BlockSpec((tm, d), row), pl.BlockSpec((tm, d), row)],
        out_shape=[jax.ShapeDtypeStruct((m, d), F32), jax.ShapeDtypeStruct((m, d), BF16)],
        compiler_params=_params(("parallel",), _vmem_limit(blk, 0, 4 * tm * d * 4)),
        name="out_proj_ln",
    )(sb_o, df_o, w, h, g.reshape(1, d), b.reshape(1, d))


def _mlp_kernel(x16_ref, x32_ref, wu_ref, wd_ref, g_ref, b_ref, o32_ref, o16_ref, acc_ref, *, alpha):
    f = pl.program_id(1)
    u = _dot(x16_ref[...], wu_ref[...])
    a = jnp.square(jnp.maximum(u, 0.0)).astype(BF16)
    part = _dot(a, wd_ref[...])

    @pl.when(f == 0)
    def _():
        acc_ref[...] = part

    @pl.when(f > 0)
    def _():
        acc_ref[...] += part

    @pl.when(f == pl.num_programs(1) - 1)
    def _():
        y = _layernorm(alpha * x32_ref[...] + acc_ref[...], g_ref[...], b_ref[...])
        o32_ref[...] = y
        o16_ref[...] = y.astype(BF16)


def _mlp_call(x16, x32, wu, wd, layer, g, b, tm, tf, alpha):
    m, d = x32.shape
    ff = wu.shape[2]
    blk = tm * d * (2 + 4 + 4 + 2) + 2 * d * tf * 2
    row = lambda i, f: (i, 0)
    fixed = lambda i, f: (0, 0)
    return pl.pallas_call(
        functools.partial(_mlp_kernel, alpha=alpha),
        grid=(m // tm, ff // tf),
        in_specs=[pl.BlockSpec((tm, d), row), pl.BlockSpec((tm, d), row),
                  pl.BlockSpec((None, d, tf), lambda i, f: (layer, 0, f)),
                  pl.BlockSpec((None, tf, d), lambda i, f: (layer, f, 0)),
                  pl.BlockSpec((1, d), fixed), pl.BlockSpec((1, d), fixed)],
        out_specs=[pl.BlockSpec((tm, d), row), pl.BlockSpec((tm, d), row)],
        out_shape=[jax.ShapeDtypeStruct((m, d), F32), jax.ShapeDtypeStruct((m, d), BF16)],
        scratch_shapes=[pltpu.VMEM((tm, d), F32)],
        compiler_params=_params(("parallel", "arbitrary"),
                                _vmem_limit(blk, tm * d * 4, 2 * tm * tf * 4 + 3 * tm * d * 4)),
        name="mlp_ln",
    )(x16, x32, wu, wd, g.reshape(1, d), b.reshape(1, d))


def _tile(n, want):
    t = min(n, want)
    while n % t or t % LANES:
        t -= LANES
        assert t > 0, (n, want)
    return t


def kernel(x, ln0_g, ln0_b, w_in, w_out, sb_norm_g, lam_q1, lam_k1, lam_q2, lam_k2, diff_norm_g,
           rel_bias, ln1_g, ln1_b, w_up, w_down, ln2_g, ln2_b):
    b, s, d = x.shape
    depth = w_in.shape[0]
    n_sb = d // (2 * HEAD_DIM)
    n_df = d // (4 * HEAD_DIM)
    assert w_in.shape[2] == 3 * n_sb * HEAD_DIM + 3 * n_df * 2 * HEAD_DIM
    assert rel_bias.shape == (N_BUCKETS, n_df)
    alpha = (2 * depth) ** 0.25
    m = b * s

    tm_ln = _tile(m, 512)
    tm_proj, tn_proj = _tile(m, 1024), _tile(w_in.shape[2], 1024)
    tm_out = _tile(m, 512)
    tm_mlp, tf_mlp = _tile(m, 512), _tile(w_up.shape[2], 1024)
    bq_sb = _tile(s, 256)
    sb_heads = math.gcd(n_sb, 8)
    bq_df = _tile(s, 256)
    df_heads = math.gcd(n_df, 2)
    assert bq_df >= MAX_DISTANCE

    w_in16, w_out16 = w_in.astype(BF16), w_out.astype(BF16)
    w_up16, w_down16 = w_up.astype(BF16), w_down.astype(BF16)
    uu = _cumsum_matrix()
    bias_tiles = _bias_call(rel_bias, bq_df)

    h32, h16 = _ln_call(x.reshape(m, d), ln0_g, ln0_b, tm_ln)
    for l in range(depth):
        lam_init = 0.8 - 0.6 * math.exp(-0.3 * l)
        proj = _matmul_call(h16, w_in16, l, tm_proj, tn_proj).reshape(b, s, -1)
        sb_o = _sb_call(proj, uu, sb_norm_g[l], n_sb, bq_sb, sb_heads)
        df_o = _df_call(proj, rel_bias, (lam_q1[l], lam_k1[l], lam_q2[l], lam_k2[l]), bias_tiles,
                        diff_norm_g[l], n_sb, n_df, bq_df, df_heads, lam_init)
        h32, h16 = _outproj_call(sb_o.reshape(m, -1), df_o.reshape(m, -1), w_out16, l, h32,
                                 ln1_g[l], ln1_b[l], tm_out, alpha)
        h32, h16 = _mlp_call(h16, h32, w_up16, w_down16, l, ln2_g[l], ln2_b[l],
                             tm_mlp, tf_mlp, alpha)
    return h32.reshape(b, s, d)
```

```python
import functools
import math

import jax
import jax.numpy as jnp
from jax import lax
from jax.experimental import pallas as pl
from jax.experimental.pallas import tpu as pltpu

HEAD_DIM = 128
N_BUCKETS = 32
MAX_DISTANCE = 128
LN_EPS = 1e-5
RMS_EPS = 1e-5
NEG_BIG = -1e30
F32_EXP2_ZERO_BELOW = -150.5
LOG2_E = 1.4426950408889634

LANES = 128
V7X_VMEM_BYTES = 64 * 1024 * 1024

F32 = jnp.float32
BF16 = jnp.bfloat16


def _vmem_limit(block_bytes, scratch_bytes, temp_bytes):
    need = 2 * block_bytes + scratch_bytes + temp_bytes
    return int(min(max(need, 16 * 1024 * 1024), V7X_VMEM_BYTES - 6 * 1024 * 1024))


def _params(semantics, vmem_bytes):
    return pltpu.CompilerParams(dimension_semantics=semantics, vmem_limit_bytes=vmem_bytes)


def _layernorm(v, g, b):
    mu = jnp.mean(v, axis=-1, keepdims=True)
    c = v - mu
    var = jnp.mean(c * c, axis=-1, keepdims=True)
    return c * lax.rsqrt(var + LN_EPS) * g + b


def _dot(a, b):
    return jnp.dot(a, b, preferred_element_type=F32)


def _dot_nt(a, b):
    return lax.dot_general(a, b, (((1,), (1,)), ((), ())), preferred_element_type=F32)


def _ln_kernel(x_ref, g_ref, b_ref, o32_ref, o16_ref):
    y = _layernorm(x_ref[...], g_ref[...], b_ref[...])
    o32_ref[...] = y
    o16_ref[...] = y.astype(BF16)


def _ln_call(x, g, b, tm):
    m, d = x.shape
    blk = tm * d * (4 + 4 + 2)
    return pl.pallas_call(
        _ln_kernel,
        grid=(m // tm,),
        in_specs=[pl.BlockSpec((tm, d), lambda i: (i, 0)),
                  pl.BlockSpec((1, d), lambda i: (0, 0)),
                  pl.BlockSpec((1, d), lambda i: (0, 0))],
        out_specs=[pl.BlockSpec((tm, d), lambda i: (i, 0)),
                   pl.BlockSpec((tm, d), lambda i: (i, 0))],
        out_shape=[jax.ShapeDtypeStruct((m, d), F32), jax.ShapeDtypeStruct((m, d), BF16)],
        compiler_params=_params(("parallel",), _vmem_limit(blk, 0, 4 * tm * d * 4)),
        name="ln_in",
    )(x, g.reshape(1, d), b.reshape(1, d))


def _matmul_kernel(x_ref, w_ref, o_ref, w16_ref):
    @pl.when(pl.program_id(1) == 0)
    def _():
        w16_ref[...] = w_ref[...].astype(BF16)

    o_ref[...] = _dot(x_ref[...], w16_ref[...]).astype(o_ref.dtype)


def _matmul_call(x, w, layer, tm, tn):
    m, k = x.shape
    n = w.shape[2]
    blk = (tm * k + tm * tn) * 2 + k * tn * 4
    return pl.pallas_call(
        _matmul_kernel,
        grid=(n // tn, m // tm),
        in_specs=[pl.BlockSpec((tm, k), lambda j, i: (i, 0)),
                  pl.BlockSpec((None, k, tn), lambda j, i: (layer, 0, j))],
        out_specs=pl.BlockSpec((tm, tn), lambda j, i: (i, j)),
        out_shape=jax.ShapeDtypeStruct((m, n), BF16),
        scratch_shapes=[pltpu.VMEM((k, tn), BF16)],
        compiler_params=_params(("arbitrary", "arbitrary"),
                                _vmem_limit(blk, k * tn * 2, 2 * tm * tn * 4)),
        name="in_proj",
    )(x, w)


def _sb_kernel(q_ref, k_ref, v_ref, uu_ref, g_ref, o_ref, acc_ref, carry_ref, *, bq, n_heads, scale):
    i = pl.program_id(2)
    nsub = bq // LANES
    uu = uu_ref[...]
    acc_ref[...] = jnp.zeros_like(acc_ref)
    carry_ref[...] = jnp.zeros_like(carry_ref)
    row = lax.broadcasted_iota(jnp.int32, (bq, bq), 0)
    col = lax.broadcasted_iota(jnp.int32, (bq, bq), 1)
    causal = col < row

    def block(j, masked):
        start = pl.multiple_of(j * bq, bq)
        for hd in range(n_heads):
            lanes = slice(hd * HEAD_DIM, (hd + 1) * HEAD_DIM)
            q = q_ref[0, :, lanes]
            ks = k_ref[0, pl.ds(start, bq), lanes]
            vs = v_ref[0, pl.ds(start, bq), lanes]
            z = _dot_nt(q, ks) * (scale * LOG2_E)
            sp = jnp.maximum(z, 0.0) + jnp.log2(1.0 + jnp.exp2(-jnp.abs(z)))
            log2_beta = z - sp
            if masked:
                sp = jnp.where(causal, sp, 0.0)
            hi = sp.astype(BF16)
            lo = (sp - hi.astype(F32)).astype(BF16)
            carry = carry_ref[hd]
            between = [None] * nsub
            for c in reversed(range(nsub)):
                sl = slice(c * LANES, (c + 1) * LANES)
                cs = _dot(jnp.concatenate([hi[:, sl], lo[:, sl]], axis=1), uu)
                between[c] = cs[:, :LANES] + carry
                carry = carry + cs[:, LANES:]
            w = jnp.exp2(log2_beta - jnp.concatenate(between, axis=1))
            if masked:
                w = jnp.where(causal, w, 0.0)
            acc_ref[hd] += _dot(w.astype(BF16), vs)
            carry_ref[hd] = carry

    block(i, True)

    def live(state):
        j, carry_min = state
        return jnp.logical_and(j >= 0, carry_min < -F32_EXP2_ZERO_BELOW)

    def step(state):
        j, _ = state
        block(j, False)
        m = carry_ref[0]
        for hd in range(1, n_heads):
            m = jnp.minimum(m, carry_ref[hd])
        return j - 1, jnp.min(m)

    lax.while_loop(live, step, (i - 1, jnp.float32(0.0)))

    for hd in range(n_heads):
        out = acc_ref[hd]
        y = out * lax.rsqrt(jnp.mean(out * out, axis=-1, keepdims=True) + RMS_EPS) * g_ref[...]
        o_ref[0, :, hd * HEAD_DIM:(hd + 1) * HEAD_DIM] = y.astype(o_ref.dtype)


def _sb_call(proj, uu, g, n_sb, bq, n_heads):
    b, s, _ = proj.shape
    width = n_heads * HEAD_DIM
    groups = n_sb // n_heads
    blk = (2 * bq * width + 2 * s * width) * 2 + uu.size * 2
    kern = functools.partial(_sb_kernel, bq=bq, n_heads=n_heads, scale=1.0 / math.sqrt(HEAD_DIM))
    return pl.pallas_call(
        kern,
        grid=(b, groups, s // bq),
        in_specs=[pl.BlockSpec((1, bq, width), lambda bi, h, i: (bi, i, h)),
                  pl.BlockSpec((1, s, width), lambda bi, h, i: (bi, 0, groups + h)),
                  pl.BlockSpec((1, s, width), lambda bi, h, i: (bi, 0, 2 * groups + h)),
                  pl.BlockSpec(uu.shape, lambda bi, h, i: (0, 0)),
                  pl.BlockSpec((1, HEAD_DIM), lambda bi, h, i: (0, 0))],
        out_specs=pl.BlockSpec((1, bq, width), lambda bi, h, i: (bi, i, h)),
        out_shape=jax.ShapeDtypeStruct((b, s, n_sb * HEAD_DIM), BF16),
        scratch_shapes=[pltpu.VMEM((n_heads, bq, HEAD_DIM), F32), pltpu.VMEM((n_heads, bq, LANES), F32)],
        compiler_params=_params(("parallel", "parallel", "arbitrary"),
                                _vmem_limit(blk, 2 * n_heads * bq * LANES * 4, 16 * n_heads * bq * bq * 4)),
        name="sb_attn",
    )(proj, proj, proj, uu, g.reshape(1, HEAD_DIM))


def _cumsum_matrix():
    j = jnp.arange(LANES)[:, None]
    s = jnp.arange(LANES)[None, :]
    u = (j > s).astype(BF16)
    blk = jnp.concatenate([u, jnp.ones((LANES, LANES), BF16)], axis=1)
    return jnp.concatenate([blk, blk], axis=0)


def _bias_kernel(rb_ref, o_ref, *, bq):
    h = pl.program_id(0)
    t = lax.broadcasted_iota(jnp.int32, (bq, bq), 0)
    s = lax.broadcasted_iota(jnp.int32, (bq, bq), 1)
    max_exact = N_BUCKETS // 2
    far = rb_ref[N_BUCKETS - 1, h]
    for which in range(2):
        dist = t - s + which * bq
        n = jnp.maximum(dist, 0)
        nf = jnp.maximum(n, 1).astype(F32)
        large = max_exact + (jnp.log(nf / max_exact) / math.log(MAX_DISTANCE / max_exact)
                             * (N_BUCKETS - max_exact)).astype(jnp.int32)
        large = jnp.minimum(large, N_BUCKETS - 1)
        bucket = jnp.where(n < max_exact, n, large)
        bias = jnp.zeros((bq, bq), F32)
        for bkt in range(N_BUCKETS):
            bias = jnp.where(bucket == bkt, (rb_ref[bkt, h] - far) * LOG2_E, bias)
        o_ref[0, which] = bias


def _bias_call(rel_bias, bq):
    n_df = rel_bias.shape[1]
    return pl.pallas_call(
        functools.partial(_bias_kernel, bq=bq),
        grid=(n_df,),
        in_specs=[pl.BlockSpec(memory_space=pltpu.SMEM)],
        out_specs=pl.BlockSpec((1, 2, bq, bq), lambda h: (h, 0, 0, 0)),
        out_shape=jax.ShapeDtypeStruct((n_df, 2, bq, bq), F32),
        compiler_params=_params(("arbitrary",), _vmem_limit(2 * bq * bq * 4, 0, 8 * bq * bq * 4)),
        name="t5_bias",
    )(rel_bias)


def _df_kernel(lq1_ref, lk1_ref, lq2_ref, lk2_ref, q_ref, k_ref, v_ref, bias_ref, g_ref,
               o_ref, s_ref, mx_ref, ls_ref, acc_ref, *, bq, n_heads, scale, lam_init):
    i = pl.program_id(2)
    dv = 2 * HEAD_DIM
    mx_ref[...] = jnp.full_like(mx_ref, -jnp.inf)
    ls_ref[...] = jnp.zeros_like(ls_ref)
    acc_ref[...] = jnp.zeros_like(acc_ref)
    row = lax.broadcasted_iota(jnp.int32, (bq, bq), 0)
    col = lax.broadcasted_iota(jnp.int32, (bq, bq), 1)

    def scores(start, width, mode):
        for hd in range(n_heads):
            for mp in range(2):
                u = 2 * hd + mp
                lanes = slice(u * HEAD_DIM, (u + 1) * HEAD_DIM)
                s = _dot_nt(q_ref[0, :, lanes], k_ref[0, pl.ds(start, width), lanes]) * (scale * LOG2_E)
                if mode == "near":
                    s = s + bias_ref[hd, 1]
                elif mode == "diag":
                    s = jnp.where(row >= col, s + bias_ref[hd, 0], NEG_BIG)
                s_ref[u, :, pl.ds(start, width)] = s
                mx = mx_ref[u]
                for c in range(width // LANES):
                    mx = jnp.maximum(mx, s[:, c * LANES:(c + 1) * LANES])
                mx_ref[u] = mx

    def values(start, width):
        for hd in range(n_heads):
            vv = v_ref[0, pl.ds(start, width), hd * dv:(hd + 1) * dv]
            for mp in range(2):
                u = 2 * hd + mp
                s = s_ref[u, :, pl.ds(start, width)]
                row_max = mx_ref[u]
                ls = ls_ref[u]
                ps = []
                for c in range(width // LANES):
                    p = jnp.exp2(s[:, c * LANES:(c + 1) * LANES] - row_max)
                    ls = ls + p
                    ps.append(p.astype(BF16))
                ls_ref[u] = ls
                acc_ref[u] += _dot(jnp.concatenate(ps, axis=1), vv)

    def looped(n_blocks, fn):
        def body(t, _):
            fn(pl.multiple_of(t * 2 * bq, 2 * bq), 2 * bq)
            return 0

        lax.fori_loop(0, lax.shift_right_logical(n_blocks, 1), body, 0)

        @pl.when((n_blocks & 1) == 1)
        def _():
            fn(pl.multiple_of((n_blocks - 1) * bq, bq), bq)

    looped(jnp.maximum(i - 1, 0), functools.partial(scores, mode="far"))

    @pl.when(i >= 1)
    def _():
        scores(pl.multiple_of((i - 1) * bq, bq), bq, "near")

    scores(pl.multiple_of(i * bq, bq), bq, "diag")

    for u in range(2 * n_heads):
        m = jnp.max(mx_ref[u], axis=-1, keepdims=True)
        mx_ref[u] = jnp.broadcast_to(m, (bq, LANES))

    looped(i + 1, values)

    lam = (jnp.exp(jnp.sum(lq1_ref[...] * lk1_ref[...], axis=-1, keepdims=True))
           - jnp.exp(jnp.sum(lq2_ref[...] * lk2_ref[...], axis=-1, keepdims=True)) + lam_init)
    for hd in range(n_heads):
        l0 = jnp.sum(ls_ref[2 * hd], axis=-1, keepdims=True)
        l1 = jnp.sum(ls_ref[2 * hd + 1], axis=-1, keepdims=True)
        out = acc_ref[2 * hd] / l0 - lam * (acc_ref[2 * hd + 1] / l1)
        y = out * lax.rsqrt(jnp.mean(out * out, axis=-1, keepdims=True) + RMS_EPS) * g_ref[...]
        o_ref[0, :, hd * dv:(hd + 1) * dv] = (y * (1.0 - lam_init)).astype(o_ref.dtype)


def _df_call(proj, lam_vecs, bias_tiles, g, n_sb, n_df, bq, n_heads, lam_init):
    b, s, _ = proj.shape
    dv = 2 * HEAD_DIM
    width = n_heads * dv
    groups = n_df // n_heads
    q_start = 3 * n_sb * HEAD_DIM
    assert q_start % width == 0 and n_df % n_heads == 0
    q_off = q_start // width
    k_off = q_off + groups
    v_off = k_off + groups
    blk = (2 * bq * width + 2 * s * width) * 2 + n_heads * 2 * bq * bq * 4
    kern = functools.partial(_df_kernel, bq=bq, n_heads=n_heads, scale=1.0 / math.sqrt(HEAD_DIM),
                             lam_init=lam_init)
    vec_spec = pl.BlockSpec((1, HEAD_DIM), lambda bi, h, i: (0, 0))
    n_soft = 2 * n_heads
    return pl.pallas_call(
        kern,
        grid=(b, groups, s // bq),
        in_specs=[vec_spec, vec_spec, vec_spec, vec_spec,
                  pl.BlockSpec((1, bq, width), lambda bi, h, i: (bi, i, q_off + h)),
                  pl.BlockSpec((1, s, width), lambda bi, h, i: (bi, 0, k_off + h)),
                  pl.BlockSpec((1, s, width), lambda bi, h, i: (bi, 0, v_off + h)),
                  pl.BlockSpec((n_heads, 2, bq, bq), lambda bi, h, i: (h, 0, 0, 0)),
                  pl.BlockSpec((1, dv), lambda bi, h, i: (0, 0))],
        out_specs=pl.BlockSpec((1, bq, width), lambda bi, h, i: (bi, i, h)),
        out_shape=jax.ShapeDtypeStruct((b, s, n_df * dv), BF16),
        scratch_shapes=[pltpu.VMEM((n_soft, bq, s), F32), pltpu.VMEM((n_soft, bq, LANES), F32),
                        pltpu.VMEM((n_soft, bq, LANES), F32), pltpu.VMEM((n_soft, bq, dv), F32)],
        compiler_params=_params(("parallel", "parallel", "arbitrary"),
                                _vmem_limit(blk, n_soft * bq * (s + dv + 2 * LANES) * 4,
                                            16 * n_heads * bq * bq * 4)),
        name="df_attn",
    )(*[v.reshape(1, HEAD_DIM) for v in lam_vecs], proj, proj, proj, bias_tiles,
      g.reshape(1, dv))


def _outproj_kernel(sb_ref, df_ref, w_ref, h_ref, g_ref, b_ref, o32_ref, o16_ref, *, alpha):
    tm, half = sb_ref.shape
    sub = tm // 4 if tm % 32 == 0 else tm
    for r in range(tm // sub):
        rows = slice(r * sub, (r + 1) * sub)
        mix = _dot(sb_ref[rows, :], w_ref[:half, :]) + _dot(df_ref[rows, :], w_ref[half:, :])
        y = _layernorm(alpha * h_ref[rows, :] + mix, g_ref[...], b_ref[...])
        o32_ref[rows, :] = y
        o16_ref[rows, :] = y.astype(BF16)


def _outproj_call(sb_o, df_o, w, layer, h, g, b, tm, alpha):
    m, d = h.shape
    half = sb_o.shape[1]
    blk = tm * half * 2 * 2 + tm * d * (4 + 4 + 2) + w.shape[1] * w.shape[2] * 2
    row = lambda i: (i, 0)
    fixed = lambda i: (0, 0)
    return pl.pallas_call(
        functools.partial(_outproj_kernel, alpha=alpha),
        grid=(m // tm,),
        in_specs=[pl.BlockSpec((tm, half), row), pl.BlockSpec((tm, half), row),
                  pl.BlockSpec((None,) + w.shape[1:], lambda i: (layer, 0, 0)),
                  pl.BlockSpec((tm, d), row),
                  pl.BlockSpec((1, d), fixed), pl.BlockSpec((1, d), fixed)],
        out_specs=[pl.BlockSpec((tm, d), row), pl.BlockSpec((tm, d), row)],
        out_shape=[jax.ShapeDtypeStruct((m, d), F32), jax.ShapeDtypeStruct((m, d), BF16)],
        compiler_params=_params(("parallel",), _vmem_limit(blk, 0, 4 * tm * d * 4)),
        name="out_proj_ln",
    )(sb_o, df_o, w, h, g.reshape(1, d), b.reshape(1, d))


def _mlp_kernel(x16_ref, x32_ref, wu_ref, wd_ref, g_ref, b_ref, o32_ref, o16_ref, acc_ref, *, alpha):
    f = pl.program_id(1)
    @pl.when(f == 0)
    def _():
        acc_ref[...] = jnp.zeros_like(acc_ref)

    u = _dot(x16_ref[...], wu_ref[...])
    a = jnp.square(jnp.maximum(u, 0.0)).astype(BF16)
    acc_ref[...] += _dot(a, wd_ref[...])

    @pl.when(f == pl.num_programs(1) - 1)
    def _():
        y = _layernorm(alpha * x32_ref[...] + acc_ref[...], g_ref[...], b_ref[...])
        o32_ref[...] = y
        o16_ref[...] = y.astype(BF16)


def _mlp_call(x16, x32, wu, wd, layer, g, b, tm, tf, alpha):
    m, d = x32.shape
    ff = wu.shape[2]
    blk = tm * d * (2 + 4 + 4 + 2) + 2 * d * tf * 2
    row = lambda i, f: (i, 0)
    fixed = lambda i, f: (0, 0)
    return pl.pallas_call(
        functools.partial(_mlp_kernel, alpha=alpha),
        grid=(m // tm, ff // tf),
        in_specs=[pl.BlockSpec((tm, d), row), pl.BlockSpec((tm, d), row),
                  pl.BlockSpec((None, d, tf), lambda i, f: (layer, 0, f)),
                  pl.BlockSpec((None, tf, d), lambda i, f: (layer, f, 0)),
                  pl.BlockSpec((1, d), fixed), pl.BlockSpec((1, d), fixed)],
        out_specs=[pl.BlockSpec((tm, d), row), pl.BlockSpec((tm, d), row)],
        out_shape=[jax.ShapeDtypeStruct((m, d), F32), jax.ShapeDtypeStruct((m, d), BF16)],
        scratch_shapes=[pltpu.VMEM((tm, d), F32)],
        compiler_params=_params(("parallel", "arbitrary"),
                                _vmem_limit(blk, tm * d * 4, 2 * tm * tf * 4 + 3 * tm * d * 4)),
        name="mlp_ln",
    )(x16, x32, wu, wd, g.reshape(1, d), b.reshape(1, d))


def _tile(n, want):
    t = min(n, want)
    while n % t or t % LANES:
        t -= LANES
        assert t > 0, (n, want)
    return t


def kernel(x, ln0_g, ln0_b, w_in, w_out, sb_norm_g, lam_q1, lam_k1, lam_q2, lam_k2, diff_norm_g,
           rel_bias, ln1_g, ln1_b, w_up, w_down, ln2_g, ln2_b):
    b, s, d = x.shape
    depth = w_in.shape[0]
    n_sb = d // (2 * HEAD_DIM)
    n_df = d // (4 * HEAD_DIM)
    assert w_in.shape[2] == 3 * n_sb * HEAD_DIM + 3 * n_df * 2 * HEAD_DIM
    assert rel_bias.shape == (N_BUCKETS, n_df)
    alpha = (2 * depth) ** 0.25
    m = b * s

    tm_ln = _tile(m, 512)
    tm_proj, tn_proj = _tile(m, 1024), _tile(w_in.shape[2], 1024)
    tm_out = _tile(m, 512)
    tm_mlp, tf_mlp = _tile(m, 512), _tile(w_up.shape[2], 1024)
    bq_sb = _tile(s, 256)
    sb_heads = math.gcd(n_sb, 8)
    bq_df = _tile(s, 256)
    df_heads = math.gcd(n_df, 2)
    assert bq_df >= MAX_DISTANCE

    w_out16 = w_out.astype(BF16)
    w_up16, w_down16 = w_up.astype(BF16), w_down.astype(BF16)
    uu = _cumsum_matrix()
    bias_tiles = _bias_call(rel_bias, bq_df)

    h32, h16 = _ln_call(x.reshape(m, d), ln0_g, ln0_b, tm_ln)
    for l in range(depth):
        lam_init = 0.8 - 0.6 * math.exp(-0.3 * l)
        proj = _matmul_call(h16, w_in, l, tm_proj, tn_proj).reshape(b, s, -1)
        sb_o = _sb_call(proj, uu, sb_norm_g[l], n_sb, bq_sb, sb_heads)
        df_o = _df_call(proj, (lam_q1[l], lam_k1[l], lam_q2[l], lam_k2[l]), bias_tiles,
                        diff_norm_g[l], n_sb, n_df, bq_df, df_heads, lam_init)
        h32, h16 = _outproj_call(sb_o.reshape(m, -1), df_o.reshape(m, -1), w_out16, l, h32,
                                 ln1_g[l], ln1_b[l], tm_out, alpha)
        h32, h16 = _mlp_call(h16, h32, w_up16, w_down16, l, ln2_g[l], ln2_b[l],
                             tm_mlp, tf_mlp, alpha)
    return h32.reshape(b, s, d)
```

```python
import functools
import math

import jax
import jax.numpy as jnp
from jax import lax
from jax.experimental import pallas as pl
from jax.experimental.pallas import tpu as pltpu

HEAD_DIM = 128
N_BUCKETS = 32
MAX_DISTANCE = 128
LN_EPS = 1e-5
RMS_EPS = 1e-5
NEG_BIG = -1e30
F32_EXP2_ZERO_BELOW = -150.5
LOG2_E = 1.4426950408889634

LANES = 128
V7X_VMEM_BYTES = 64 * 1024 * 1024

F32 = jnp.float32
BF16 = jnp.bfloat16


def _vmem_limit(block_bytes, scratch_bytes, temp_bytes):
    need = 2 * block_bytes + scratch_bytes + temp_bytes
    return int(min(max(need, 16 * 1024 * 1024), V7X_VMEM_BYTES - 6 * 1024 * 1024))


def _params(semantics, vmem_bytes):
    return pltpu.CompilerParams(dimension_semantics=semantics, vmem_limit_bytes=vmem_bytes)


def _layernorm(v, g, b):
    mu = jnp.mean(v, axis=-1, keepdims=True)
    c = v - mu
    var = jnp.mean(c * c, axis=-1, keepdims=True)
    return c * lax.rsqrt(var + LN_EPS) * g + b


def _dot(a, b):
    return jnp.dot(a, b, preferred_element_type=F32)


def _dot_nt(a, b):
    return lax.dot_general(a, b, (((1,), (1,)), ((), ())), preferred_element_type=F32)


def _ln_kernel(x_ref, g_ref, b_ref, o32_ref, o16_ref):
    y = _layernorm(x_ref[...], g_ref[...], b_ref[...])
    o32_ref[...] = y
    o16_ref[...] = y.astype(BF16)


def _ln_call(x, g, b, tm):
    m, d = x.shape
    blk = tm * d * (4 + 4 + 2)
    return pl.pallas_call(
        _ln_kernel,
        grid=(m // tm,),
        in_specs=[pl.BlockSpec((tm, d), lambda i: (i, 0)),
                  pl.BlockSpec((1, d), lambda i: (0, 0)),
                  pl.BlockSpec((1, d), lambda i: (0, 0))],
        out_specs=[pl.BlockSpec((tm, d), lambda i: (i, 0)),
                   pl.BlockSpec((tm, d), lambda i: (i, 0))],
        out_shape=[jax.ShapeDtypeStruct((m, d), F32), jax.ShapeDtypeStruct((m, d), BF16)],
        compiler_params=_params(("parallel",), _vmem_limit(blk, 0, 4 * tm * d * 4)),
        name="ln_in",
    )(x, g.reshape(1, d), b.reshape(1, d))


def _matmul_kernel(x_ref, w_ref, o_ref, w16_ref):
    @pl.when(pl.program_id(1) == 0)
    def _():
        w16_ref[...] = w_ref[...].astype(BF16)

    o_ref[...] = _dot(x_ref[...], w16_ref[...]).astype(o_ref.dtype)


def _matmul_call(x, w, layer, tm, tn):
    m, k = x.shape
    n = w.shape[2]
    blk = (tm * k + tm * tn) * 2 + k * tn * 4
    return pl.pallas_call(
        _matmul_kernel,
        grid=(n // tn, m // tm),
        in_specs=[pl.BlockSpec((tm, k), lambda j, i: (i, 0)),
                  pl.BlockSpec((None, k, tn), lambda j, i: (layer, 0, j))],
        out_specs=pl.BlockSpec((tm, tn), lambda j, i: (i, j)),
        out_shape=jax.ShapeDtypeStruct((m, n), BF16),
        scratch_shapes=[pltpu.VMEM((k, tn), BF16)],
        compiler_params=_params(("arbitrary", "arbitrary"),
                                _vmem_limit(blk, k * tn * 2, 2 * tm * tn * 4)),
        name="in_proj",
    )(x, w)


def _sb_kernel(q_ref, k_ref, v_ref, uu_ref, g_ref, o_ref, acc_ref, carry_ref, *, bq, n_heads, scale):
    i = pl.program_id(2)
    nsub = bq // LANES
    uu = uu_ref[...]
    acc_ref[...] = jnp.zeros_like(acc_ref)
    carry_ref[...] = jnp.zeros_like(carry_ref)
    row = lax.broadcasted_iota(jnp.int32, (bq, bq), 0)
    col = lax.broadcasted_iota(jnp.int32, (bq, bq), 1)
    causal = col < row

    def block(j, masked):
        start = pl.multiple_of(j * bq, bq)
        for hd in range(n_heads):
            lanes = slice(hd * HEAD_DIM, (hd + 1) * HEAD_DIM)
            q = q_ref[0, :, lanes]
            ks = k_ref[0, pl.ds(start, bq), lanes]
            vs = v_ref[0, pl.ds(start, bq), lanes]
            z = _dot_nt(q, ks) * (scale * LOG2_E)
            sp = jnp.maximum(z, 0.0) + jnp.log2(1.0 + jnp.exp2(-jnp.abs(z)))
            log2_beta = z - sp
            if masked:
                sp = jnp.where(causal, sp, 0.0)
            carry = carry_ref[hd]
            between = [None] * nsub
            for c in reversed(range(nsub)):
                sp_c = sp[:, c * LANES:(c + 1) * LANES]
                hi = sp_c.astype(BF16)
                lo = (sp_c - hi.astype(F32)).astype(BF16)
                cs = _dot(jnp.concatenate([hi, lo], axis=1), uu)
                between[c] = cs[:, :LANES] + carry
                carry = carry + cs[:, LANES:]
            w = jnp.exp2(log2_beta - jnp.concatenate(between, axis=1))
            if masked:
                w = jnp.where(causal, w, 0.0)
            acc_ref[hd] += _dot(w.astype(BF16), vs)
            carry_ref[hd] = carry

    block(i, True)

    def live(state):
        j, carry_min = state
        return jnp.logical_and(j >= 0, carry_min < -F32_EXP2_ZERO_BELOW)

    def step(state):
        j, _ = state
        block(j, False)
        m = carry_ref[0]
        for hd in range(1, n_heads):
            m = jnp.minimum(m, carry_ref[hd])
        return j - 1, jnp.min(m)

    lax.while_loop(live, step, (i - 1, jnp.float32(0.0)))

    for hd in range(n_heads):
        out = acc_ref[hd]
        y = out * lax.rsqrt(jnp.mean(out * out, axis=-1, keepdims=True) + RMS_EPS) * g_ref[...]
        o_ref[0, :, hd * HEAD_DIM:(hd + 1) * HEAD_DIM] = y.astype(o_ref.dtype)


def _sb_call(proj, uu, g, n_sb, bq, n_heads):
    b, s, _ = proj.shape
    width = n_heads * HEAD_DIM
    groups = n_sb // n_heads
    blk = (2 * bq * width + 2 * s * width) * 2 + uu.size * 2
    kern = functools.partial(_sb_kernel, bq=bq, n_heads=n_heads, scale=1.0 / math.sqrt(HEAD_DIM))
    return pl.pallas_call(
        kern,
        grid=(b, groups, s // bq),
        in_specs=[pl.BlockSpec((1, bq, width), lambda bi, h, i: (bi, i, h)),
                  pl.BlockSpec((1, s, width), lambda bi, h, i: (bi, 0, groups + h)),
                  pl.BlockSpec((1, s, width), lambda bi, h, i: (bi, 0, 2 * groups + h)),
                  pl.BlockSpec(uu.shape, lambda bi, h, i: (0, 0)),
                  pl.BlockSpec((1, HEAD_DIM), lambda bi, h, i: (0, 0))],
        out_specs=pl.BlockSpec((1, bq, width), lambda bi, h, i: (bi, i, h)),
        out_shape=jax.ShapeDtypeStruct((b, s, n_sb * HEAD_DIM), BF16),
        scratch_shapes=[pltpu.VMEM((n_heads, bq, HEAD_DIM), F32), pltpu.VMEM((n_heads, bq, LANES), F32)],
        compiler_params=_params(("parallel", "parallel", "arbitrary"),
                                _vmem_limit(blk, 2 * n_heads * bq * LANES * 4, 16 * n_heads * bq * bq * 4)),
        name="sb_attn",
    )(proj, proj, proj, uu, g.reshape(1, HEAD_DIM))


def _cumsum_matrix():
    j = jnp.arange(LANES)[:, None]
    s = jnp.arange(LANES)[None, :]
    u = (j > s).astype(BF16)
    blk = jnp.concatenate([u, jnp.ones((LANES, LANES), BF16)], axis=1)
    return jnp.concatenate([blk, blk], axis=0)


def _bias_kernel(rb_ref, o_ref, *, bq):
    h = pl.program_id(0)
    t = lax.broadcasted_iota(jnp.int32, (bq, bq), 0)
    s = lax.broadcasted_iota(jnp.int32, (bq, bq), 1)
    max_exact = N_BUCKETS // 2
    far = rb_ref[N_BUCKETS - 1, h]
    for which in range(2):
        dist = t - s + which * bq
        n = jnp.maximum(dist, 0)
        nf = jnp.maximum(n, 1).astype(F32)
        large = max_exact + (jnp.log(nf / max_exact) / math.log(MAX_DISTANCE / max_exact)
                             * (N_BUCKETS - max_exact)).astype(jnp.int32)
        large = jnp.minimum(large, N_BUCKETS - 1)
        bucket = jnp.where(n < max_exact, n, large)
        bias = jnp.zeros((bq, bq), F32)
        for bkt in range(N_BUCKETS):
            bias = jnp.where(bucket == bkt, (rb_ref[bkt, h] - far) * LOG2_E, bias)
        o_ref[0, which] = bias


def _bias_call(rel_bias, bq):
    n_df = rel_bias.shape[1]
    return pl.pallas_call(
        functools.partial(_bias_kernel, bq=bq),
        grid=(n_df,),
        in_specs=[pl.BlockSpec(memory_space=pltpu.SMEM)],
        out_specs=pl.BlockSpec((1, 2, bq, bq), lambda h: (h, 0, 0, 0)),
        out_shape=jax.ShapeDtypeStruct((n_df, 2, bq, bq), F32),
        compiler_params=_params(("arbitrary",), _vmem_limit(2 * bq * bq * 4, 0, 8 * bq * bq * 4)),
        name="t5_bias",
    )(rel_bias)


def _df_kernel(lq1_ref, lk1_ref, lq2_ref, lk2_ref, q_ref, k_ref, v_ref, bias_ref, g_ref,
               o_ref, s_ref, mx_ref, ls_ref, acc_ref, *, bq, n_heads, scale, lam_init):
    i = pl.program_id(2)
    dv = 2 * HEAD_DIM
    mx_ref[...] = jnp.full_like(mx_ref, -jnp.inf)
    ls_ref[...] = jnp.zeros_like(ls_ref)
    acc_ref[...] = jnp.zeros_like(acc_ref)
    row = lax.broadcasted_iota(jnp.int32, (bq, bq), 0)
    col = lax.broadcasted_iota(jnp.int32, (bq, bq), 1)

    def scores(start, width, mode):
        for hd in range(n_heads):
            for mp in range(2):
                u = 2 * hd + mp
                lanes = slice(u * HEAD_DIM, (u + 1) * HEAD_DIM)
                s = _dot_nt(q_ref[0, :, lanes], k_ref[0, pl.ds(start, width), lanes]) * (scale * LOG2_E)
                if mode == "diag":
                    s = jnp.where(row >= col, s + bias_ref[hd, 0], NEG_BIG)
                elif mode == "near+diag":
                    s = jnp.concatenate(
                        [s[:, :bq] + bias_ref[hd, 1],
                         jnp.where(row >= col, s[:, bq:] + bias_ref[hd, 0], NEG_BIG)], axis=1)
                s_ref[u, :, pl.ds(start, width)] = s
                mx = mx_ref[u]
                for c in range(width // LANES):
                    mx = jnp.maximum(mx, s[:, c * LANES:(c + 1) * LANES])
                mx_ref[u] = mx

    def values(start, width):
        for hd in range(n_heads):
            vv = v_ref[0, pl.ds(start, width), hd * dv:(hd + 1) * dv]
            for mp in range(2):
                u = 2 * hd + mp
                s = s_ref[u, :, pl.ds(start, width)]
                row_max = mx_ref[u]
                ls = ls_ref[u]
                ps = []
                for c in range(width // LANES):
                    p = jnp.exp2(s[:, c * LANES:(c + 1) * LANES] - row_max)
                    ls = ls + p
                    ps.append(p.astype(BF16))
                ls_ref[u] = ls
                acc_ref[u] += _dot(jnp.concatenate(ps, axis=1), vv)

    def looped(n_blocks, fn):
        def body(t, _):
            fn(pl.multiple_of(t * 4 * bq, 4 * bq), 4 * bq)
            return 0

        quads = lax.shift_right_logical(n_blocks, 2)
        lax.fori_loop(0, quads, body, 0)

        @pl.when((n_blocks & 2) == 2)
        def _():
            fn(pl.multiple_of(quads * 4 * bq, 2 * bq), 2 * bq)

        @pl.when((n_blocks & 1) == 1)
        def _():
            fn(pl.multiple_of((n_blocks - 1) * bq, bq), bq)

    looped(jnp.maximum(i - 1, 0), functools.partial(scores, mode="far"))

    @pl.when(i >= 1)
    def _():
        scores(pl.multiple_of((i - 1) * bq, bq), 2 * bq, "near+diag")

    @pl.when(i == 0)
    def _():
        scores(0, bq, "diag")

    for u in range(2 * n_heads):
        m = jnp.max(mx_ref[u], axis=-1, keepdims=True)
        mx_ref[u] = jnp.broadcast_to(m, (bq, LANES))

    looped(i + 1, values)

    lam = (jnp.exp(jnp.sum(lq1_ref[...] * lk1_ref[...], axis=-1, keepdims=True))
           - jnp.exp(jnp.sum(lq2_ref[...] * lk2_ref[...], axis=-1, keepdims=True)) + lam_init)
    for hd in range(n_heads):
        l0 = jnp.sum(ls_ref[2 * hd], axis=-1, keepdims=True)
        l1 = jnp.sum(ls_ref[2 * hd + 1], axis=-1, keepdims=True)
        out = acc_ref[2 * hd] / l0 - lam * (acc_ref[2 * hd + 1] / l1)
        y = out * lax.rsqrt(jnp.mean(out * out, axis=-1, keepdims=True) + RMS_EPS) * g_ref[...]
        o_ref[0, :, hd * dv:(hd + 1) * dv] = (y * (1.0 - lam_init)).astype(o_ref.dtype)


def _df_call(proj, lam_vecs, bias_tiles, g, n_sb, n_df, bq, n_heads, lam_init):
    b, s, _ = proj.shape
    dv = 2 * HEAD_DIM
    width = n_heads * dv
    groups = n_df // n_heads
    q_start = 3 * n_sb * HEAD_DIM
    assert q_start % width == 0 and n_df % n_heads == 0
    q_off = q_start // width
    k_off = q_off + groups
    v_off = k_off + groups
    blk = (2 * bq * width + 2 * s * width) * 2 + n_heads * 2 * bq * bq * 4
    kern = functools.partial(_df_kernel, bq=bq, n_heads=n_heads, scale=1.0 / math.sqrt(HEAD_DIM),
                             lam_init=lam_init)
    vec_spec = pl.BlockSpec((1, HEAD_DIM), lambda bi, h, i: (0, 0))
    n_soft = 2 * n_heads
    return pl.pallas_call(
        kern,
        grid=(b, groups, s // bq),
        in_specs=[vec_spec, vec_spec, vec_spec, vec_spec,
                  pl.BlockSpec((1, bq, width), lambda bi, h, i: (bi, i, q_off + h)),
                  pl.BlockSpec((1, s, width), lambda bi, h, i: (bi, 0, k_off + h)),
                  pl.BlockSpec((1, s, width), lambda bi, h, i: (bi, 0, v_off + h)),
                  pl.BlockSpec((n_heads, 2, bq, bq), lambda bi, h, i: (h, 0, 0, 0)),
                  pl.BlockSpec((1, dv), lambda bi, h, i: (0, 0))],
        out_specs=pl.BlockSpec((1, bq, width), lambda bi, h, i: (bi, i, h)),
        out_shape=jax.ShapeDtypeStruct((b, s, n_df * dv), BF16),
        scratch_shapes=[pltpu.VMEM((n_soft, bq, s), F32), pltpu.VMEM((n_soft, bq, LANES), F32),
                        pltpu.VMEM((n_soft, bq, LANES), F32), pltpu.VMEM((n_soft, bq, dv), F32)],
        compiler_params=_params(("parallel", "parallel", "arbitrary"),
                                _vmem_limit(blk, n_soft * bq * (s + dv + 2 * LANES) * 4,
                                            16 * n_heads * bq * bq * 4)),
        name="df_attn",
    )(*[v.reshape(1, HEAD_DIM) for v in lam_vecs], proj, proj, proj, bias_tiles,
      g.reshape(1, dv))


def _outproj_kernel(sb_ref, df_ref, w_ref, h_ref, g_ref, b_ref, o32_ref, o16_ref, *, alpha):
    tm, half = sb_ref.shape
    sub = tm // 4 if tm % 32 == 0 else tm
    for r in range(tm // sub):
        rows = slice(r * sub, (r + 1) * sub)
        mix = _dot(sb_ref[rows, :], w_ref[:half, :]) + _dot(df_ref[rows, :], w_ref[half:, :])
        y = _layernorm(alpha * h_ref[rows, :] + mix, g_ref[...], b_ref[...])
        o32_ref[rows, :] = y
        o16_ref[rows, :] = y.astype(BF16)


def _outproj_call(sb_o, df_o, w, layer, h, g, b, tm, alpha):
    m, d = h.shape
    half = sb_o.shape[1]
    blk = tm * half * 2 * 2 + tm * d * (4 + 4 + 2) + w.shape[1] * w.shape[2] * 2
    row = lambda i: (i, 0)
    fixed = lambda i: (0, 0)
    return pl.pallas_call(
        functools.partial(_outproj_kernel, alpha=alpha),
        grid=(m // tm,),
        in_specs=[pl.BlockSpec((tm, half), row), pl.BlockSpec((tm, half), row),
                  pl.BlockSpec((None,) + w.shape[1:], lambda i: (layer, 0, 0)),
                  pl.BlockSpec((tm, d), row),
                  pl.BlockSpec((1, d), fixed), pl.BlockSpec((1, d), fixed)],
        out_specs=[pl.BlockSpec((tm, d), row), pl.BlockSpec((tm, d), row)],
        out_shape=[jax.ShapeDtypeStruct((m, d), F32), jax.ShapeDtypeStruct((m, d), BF16)],
        compiler_params=_params(("parallel",), _vmem_limit(blk, 0, 4 * tm * d * 4)),
        name="out_proj_ln",
    )(sb_o, df_o, w, h, g.reshape(1, d), b.reshape(1, d))


def _mlp_kernel(x16_ref, x32_ref, wu_ref, wd_ref, g_ref, b_ref, o32_ref, o16_ref, acc_ref, *, alpha):
    f = pl.program_id(1)
    @pl.when(f == 0)
    def _():
        acc_ref[...] = jnp.zeros_like(acc_ref)

    u = _dot(x16_ref[...], wu_ref[...])
    a = jnp.square(jnp.maximum(u, 0.0)).astype(BF16)
    acc_ref[...] += _dot(a, wd_ref[...])

    @pl.when(f == pl.num_programs(1) - 1)
    def _():
        y = _layernorm(alpha * x32_ref[...] + acc_ref[...], g_ref[...], b_ref[...])
        o32_ref[...] = y
        o16_ref[...] = y.astype(BF16)


def _mlp_call(x16, x32, wu, wd, layer, g, b, tm, tf, alpha):
    m, d = x32.shape
    ff = wu.shape[2]
    blk = tm * d * (2 + 4 + 4 + 2) + 2 * d * tf * 2
    row = lambda i, f: (i, 0)
    fixed = lambda i, f: (0, 0)
    return pl.pallas_call(
        functools.partial(_mlp_kernel, alpha=alpha),
        grid=(m // tm, ff // tf),
        in_specs=[pl.BlockSpec((tm, d), row), pl.BlockSpec((tm, d), row),
                  pl.BlockSpec((None, d, tf), lambda i, f: (layer, 0, f)),
                  pl.BlockSpec((None, tf, d), lambda i, f: (layer, f, 0)),
                  pl.BlockSpec((1, d), fixed), pl.BlockSpec((1, d), fixed)],
        out_specs=[pl.BlockSpec((tm, d), row), pl.BlockSpec((tm, d), row)],
        out_shape=[jax.ShapeDtypeStruct((m, d), F32), jax.ShapeDtypeStruct((m, d), BF16)],
        scratch_shapes=[pltpu.VMEM((tm, d), F32)],
        compiler_params=_params(("parallel", "arbitrary"),
                                _vmem_limit(blk, tm * d * 4, 2 * tm * tf * 4 + 3 * tm * d * 4)),
        name="mlp_ln",
    )(x16, x32, wu, wd, g.reshape(1, d), b.reshape(1, d))


def _tile(n, want):
    t = min(n, want)
    while n % t or t % LANES:
        t -= LANES
        assert t > 0, (n, want)
    return t


def kernel(x, ln0_g, ln0_b, w_in, w_out, sb_norm_g, lam_q1, lam_k1, lam_q2, lam_k2, diff_norm_g,
           rel_bias, ln1_g, ln1_b, w_up, w_down, ln2_g, ln2_b):
    b, s, d = x.shape
    depth = w_in.shape[0]
    n_sb = d // (2 * HEAD_DIM)
    n_df = d // (4 * HEAD_DIM)
    assert w_in.shape[2] == 3 * n_sb * HEAD_DIM + 3 * n_df * 2 * HEAD_DIM
    assert rel_bias.shape == (N_BUCKETS, n_df)
    alpha = (2 * depth) ** 0.25
    m = b * s

    tm_ln = _tile(m, 512)
    tm_proj, tn_proj = _tile(m, 1024), _tile(w_in.shape[2], 1024)
    tm_out = _tile(m, 512)
    tm_mlp, tf_mlp = _tile(m, 512), _tile(w_up.shape[2], 1024)
    bq_sb = _tile(s, 256)
    sb_heads = math.gcd(n_sb, 8)
    bq_df = _tile(s, 256)
    df_heads = math.gcd(n_df, 2)
    assert bq_df >= MAX_DISTANCE

    w_out16 = w_out.astype(BF16)
    w_up16, w_down16 = w_up.astype(BF16), w_down.astype(BF16)
    uu = _cumsum_matrix()
    bias_tiles = _bias_call(rel_bias, bq_df)

    h32, h16 = _ln_call(x.reshape(m, d), ln0_g, ln0_b, tm_ln)
    for l in range(depth):
        lam_init = 0.8 - 0.6 * math.exp(-0.3 * l)
        proj = _matmul_call(h16, w_in, l, tm_proj, tn_proj).reshape(b, s, -1)
        sb_o = _sb_call(proj, uu, sb_norm_g[l], n_sb, bq_sb, sb_heads)
        df_o = _df_call(proj, (lam_q1[l], lam_k1[l], lam_q2[l], lam_k2[l]), bias_tiles,
                        diff_norm_g[l], n_sb, n_df, bq_df, df_heads, lam_init)
        h32, h16 = _outproj_call(sb_o.reshape(m, -1), df_o.reshape(m, -1), w_out16, l, h32,
                                 ln1_g[l], ln1_b[l], tm_out, alpha)
        h32, h16 = _mlp_call(h16, h32, w_up16, w_down16, l, ln2_g[l], ln2_b[l],
                             tm_mlp, tf_mlp, alpha)
    return h32.reshape(b, s, d)
```

```python
import functools
import math

import jax
import jax.numpy as jnp
from jax import lax
from jax.experimental import pallas as pl
from jax.experimental.pallas import tpu as pltpu

HEAD_DIM = 128
N_BUCKETS = 32
MAX_DISTANCE = 128
LN_EPS = 1e-5
RMS_EPS = 1e-5
NEG_BIG = -1e30
F32_EXP2_ZERO_BELOW = -150.5
LOG2_E = 1.4426950408889634

LANES = 128
V7X_VMEM_BYTES = 64 * 1024 * 1024

F32 = jnp.float32
BF16 = jnp.bfloat16


def _vmem_limit(block_bytes, scratch_bytes, temp_bytes):
    need = 2 * block_bytes + scratch_bytes + temp_bytes
    return int(min(max(need, 16 * 1024 * 1024), V7X_VMEM_BYTES - 6 * 1024 * 1024))


def _params(semantics, vmem_bytes):
    return pltpu.CompilerParams(dimension_semantics=semantics, vmem_limit_bytes=vmem_bytes)


def _layernorm(v, g, b):
    mu = jnp.mean(v, axis=-1, keepdims=True)
    c = v - mu
    var = jnp.mean(c * c, axis=-1, keepdims=True)
    return c * lax.rsqrt(var + LN_EPS) * g + b


def _dot(a, b):
    return jnp.dot(a, b, preferred_element_type=F32)


def _dot_nt(a, b):
    return lax.dot_general(a, b, (((1,), (1,)), ((), ())), preferred_element_type=F32)


def _ln_kernel(x_ref, g_ref, b_ref, o32_ref, o16_ref):
    y = _layernorm(x_ref[...], g_ref[...], b_ref[...])
    o32_ref[...] = y
    o16_ref[...] = y.astype(BF16)


def _ln_call(x, g, b, tm):
    m, d = x.shape
    blk = tm * d * (4 + 4 + 2)
    return pl.pallas_call(
        _ln_kernel,
        grid=(m // tm,),
        in_specs=[pl.BlockSpec((tm, d), lambda i: (i, 0)),
                  pl.BlockSpec((1, d), lambda i: (0, 0)),
                  pl.BlockSpec((1, d), lambda i: (0, 0))],
        out_specs=[pl.BlockSpec((tm, d), lambda i: (i, 0)),
                   pl.BlockSpec((tm, d), lambda i: (i, 0))],
        out_shape=[jax.ShapeDtypeStruct((m, d), F32), jax.ShapeDtypeStruct((m, d), BF16)],
        compiler_params=_params(("parallel",), _vmem_limit(blk, 0, 4 * tm * d * 4)),
        name="ln_in",
    )(x, g.reshape(1, d), b.reshape(1, d))


def _matmul_kernel(x_ref, w_ref, o_ref, w16_ref):
    @pl.when(pl.program_id(1) == 0)
    def _():
        w16_ref[...] = w_ref[...].astype(BF16)

    o_ref[...] = _dot(x_ref[...], w16_ref[...]).astype(o_ref.dtype)


def _matmul_call(x, w, layer, tm, tn):
    m, k = x.shape
    n = w.shape[2]
    blk = (tm * k + tm * tn) * 2 + k * tn * 4
    return pl.pallas_call(
        _matmul_kernel,
        grid=(n // tn, m // tm),
        in_specs=[pl.BlockSpec((tm, k), lambda j, i: (i, 0)),
                  pl.BlockSpec((None, k, tn), lambda j, i: (layer, 0, j))],
        out_specs=pl.BlockSpec((tm, tn), lambda j, i: (i, j)),
        out_shape=jax.ShapeDtypeStruct((m, n), BF16),
        scratch_shapes=[pltpu.VMEM((k, tn), BF16)],
        compiler_params=_params(("arbitrary", "arbitrary"),
                                _vmem_limit(blk, k * tn * 2, 2 * tm * tn * 4)),
        name="in_proj",
    )(x, w)


def _sb_kernel(q_ref, k_ref, v_ref, uu_ref, g_ref, o_ref, acc_ref, carry_ref, *, bq, n_heads, scale):
    i = pl.program_id(2)
    uu = uu_ref[...]
    acc_ref[...] = jnp.zeros_like(acc_ref)
    carry_ref[...] = jnp.zeros_like(carry_ref)
    row = lax.broadcasted_iota(jnp.int32, (bq, bq), 0)
    col = lax.broadcasted_iota(jnp.int32, (bq, bq), 1)
    causal = col < row

    def mask_diag(t):
        lead = t.shape[1] - bq
        tail = jnp.where(causal, t[:, lead:], 0.0)
        return tail if lead == 0 else jnp.concatenate([t[:, :lead], tail], axis=1)

    def block(start, width, diag):
        for hd in range(n_heads):
            lanes = slice(hd * HEAD_DIM, (hd + 1) * HEAD_DIM)
            q = q_ref[0, :, lanes]
            ks = k_ref[0, pl.ds(start, width), lanes]
            vs = v_ref[0, pl.ds(start, width), lanes]
            z = _dot_nt(q, ks) * (scale * LOG2_E)
            sp = jnp.maximum(z, 0.0) + jnp.log2(1.0 + jnp.exp2(-jnp.abs(z)))
            log2_beta = z - sp
            if diag:
                sp = mask_diag(sp)
            carry = carry_ref[hd]
            n_chunks = width // LANES
            between = [None] * n_chunks
            for c in reversed(range(n_chunks)):
                sp_c = sp[:, c * LANES:(c + 1) * LANES]
                hi = sp_c.astype(BF16)
                lo = (sp_c - hi.astype(F32)).astype(BF16)
                cs = _dot(jnp.concatenate([hi, lo], axis=1), uu)
                between[c] = cs[:, :LANES] + carry
                carry = carry + cs[:, LANES:]
            w = jnp.exp2(log2_beta - jnp.concatenate(between, axis=1))
            if diag:
                w = mask_diag(w)
            acc_ref[hd] += _dot(w.astype(BF16), vs)
            carry_ref[hd] = carry

    def carry_min():
        m = carry_ref[0]
        for hd in range(1, n_heads):
            m = jnp.minimum(m, carry_ref[hd])
        return jnp.min(m)

    @pl.when(i == 0)
    def _():
        block(0, bq, True)

    @pl.when(i >= 1)
    def _():
        block(pl.multiple_of((i - 1) * bq, bq), 2 * bq, True)

    def live(state):
        j, lowest = state
        return jnp.logical_and(j >= 0, lowest < -F32_EXP2_ZERO_BELOW)

    def step(state):
        j, _ = state
        block(pl.multiple_of(j * bq, bq), bq, False)
        return j - 1, carry_min()

    lax.while_loop(live, step, (i - 2, carry_min()))

    for hd in range(n_heads):
        out = acc_ref[hd]
        y = out * lax.rsqrt(jnp.mean(out * out, axis=-1, keepdims=True) + RMS_EPS) * g_ref[...]
        o_ref[0, :, hd * HEAD_DIM:(hd + 1) * HEAD_DIM] = y.astype(o_ref.dtype)


def _sb_call(proj, uu, g, n_sb, bq, n_heads):
    b, s, _ = proj.shape
    width = n_heads * HEAD_DIM
    groups = n_sb // n_heads
    blk = (2 * bq * width + 2 * s * width) * 2 + uu.size * 2
    kern = functools.partial(_sb_kernel, bq=bq, n_heads=n_heads, scale=1.0 / math.sqrt(HEAD_DIM))
    return pl.pallas_call(
        kern,
        grid=(b, groups, s // bq),
        in_specs=[pl.BlockSpec((1, bq, width), lambda bi, h, i: (bi, i, h)),
                  pl.BlockSpec((1, s, width), lambda bi, h, i: (bi, 0, groups + h)),
                  pl.BlockSpec((1, s, width), lambda bi, h, i: (bi, 0, 2 * groups + h)),
                  pl.BlockSpec(uu.shape, lambda bi, h, i: (0, 0)),
                  pl.BlockSpec((1, HEAD_DIM), lambda bi, h, i: (0, 0))],
        out_specs=pl.BlockSpec((1, bq, width), lambda bi, h, i: (bi, i, h)),
        out_shape=jax.ShapeDtypeStruct((b, s, n_sb * HEAD_DIM), BF16),
        scratch_shapes=[pltpu.VMEM((n_heads, bq, HEAD_DIM), F32), pltpu.VMEM((n_heads, bq, LANES), F32)],
        compiler_params=_params(("parallel", "parallel", "arbitrary"),
                                _vmem_limit(blk, 2 * n_heads * bq * LANES * 4, 16 * n_heads * bq * bq * 4)),
        name="sb_attn",
    )(proj, proj, proj, uu, g.reshape(1, HEAD_DIM))


def _cumsum_matrix():
    j = jnp.arange(LANES)[:, None]
    s = jnp.arange(LANES)[None, :]
    u = (j > s).astype(BF16)
    blk = jnp.concatenate([u, jnp.ones((LANES, LANES), BF16)], axis=1)
    return jnp.concatenate([blk, blk], axis=0)


def _bias_kernel(rb_ref, o_ref, *, bq):
    h = pl.program_id(0)
    t = lax.broadcasted_iota(jnp.int32, (bq, bq), 0)
    s = lax.broadcasted_iota(jnp.int32, (bq, bq), 1)
    max_exact = N_BUCKETS // 2
    far = rb_ref[N_BUCKETS - 1, h]
    for which in range(2):
        dist = t - s + which * bq
        n = jnp.maximum(dist, 0)
        nf = jnp.maximum(n, 1).astype(F32)
        large = max_exact + (jnp.log(nf / max_exact) / math.log(MAX_DISTANCE / max_exact)
                             * (N_BUCKETS - max_exact)).astype(jnp.int32)
        large = jnp.minimum(large, N_BUCKETS - 1)
        bucket = jnp.where(n < max_exact, n, large)
        bias = jnp.zeros((bq, bq), F32)
        for bkt in range(N_BUCKETS):
            bias = jnp.where(bucket == bkt, (rb_ref[bkt, h] - far) * LOG2_E, bias)
        o_ref[0, which] = bias


def _bias_call(rel_bias, bq):
    n_df = rel_bias.shape[1]
    return pl.pallas_call(
        functools.partial(_bias_kernel, bq=bq),
        grid=(n_df,),
        in_specs=[pl.BlockSpec(memory_space=pltpu.SMEM)],
        out_specs=pl.BlockSpec((1, 2, bq, bq), lambda h: (h, 0, 0, 0)),
        out_shape=jax.ShapeDtypeStruct((n_df, 2, bq, bq), F32),
        compiler_params=_params(("arbitrary",), _vmem_limit(2 * bq * bq * 4, 0, 8 * bq * bq * 4)),
        name="t5_bias",
    )(rel_bias)


def _df_kernel(lq1_ref, lk1_ref, lq2_ref, lk2_ref, q_ref, k_ref, v_ref, bias_ref, g_ref,
               o_ref, s_ref, mx_ref, ls_ref, acc_ref, *, bq, n_heads, scale, lam_init):
    i = pl.program_id(2)
    dv = 2 * HEAD_DIM
    mx_ref[...] = jnp.full_like(mx_ref, -jnp.inf)
    ls_ref[...] = jnp.zeros_like(ls_ref)
    acc_ref[...] = jnp.zeros_like(acc_ref)
    row = lax.broadcasted_iota(jnp.int32, (bq, bq), 0)
    col = lax.broadcasted_iota(jnp.int32, (bq, bq), 1)

    def scores(start, width, mode):
        for hd in range(n_heads):
            for mp in range(2):
                u = 2 * hd + mp
                lanes = slice(u * HEAD_DIM, (u + 1) * HEAD_DIM)
                s = _dot_nt(q_ref[0, :, lanes], k_ref[0, pl.ds(start, width), lanes]) * (scale * LOG2_E)
                if mode == "diag":
                    s = jnp.where(row >= col, s + bias_ref[hd, 0], NEG_BIG)
                elif mode == "near+diag":
                    s = jnp.concatenate(
                        [s[:, :bq] + bias_ref[hd, 1],
                         jnp.where(row >= col, s[:, bq:] + bias_ref[hd, 0], NEG_BIG)], axis=1)
                s_ref[u, :, pl.ds(start, width)] = s
                mx = mx_ref[u]
                for c in range(width // LANES):
                    mx = jnp.maximum(mx, s[:, c * LANES:(c + 1) * LANES])
                mx_ref[u] = mx

    def values(start, width):
        for hd in range(n_heads):
            vv = v_ref[0, pl.ds(start, width), hd * dv:(hd + 1) * dv]
            for mp in range(2):
                u = 2 * hd + mp
                s = s_ref[u, :, pl.ds(start, width)]
                row_max = mx_ref[u]
                ls = ls_ref[u]
                ps = []
                for c in range(width // LANES):
                    p = jnp.exp2(s[:, c * LANES:(c + 1) * LANES] - row_max)
                    ls = ls + p
                    ps.append(p.astype(BF16))
                ls_ref[u] = ls
                acc_ref[u] += _dot(jnp.concatenate(ps, axis=1), vv)

    def looped(n_blocks, fn):
        def body(t, _):
            fn(pl.multiple_of(t * 8 * bq, 8 * bq), 8 * bq)
            return 0

        lax.fori_loop(0, lax.shift_right_logical(n_blocks, 3), body, 0)
        for span in (4, 2, 1):
            @pl.when((n_blocks & span) == span)
            def _(span=span):
                done = n_blocks & ~(2 * span - 1)
                fn(pl.multiple_of(done * bq, span * bq), span * bq)

    looped(jnp.maximum(i - 1, 0), functools.partial(scores, mode="far"))

    @pl.when(i >= 1)
    def _():
        scores(pl.multiple_of((i - 1) * bq, bq), 2 * bq, "near+diag")

    @pl.when(i == 0)
    def _():
        scores(0, bq, "diag")

    for u in range(2 * n_heads):
        m = jnp.max(mx_ref[u], axis=-1, keepdims=True)
        mx_ref[u] = jnp.broadcast_to(m, (bq, LANES))

    looped(i + 1, values)

    lam = (jnp.exp(jnp.sum(lq1_ref[...] * lk1_ref[...], axis=-1, keepdims=True))
           - jnp.exp(jnp.sum(lq2_ref[...] * lk2_ref[...], axis=-1, keepdims=True)) + lam_init)
    for hd in range(n_heads):
        l0 = jnp.sum(ls_ref[2 * hd], axis=-1, keepdims=True)
        l1 = jnp.sum(ls_ref[2 * hd + 1], axis=-1, keepdims=True)
        out = acc_ref[2 * hd] / l0 - lam * (acc_ref[2 * hd + 1] / l1)
        y = out * lax.rsqrt(jnp.mean(out * out, axis=-1, keepdims=True) + RMS_EPS) * g_ref[...]
        o_ref[0, :, hd * dv:(hd + 1) * dv] = (y * (1.0 - lam_init)).astype(o_ref.dtype)


def _df_call(proj, lam_vecs, bias_tiles, g, n_sb, n_df, bq, n_heads, lam_init):
    b, s, _ = proj.shape
    dv = 2 * HEAD_DIM
    width = n_heads * dv
    groups = n_df // n_heads
    q_start = 3 * n_sb * HEAD_DIM
    assert q_start % width == 0 and n_df % n_heads == 0
    q_off = q_start // width
    k_off = q_off + groups
    v_off = k_off + groups
    blk = (2 * bq * width + 2 * s * width) * 2 + n_heads * 2 * bq * bq * 4
    kern = functools.partial(_df_kernel, bq=bq, n_heads=n_heads, scale=1.0 / math.sqrt(HEAD_DIM),
                             lam_init=lam_init)
    vec_spec = pl.BlockSpec((1, HEAD_DIM), lambda bi, h, i: (0, 0))
    n_soft = 2 * n_heads
    return pl.pallas_call(
        kern,
        grid=(b, groups, s // bq),
        in_specs=[vec_spec, vec_spec, vec_spec, vec_spec,
                  pl.BlockSpec((1, bq, width), lambda bi, h, i: (bi, i, q_off + h)),
                  pl.BlockSpec((1, s, width), lambda bi, h, i: (bi, 0, k_off + h)),
                  pl.BlockSpec((1, s, width), lambda bi, h, i: (bi, 0, v_off + h)),
                  pl.BlockSpec((n_heads, 2, bq, bq), lambda bi, h, i: (h, 0, 0, 0)),
                  pl.BlockSpec((1, dv), lambda bi, h, i: (0, 0))],
        out_specs=pl.BlockSpec((1, bq, width), lambda bi, h, i: (bi, i, h)),
        out_shape=jax.ShapeDtypeStruct((b, s, n_df * dv), BF16),
        scratch_shapes=[pltpu.VMEM((n_soft, bq, s), F32), pltpu.VMEM((n_soft, bq, LANES), F32),
                        pltpu.VMEM((n_soft, bq, LANES), F32), pltpu.VMEM((n_soft, bq, dv), F32)],
        compiler_params=_params(("parallel", "parallel", "arbitrary"),
                                _vmem_limit(blk, n_soft * bq * (s + dv + 2 * LANES) * 4,
                                            16 * n_heads * bq * bq * 4)),
        name="df_attn",
    )(*[v.reshape(1, HEAD_DIM) for v in lam_vecs], proj, proj, proj, bias_tiles,
      g.reshape(1, dv))


def _outproj_kernel(sb_ref, df_ref, w_ref, h_ref, g_ref, b_ref, o32_ref, o16_ref, *, alpha):
    tm, half = sb_ref.shape
    sub = tm // 4 if tm % 32 == 0 else tm
    for r in range(tm // sub):
        rows = slice(r * sub, (r + 1) * sub)
        mix = _dot(sb_ref[rows, :], w_ref[:half, :]) + _dot(df_ref[rows, :], w_ref[half:, :])
        y = _layernorm(alpha * h_ref[rows, :] + mix, g_ref[...], b_ref[...])
        o32_ref[rows, :] = y
        o16_ref[rows, :] = y.astype(BF16)


def _outproj_call(sb_o, df_o, w, layer, h, g, b, tm, alpha):
    m, d = h.shape
    half = sb_o.shape[1]
    blk = tm * half * 2 * 2 + tm * d * (4 + 4 + 2) + w.shape[1] * w.shape[2] * 2
    row = lambda i: (i, 0)
    fixed = lambda i: (0, 0)
    return pl.pallas_call(
        functools.partial(_outproj_kernel, alpha=alpha),
        grid=(m // tm,),
        in_specs=[pl.BlockSpec((tm, half), row), pl.BlockSpec((tm, half), row),
                  pl.BlockSpec((None,) + w.shape[1:], lambda i: (layer, 0, 0)),
                  pl.BlockSpec((tm, d), row),
                  pl.BlockSpec((1, d), fixed), pl.BlockSpec((1, d), fixed)],
        out_specs=[pl.BlockSpec((tm, d), row), pl.BlockSpec((tm, d), row)],
        out_shape=[jax.ShapeDtypeStruct((m, d), F32), jax.ShapeDtypeStruct((m, d), BF16)],
        compiler_params=_params(("parallel",), _vmem_limit(blk, 0, 4 * tm * d * 4)),
        name="out_proj_ln",
    )(sb_o, df_o, w, h, g.reshape(1, d), b.reshape(1, d))


def _mlp_kernel(x16_ref, x32_ref, wu_ref, wd_ref, g_ref, b_ref, o32_ref, o16_ref, acc_ref, *, alpha):
    f = pl.program_id(1)
    @pl.when(f == 0)
    def _():
        acc_ref[...] = jnp.zeros_like(acc_ref)

    u = _dot(x16_ref[...], wu_ref[...])
    a = jnp.square(jnp.maximum(u, 0.0)).astype(BF16)
    acc_ref[...] += _dot(a, wd_ref[...])

    @pl.when(f == pl.num_programs(1) - 1)
    def _():
        y = _layernorm(alpha * x32_ref[...] + acc_ref[...], g_ref[...], b_ref[...])
        o32_ref[...] = y
        o16_ref[...] = y.astype(BF16)


def _mlp_call(x16, x32, wu, wd, layer, g, b, tm, tf, alpha):
    m, d = x32.shape
    ff = wu.shape[2]
    blk = tm * d * (2 + 4 + 4 + 2) + 2 * d * tf * 2
    row = lambda i, f: (i, 0)
    fixed = lambda i, f: (0, 0)
    return pl.pallas_call(
        functools.partial(_mlp_kernel, alpha=alpha),
        grid=(m // tm, ff // tf),
        in_specs=[pl.BlockSpec((tm, d), row), pl.BlockSpec((tm, d), row),
                  pl.BlockSpec((None, d, tf), lambda i, f: (layer, 0, f)),
                  pl.BlockSpec((None, tf, d), lambda i, f: (layer, f, 0)),
                  pl.BlockSpec((1, d), fixed), pl.BlockSpec((1, d), fixed)],
        out_specs=[pl.BlockSpec((tm, d), row), pl.BlockSpec((tm, d), row)],
        out_shape=[jax.ShapeDtypeStruct((m, d), F32), jax.ShapeDtypeStruct((m, d), BF16)],
        scratch_shapes=[pltpu.VMEM((tm, d), F32)],
        compiler_params=_params(("parallel", "arbitrary"),
                                _vmem_limit(blk, tm * d * 4, 2 * tm * tf * 4 + 3 * tm * d * 4)),
        name="mlp_ln",
    )(x16, x32, wu, wd, g.reshape(1, d), b.reshape(1, d))


def _tile(n, want):
    t = min(n, want)
    while n % t or t % LANES:
        t -= LANES
        assert t > 0, (n, want)
    return t


def kernel(x, ln0_g, ln0_b, w_in, w_out, sb_norm_g, lam_q1, lam_k1, lam_q2, lam_k2, diff_norm_g,
           rel_bias, ln1_g, ln1_b, w_up, w_down, ln2_g, ln2_b):
    b, s, d = x.shape
    depth = w_in.shape[0]
    n_sb = d // (2 * HEAD_DIM)
    n_df = d // (4 * HEAD_DIM)
    assert w_in.shape[2] == 3 * n_sb * HEAD_DIM + 3 * n_df * 2 * HEAD_DIM
    assert rel_bias.shape == (N_BUCKETS, n_df)
    alpha = (2 * depth) ** 0.25
    m = b * s

    tm_ln = _tile(m, 512)
    tm_proj, tn_proj = _tile(m, 1024), _tile(w_in.shape[2], 1024)
    tm_out = _tile(m, 512)
    tm_mlp, tf_mlp = _tile(m, 512), _tile(w_up.shape[2], 1024)
    bq_sb = _tile(s, 256)
    sb_heads = math.gcd(n_sb, 8)
    bq_df = _tile(s, 256)
    df_heads = math.gcd(n_df, 2)
    assert bq_df >= MAX_DISTANCE

    w_out16 = w_out.astype(BF16)
    w_up16, w_down16 = w_up.astype(BF16), w_down.astype(BF16)
    uu = _cumsum_matrix()
    bias_tiles = _bias_call(rel_bias, bq_df)

    h32, h16 = _ln_call(x.reshape(m, d), ln0_g, ln0_b, tm_ln)
    for l in range(depth):
        lam_init = 0.8 - 0.6 * math.exp(-0.3 * l)
        proj = _matmul_call(h16, w_in, l, tm_proj, tn_proj).reshape(b, s, -1)
        sb_o = _sb_call(proj, uu, sb_norm_g[l], n_sb, bq_sb, sb_heads)
        df_o = _df_call(proj, (lam_q1[l], lam_k1[l], lam_q2[l], lam_k2[l]), bias_tiles,
                        diff_norm_g[l], n_sb, n_df, bq_df, df_heads, lam_init)
        h32, h16 = _outproj_call(sb_o.reshape(m, -1), df_o.reshape(m, -1), w_out16, l, h32,
                                 ln1_g[l], ln1_b[l], tm_out, alpha)
        h32, h16 = _mlp_call(h16, h32, w_up16, w_down16, l, ln2_g[l], ln2_b[l],
                             tm_mlp, tf_mlp, alpha)
    return h32.reshape(b, s, d)
```

```python
import functools
import math

import jax
import jax.numpy as jnp
from jax import lax
from jax.experimental import pallas as pl
from jax.experimental.pallas import tpu as pltpu

HEAD_DIM = 128
N_BUCKETS = 32
MAX_DISTANCE = 128
LN_EPS = 1e-5
RMS_EPS = 1e-5
NEG_BIG = -1e30
F32_EXP2_ZERO_BELOW = -150.5
LOG2_E = 1.4426950408889634

LANES = 128
V7X_VMEM_BYTES = 64 * 1024 * 1024

F32 = jnp.float32
BF16 = jnp.bfloat16


def _vmem_limit(block_bytes, scratch_bytes, temp_bytes):
    need = 2 * block_bytes + scratch_bytes + temp_bytes
    return int(min(max(need, 16 * 1024 * 1024), V7X_VMEM_BYTES - 6 * 1024 * 1024))


def _params(semantics, vmem_bytes):
    return pltpu.CompilerParams(dimension_semantics=semantics, vmem_limit_bytes=vmem_bytes)


def _layernorm(v, g, b):
    mu = jnp.mean(v, axis=-1, keepdims=True)
    c = v - mu
    var = jnp.mean(c * c, axis=-1, keepdims=True)
    return c * lax.rsqrt(var + LN_EPS) * g + b


def _dot(a, b):
    return jnp.dot(a, b, preferred_element_type=F32)


def _dot_nt(a, b):
    return lax.dot_general(a, b, (((1,), (1,)), ((), ())), preferred_element_type=F32)


def _ln_kernel(x_ref, g_ref, b_ref, o32_ref, o16_ref):
    y = _layernorm(x_ref[...], g_ref[...], b_ref[...])
    o32_ref[...] = y
    o16_ref[...] = y.astype(BF16)


def _ln_call(x, g, b, tm):
    m, d = x.shape
    blk = tm * d * (4 + 4 + 2)
    return pl.pallas_call(
        _ln_kernel,
        grid=(m // tm,),
        in_specs=[pl.BlockSpec((tm, d), lambda i: (i, 0)),
                  pl.BlockSpec((1, d), lambda i: (0, 0)),
                  pl.BlockSpec((1, d), lambda i: (0, 0))],
        out_specs=[pl.BlockSpec((tm, d), lambda i: (i, 0)),
                   pl.BlockSpec((tm, d), lambda i: (i, 0))],
        out_shape=[jax.ShapeDtypeStruct((m, d), F32), jax.ShapeDtypeStruct((m, d), BF16)],
        compiler_params=_params(("parallel",), _vmem_limit(blk, 0, 4 * tm * d * 4)),
        name="ln_in",
    )(x, g.reshape(1, d), b.reshape(1, d))


def _matmul_kernel(x_ref, w_ref, o_ref, w16_ref):
    @pl.when(pl.program_id(1) == 0)
    def _():
        w16_ref[...] = w_ref[...].astype(BF16)

    o_ref[...] = _dot(x_ref[...], w16_ref[...]).astype(o_ref.dtype)


def _matmul_call(x, w, layer, tm, tn):
    m, k = x.shape
    n = w.shape[2]
    blk = (tm * k + tm * tn) * 2 + k * tn * 4
    return pl.pallas_call(
        _matmul_kernel,
        grid=(n // tn, m // tm),
        in_specs=[pl.BlockSpec((tm, k), lambda j, i: (i, 0)),
                  pl.BlockSpec((None, k, tn), lambda j, i: (layer, 0, j))],
        out_specs=pl.BlockSpec((tm, tn), lambda j, i: (i, j)),
        out_shape=jax.ShapeDtypeStruct((m, n), BF16),
        scratch_shapes=[pltpu.VMEM((k, tn), BF16)],
        compiler_params=_params(("arbitrary", "arbitrary"),
                                _vmem_limit(blk, k * tn * 2, 2 * tm * tn * 4)),
        name="in_proj",
    )(x, w)


def _sb_kernel(q_ref, k_ref, v_ref, uu_ref, g_ref, o_ref, acc_ref, carry_ref, *, bq, n_heads, scale):
    i = pl.program_id(2)
    uu = uu_ref[...]
    acc_ref[...] = jnp.zeros_like(acc_ref)
    carry_ref[...] = jnp.zeros_like(carry_ref)
    row = lax.broadcasted_iota(jnp.int32, (bq, bq), 0)
    col = lax.broadcasted_iota(jnp.int32, (bq, bq), 1)
    causal = col < row

    def mask_diag(t):
        lead = t.shape[1] - bq
        tail = jnp.where(causal, t[:, lead:], 0.0)
        return tail if lead == 0 else jnp.concatenate([t[:, :lead], tail], axis=1)

    def block(start, width, diag):
        for hd in range(n_heads):
            lanes = slice(hd * HEAD_DIM, (hd + 1) * HEAD_DIM)
            q = q_ref[0, :, lanes]
            ks = k_ref[0, pl.ds(start, width), lanes]
            vs = v_ref[0, pl.ds(start, width), lanes]
            z = _dot_nt(q, ks) * (scale * LOG2_E)
            sp = jnp.maximum(z, 0.0) + jnp.log2(1.0 + jnp.exp2(-jnp.abs(z)))
            log2_beta = z - sp
            if diag:
                sp = mask_diag(sp)
            carry = carry_ref[hd]
            n_chunks = width // LANES
            between = [None] * n_chunks
            for c in reversed(range(n_chunks)):
                sp_c = sp[:, c * LANES:(c + 1) * LANES]
                hi = sp_c.astype(BF16)
                lo = (sp_c - hi.astype(F32)).astype(BF16)
                cs = _dot(jnp.concatenate([hi, lo], axis=1), uu)
                between[c] = cs[:, :LANES] + carry
                carry = carry + cs[:, LANES:]
            w = jnp.exp2(log2_beta - jnp.concatenate(between, axis=1))
            if diag:
                w = mask_diag(w)
            acc_ref[hd] += _dot(w.astype(BF16), vs)
            carry_ref[hd] = carry

    def carry_min():
        m = carry_ref[0]
        for hd in range(1, n_heads):
            m = jnp.minimum(m, carry_ref[hd])
        return jnp.min(m)

    @pl.when(i == 0)
    def _():
        block(0, bq, True)

    @pl.when(i >= 1)
    def _():
        block(pl.multiple_of((i - 1) * bq, bq), 2 * bq, True)

    def live(state):
        j, lowest = state
        return jnp.logical_and(j >= 0, lowest < -F32_EXP2_ZERO_BELOW)

    def step(state):
        j, _ = state
        block(pl.multiple_of(j * bq, bq), bq, False)
        return j - 1, carry_min()

    lax.while_loop(live, step, (i - 2, carry_min()))

    for hd in range(n_heads):
        out = acc_ref[hd]
        y = out * lax.rsqrt(jnp.mean(out * out, axis=-1, keepdims=True) + RMS_EPS) * g_ref[...]
        o_ref[0, :, hd * HEAD_DIM:(hd + 1) * HEAD_DIM] = y.astype(o_ref.dtype)


def _sb_call(proj, uu, g, n_sb, bq, n_heads):
    b, s, _ = proj.shape
    width = n_heads * HEAD_DIM
    groups = n_sb // n_heads
    blk = (2 * bq * width + 2 * s * width) * 2 + uu.size * 2
    kern = functools.partial(_sb_kernel, bq=bq, n_heads=n_heads, scale=1.0 / math.sqrt(HEAD_DIM))
    return pl.pallas_call(
        kern,
        grid=(b, groups, s // bq),
        in_specs=[pl.BlockSpec((1, bq, width), lambda bi, h, i: (bi, i, h)),
                  pl.BlockSpec((1, s, width), lambda bi, h, i: (bi, 0, groups + h)),
                  pl.BlockSpec((1, s, width), lambda bi, h, i: (bi, 0, 2 * groups + h)),
                  pl.BlockSpec(uu.shape, lambda bi, h, i: (0, 0)),
                  pl.BlockSpec((1, HEAD_DIM), lambda bi, h, i: (0, 0))],
        out_specs=pl.BlockSpec((1, bq, width), lambda bi, h, i: (bi, i, h)),
        out_shape=jax.ShapeDtypeStruct((b, s, n_sb * HEAD_DIM), BF16),
        scratch_shapes=[pltpu.VMEM((n_heads, bq, HEAD_DIM), F32), pltpu.VMEM((n_heads, bq, LANES), F32)],
        compiler_params=_params(("parallel", "parallel", "arbitrary"),
                                _vmem_limit(blk, 2 * n_heads * bq * LANES * 4, 16 * n_heads * bq * bq * 4)),
        name="sb_attn",
    )(proj, proj, proj, uu, g.reshape(1, HEAD_DIM))


def _cumsum_matrix():
    j = jnp.arange(LANES)[:, None]
    s = jnp.arange(LANES)[None, :]
    u = (j > s).astype(BF16)
    blk = jnp.concatenate([u, jnp.ones((LANES, LANES), BF16)], axis=1)
    return jnp.concatenate([blk, blk], axis=0)


def _bias_kernel(rb_ref, o_ref, *, bq):
    h = pl.program_id(0)
    t = lax.broadcasted_iota(jnp.int32, (bq, bq), 0)
    s = lax.broadcasted_iota(jnp.int32, (bq, bq), 1)
    max_exact = N_BUCKETS // 2
    far = rb_ref[N_BUCKETS - 1, h]
    for which in range(2):
        dist = t - s + which * bq
        n = jnp.maximum(dist, 0)
        nf = jnp.maximum(n, 1).astype(F32)
        large = max_exact + (jnp.log(nf / max_exact) / math.log(MAX_DISTANCE / max_exact)
                             * (N_BUCKETS - max_exact)).astype(jnp.int32)
        large = jnp.minimum(large, N_BUCKETS - 1)
        bucket = jnp.where(n < max_exact, n, large)
        bias = jnp.zeros((bq, bq), F32)
        for bkt in range(N_BUCKETS):
            bias = jnp.where(bucket == bkt, (rb_ref[bkt, h] - far) * LOG2_E, bias)
        o_ref[0, which] = bias


def _bias_call(rel_bias, bq):
    n_df = rel_bias.shape[1]
    return pl.pallas_call(
        functools.partial(_bias_kernel, bq=bq),
        grid=(n_df,),
        in_specs=[pl.BlockSpec(memory_space=pltpu.SMEM)],
        out_specs=pl.BlockSpec((1, 2, bq, bq), lambda h: (h, 0, 0, 0)),
        out_shape=jax.ShapeDtypeStruct((n_df, 2, bq, bq), F32),
        compiler_params=_params(("arbitrary",), _vmem_limit(2 * bq * bq * 4, 0, 8 * bq * bq * 4)),
        name="t5_bias",
    )(rel_bias)


def _df_kernel(lq1_ref, lk1_ref, lq2_ref, lk2_ref, q_ref, k_ref, v_ref, bias_ref, g_ref,
               o_ref, s_ref, mx_ref, ls_ref, acc_ref, *, bq, n_heads, scale, lam_init):
    i = pl.program_id(2)
    dv = 2 * HEAD_DIM
    mx_ref[...] = jnp.full_like(mx_ref, -jnp.inf)
    ls_ref[...] = jnp.zeros_like(ls_ref)
    acc_ref[...] = jnp.zeros_like(acc_ref)
    row = lax.broadcasted_iota(jnp.int32, (bq, bq), 0)
    col = lax.broadcasted_iota(jnp.int32, (bq, bq), 1)

    def scores(start, width, mode):
        for hd in range(n_heads):
            for mp in range(2):
                u = 2 * hd + mp
                lanes = slice(u * HEAD_DIM, (u + 1) * HEAD_DIM)
                s = _dot_nt(q_ref[0, :, lanes], k_ref[0, pl.ds(start, width), lanes]) * (scale * LOG2_E)
                if mode == "diag":
                    s = jnp.where(row >= col, s + bias_ref[hd, 0], NEG_BIG)
                elif mode == "near+diag":
                    s = jnp.concatenate(
                        [s[:, :bq] + bias_ref[hd, 1],
                         jnp.where(row >= col, s[:, bq:] + bias_ref[hd, 0], NEG_BIG)], axis=1)
                s_ref[u, :, pl.ds(start, width)] = s
                mx = mx_ref[u]
                for c in range(width // LANES):
                    mx = jnp.maximum(mx, s[:, c * LANES:(c + 1) * LANES])
                mx_ref[u] = mx

    def values(start, width):
        for hd in range(n_heads):
            vv = v_ref[0, pl.ds(start, width), hd * dv:(hd + 1) * dv]
            for mp in range(2):
                u = 2 * hd + mp
                s = s_ref[u, :, pl.ds(start, width)]
                row_max = mx_ref[u]
                ls = ls_ref[u]
                ps = []
                for c in range(width // LANES):
                    p = jnp.exp2(s[:, c * LANES:(c + 1) * LANES] - row_max)
                    ls = ls + p
                    ps.append(p.astype(BF16))
                ls_ref[u] = ls
                acc_ref[u] += _dot(jnp.concatenate(ps, axis=1), vv)

    def looped(n_blocks, fn):
        def body(t, _):
            fn(pl.multiple_of(t * 8 * bq, 8 * bq), 8 * bq)
            return 0

        lax.fori_loop(0, lax.shift_right_logical(n_blocks, 3), body, 0)
        for span in (4, 2, 1):
            @pl.when((n_blocks & span) == span)
            def _(span=span):
                done = n_blocks & ~(2 * span - 1)
                fn(pl.multiple_of(done * bq, span * bq), span * bq)

    looped(jnp.maximum(i - 1, 0), functools.partial(scores, mode="far"))

    @pl.when(i >= 1)
    def _():
        scores(pl.multiple_of((i - 1) * bq, bq), 2 * bq, "near+diag")

    @pl.when(i == 0)
    def _():
        scores(0, bq, "diag")

    for u in range(2 * n_heads):
        m = jnp.max(mx_ref[u], axis=-1, keepdims=True)
        mx_ref[u] = jnp.broadcast_to(m, (bq, LANES))

    looped(i + 1, values)

    lam = (jnp.exp(jnp.sum(lq1_ref[...] * lk1_ref[...], axis=-1, keepdims=True))
           - jnp.exp(jnp.sum(lq2_ref[...] * lk2_ref[...], axis=-1, keepdims=True)) + lam_init)
    for hd in range(n_heads):
        l0 = jnp.sum(ls_ref[2 * hd], axis=-1, keepdims=True)
        l1 = jnp.sum(ls_ref[2 * hd + 1], axis=-1, keepdims=True)
        out = acc_ref[2 * hd] / l0 - lam * (acc_ref[2 * hd + 1] / l1)
        y = out * lax.rsqrt(jnp.mean(out * out, axis=-1, keepdims=True) + RMS_EPS) * g_ref[...]
        o_ref[0, :, hd * dv:(hd + 1) * dv] = (y * (1.0 - lam_init)).astype(o_ref.dtype)


def _df_call(proj, lam_vecs, bias_tiles, g, n_sb, n_df, bq, n_heads, lam_init):
    b, s, _ = proj.shape
    dv = 2 * HEAD_DIM
    width = n_heads * dv
    groups = n_df // n_heads
    q_start = 3 * n_sb * HEAD_DIM
    assert q_start % width == 0 and n_df % n_heads == 0
    q_off = q_start // width
    k_off = q_off + groups
    v_off = k_off + groups
    blk = (2 * bq * width + 2 * s * width) * 2 + n_heads * 2 * bq * bq * 4
    kern = functools.partial(_df_kernel, bq=bq, n_heads=n_heads, scale=1.0 / math.sqrt(HEAD_DIM),
                             lam_init=lam_init)
    vec_spec = pl.BlockSpec((1, HEAD_DIM), lambda bi, h, i: (0, 0))
    n_soft = 2 * n_heads
    return pl.pallas_call(
        kern,
        grid=(b, groups, s // bq),
        in_specs=[vec_spec, vec_spec, vec_spec, vec_spec,
                  pl.BlockSpec((1, bq, width), lambda bi, h, i: (bi, i, q_off + h)),
                  pl.BlockSpec((1, s, width), lambda bi, h, i: (bi, 0, k_off + h)),
                  pl.BlockSpec((1, s, width), lambda bi, h, i: (bi, 0, v_off + h)),
                  pl.BlockSpec((n_heads, 2, bq, bq), lambda bi, h, i: (h, 0, 0, 0)),
                  pl.BlockSpec((1, dv), lambda bi, h, i: (0, 0))],
        out_specs=pl.BlockSpec((1, bq, width), lambda bi, h, i: (bi, i, h)),
        out_shape=jax.ShapeDtypeStruct((b, s, n_df * dv), BF16),
        scratch_shapes=[pltpu.VMEM((n_soft, bq, s), F32), pltpu.VMEM((n_soft, bq, LANES), F32),
                        pltpu.VMEM((n_soft, bq, LANES), F32), pltpu.VMEM((n_soft, bq, dv), F32)],
        compiler_params=_params(("parallel", "parallel", "arbitrary"),
                                _vmem_limit(blk, n_soft * bq * (s + dv + 2 * LANES) * 4,
                                            16 * n_heads * bq * bq * 4)),
        name="df_attn",
    )(*[v.reshape(1, HEAD_DIM) for v in lam_vecs], proj, proj, proj, bias_tiles,
      g.reshape(1, dv))


def _outproj_kernel(sb_ref, df_ref, w_ref, h_ref, g_ref, b_ref, o32_ref, o16_ref, *, alpha):
    tm, half = sb_ref.shape
    sub = tm // 4 if tm % 32 == 0 else tm
    for r in range(tm // sub):
        rows = slice(r * sub, (r + 1) * sub)
        mix = _dot(sb_ref[rows, :], w_ref[:half, :]) + _dot(df_ref[rows, :], w_ref[half:, :])
        y = _layernorm(alpha * h_ref[rows, :] + mix, g_ref[...], b_ref[...])
        o32_ref[rows, :] = y
        o16_ref[rows, :] = y.astype(BF16)


def _outproj_call(sb_o, df_o, w, layer, h, g, b, tm, alpha):
    m, d = h.shape
    half = sb_o.shape[1]
    blk = tm * half * 2 * 2 + tm * d * (4 + 4 + 2) + w.shape[1] * w.shape[2] * 2
    row = lambda i: (i, 0)
    fixed = lambda i: (0, 0)
    return pl.pallas_call(
        functools.partial(_outproj_kernel, alpha=alpha),
        grid=(m // tm,),
        in_specs=[pl.BlockSpec((tm, half), row), pl.BlockSpec((tm, half), row),
                  pl.BlockSpec((None,) + w.shape[1:], lambda i: (layer, 0, 0)),
                  pl.BlockSpec((tm, d), row),
                  pl.BlockSpec((1, d), fixed), pl.BlockSpec((1, d), fixed)],
        out_specs=[pl.BlockSpec((tm, d), row), pl.BlockSpec((tm, d), row)],
        out_shape=[jax.ShapeDtypeStruct((m, d), F32), jax.ShapeDtypeStruct((m, d), BF16)],
        compiler_params=_params(("parallel",), _vmem_limit(blk, 0, 4 * tm * d * 4)),
        name="out_proj_ln",
    )(sb_o, df_o, w, h, g.reshape(1, d), b.reshape(1, d))


def _mlp_kernel(x16_ref, x32_ref, wu_ref, wd_ref, g_ref, b_ref, o32_ref, o16_ref, acc0_ref, acc1_ref,
                *, alpha, n_tiles, n_f):
    i = pl.program_id(0)
    f = pl.program_id(1)
    accs = (acc0_ref, acc1_ref)
    rows_per_step = o32_ref.shape[0] // n_f

    @pl.when(jnp.logical_and(i == 0, f == 0))
    def _():
        acc1_ref[...] = jnp.zeros_like(acc1_ref)

    def finish_slice(prev_ref):
        rows = pl.ds(pl.multiple_of(f * rows_per_step, rows_per_step), rows_per_step)
        y = _layernorm(alpha * x32_ref[rows, :] + prev_ref[rows, :], g_ref[...], b_ref[...])
        o32_ref[rows, :] = y
        o16_ref[rows, :] = y.astype(BF16)

    for parity in range(2):
        cur_ref, prev_ref = accs[parity], accs[1 - parity]
        mine = jnp.logical_and(i < n_tiles, lax.rem(i, 2) == parity)

        @pl.when(jnp.logical_and(mine, f == 0))
        def _(cur_ref=cur_ref):
            cur_ref[...] = jnp.zeros_like(cur_ref)

        @pl.when(mine)
        def _(cur_ref=cur_ref, prev_ref=prev_ref):
            u = _dot(x16_ref[...], wu_ref[...])
            a = jnp.square(jnp.maximum(u, 0.0)).astype(BF16)
            cur_ref[...] += _dot(a, wd_ref[...])
            finish_slice(prev_ref)

    @pl.when(i == n_tiles)
    def _():
        finish_slice(accs[(n_tiles - 1) % 2])


def _mlp_call(x16, x32, wu, wd, layer, g, b, tm, tf, alpha):
    m, d = x32.shape
    ff = wu.shape[2]
    n_tiles, n_f = m // tm, ff // tf
    assert tm % n_f == 0 and (tm // n_f) % 16 == 0, (tm, n_f)
    blk = tm * d * (2 + 4 + 4 + 2) + 2 * d * tf * 2
    lagged = lambda i, f: (jnp.maximum(i - 1, 0), 0)
    fixed = lambda i, f: (0, 0)
    f_idx = lambda i, f: jnp.where(i < n_tiles, f, n_f - 1)
    return pl.pallas_call(
        functools.partial(_mlp_kernel, alpha=alpha, n_tiles=n_tiles, n_f=n_f),
        grid=(n_tiles + 1, n_f),
        in_specs=[pl.BlockSpec((tm, d), lambda i, f: (jnp.minimum(i, n_tiles - 1), 0)),
                  pl.BlockSpec((tm, d), lagged),
                  pl.BlockSpec((None, d, tf), lambda i, f: (layer, 0, f_idx(i, f))),
                  pl.BlockSpec((None, tf, d), lambda i, f: (layer, f_idx(i, f), 0)),
                  pl.BlockSpec((1, d), fixed), pl.BlockSpec((1, d), fixed)],
        out_specs=[pl.BlockSpec((tm, d), lagged), pl.BlockSpec((tm, d), lagged)],
        out_shape=[jax.ShapeDtypeStruct((m, d), F32), jax.ShapeDtypeStruct((m, d), BF16)],
        scratch_shapes=[pltpu.VMEM((tm, d), F32), pltpu.VMEM((tm, d), F32)],
        compiler_params=_params(("arbitrary", "arbitrary"),
                                _vmem_limit(blk, 2 * tm * d * 4, 2 * tm * tf * 4 + 2 * tm * d * 4)),
        name="mlp_ln",
    )(x16, x32, wu, wd, g.reshape(1, d), b.reshape(1, d))


def _tile(n, want):
    t = min(n, want)
    while n % t or t % LANES:
        t -= LANES
        assert t > 0, (n, want)
    return t


def kernel(x, ln0_g, ln0_b, w_in, w_out, sb_norm_g, lam_q1, lam_k1, lam_q2, lam_k2, diff_norm_g,
           rel_bias, ln1_g, ln1_b, w_up, w_down, ln2_g, ln2_b):
    b, s, d = x.shape
    depth = w_in.shape[0]
    n_sb = d // (2 * HEAD_DIM)
    n_df = d // (4 * HEAD_DIM)
    assert w_in.shape[2] == 3 * n_sb * HEAD_DIM + 3 * n_df * 2 * HEAD_DIM
    assert rel_bias.shape == (N_BUCKETS, n_df)
    alpha = (2 * depth) ** 0.25
    m = b * s

    tm_ln = _tile(m, 512)
    tm_proj, tn_proj = _tile(m, 1024), _tile(w_in.shape[2], 1024)
    tm_out = _tile(m, 512)
    tm_mlp, tf_mlp = _tile(m, 512), _tile(w_up.shape[2], 1024)
    bq_sb = _tile(s, 256)
    sb_heads = math.gcd(n_sb, 8)
    bq_df = _tile(s, 256)
    df_heads = math.gcd(n_df, 2)
    assert bq_df >= MAX_DISTANCE

    w_out16 = w_out.astype(BF16)
    w_up16, w_down16 = w_up.astype(BF16), w_down.astype(BF16)
    uu = _cumsum_matrix()
    bias_tiles = _bias_call(rel_bias, bq_df)

    h32, h16 = _ln_call(x.reshape(m, d), ln0_g, ln0_b, tm_ln)
    for l in range(depth):
        lam_init = 0.8 - 0.6 * math.exp(-0.3 * l)
        proj = _matmul_call(h16, w_in, l, tm_proj, tn_proj).reshape(b, s, -1)
        sb_o = _sb_call(proj, uu, sb_norm_g[l], n_sb, bq_sb, sb_heads)
        df_o = _df_call(proj, (lam_q1[l], lam_k1[l], lam_q2[l], lam_k2[l]), bias_tiles,
                        diff_norm_g[l], n_sb, n_df, bq_df, df_heads, lam_init)
        h32, h16 = _outproj_call(sb_o.reshape(m, -1), df_o.reshape(m, -1), w_out16, l, h32,
                                 ln1_g[l], ln1_b[l], tm_out, alpha)
        h32, h16 = _mlp_call(h16, h32, w_up16, w_down16, l, ln2_g[l], ln2_b[l],
                             tm_mlp, tf_mlp, alpha)
    return h32.reshape(b, s, d)
```

```python
import functools
import math

import jax
import jax.numpy as jnp
from jax import lax
from jax.experimental import pallas as pl
from jax.experimental.pallas import tpu as pltpu

HEAD_DIM = 128
N_BUCKETS = 32
MAX_DISTANCE = 128
LN_EPS = 1e-5
RMS_EPS = 1e-5
NEG_BIG = -1e30
F32_EXP2_ZERO_BELOW = -150.5
LOG2_E = 1.4426950408889634

LANES = 128
V7X_VMEM_BYTES = 64 * 1024 * 1024

F32 = jnp.float32
BF16 = jnp.bfloat16


def _vmem_limit(block_bytes, scratch_bytes, temp_bytes):
    need = 2 * block_bytes + scratch_bytes + temp_bytes
    return int(min(max(need, 16 * 1024 * 1024), V7X_VMEM_BYTES - 6 * 1024 * 1024))


def _params(semantics, vmem_bytes):
    return pltpu.CompilerParams(dimension_semantics=semantics, vmem_limit_bytes=vmem_bytes)


def _layernorm(v, g, b):
    mu = jnp.mean(v, axis=-1, keepdims=True)
    c = v - mu
    var = jnp.mean(c * c, axis=-1, keepdims=True)
    return c * lax.rsqrt(var + LN_EPS) * g + b


def _dot(a, b):
    return jnp.dot(a, b, preferred_element_type=F32)


def _dot_nt(a, b):
    return lax.dot_general(a, b, (((1,), (1,)), ((), ())), preferred_element_type=F32)


def _ln_kernel(x_ref, g_ref, b_ref, o32_ref, o16_ref):
    y = _layernorm(x_ref[...], g_ref[...], b_ref[...])
    o32_ref[...] = y
    o16_ref[...] = y.astype(BF16)


def _ln_call(x, g, b, tm):
    m, d = x.shape
    blk = tm * d * (4 + 4 + 2)
    return pl.pallas_call(
        _ln_kernel,
        grid=(m // tm,),
        in_specs=[pl.BlockSpec((tm, d), lambda i: (i, 0)),
                  pl.BlockSpec((1, d), lambda i: (0, 0)),
                  pl.BlockSpec((1, d), lambda i: (0, 0))],
        out_specs=[pl.BlockSpec((tm, d), lambda i: (i, 0)),
                   pl.BlockSpec((tm, d), lambda i: (i, 0))],
        out_shape=[jax.ShapeDtypeStruct((m, d), F32), jax.ShapeDtypeStruct((m, d), BF16)],
        compiler_params=_params(("parallel",), _vmem_limit(blk, 0, 4 * tm * d * 4)),
        name="ln_in",
    )(x, g.reshape(1, d), b.reshape(1, d))


def _matmul_kernel(*refs, n_side):
    x_ref, w_ref = refs[:2]
    side_in = refs[2:2 + n_side]
    o_ref = refs[2 + n_side]
    side_out = refs[3 + n_side:3 + 2 * n_side]
    w16_ref = refs[3 + 2 * n_side]

    @pl.when(pl.program_id(1) == 0)
    def _():
        w16_ref[...] = w_ref[...].astype(BF16)

    o_ref[...] = _dot(x_ref[...], w16_ref[...]).astype(o_ref.dtype)
    for src_ref, dst_ref in zip(side_in, side_out):
        dst_ref[...] = src_ref[...].astype(BF16)


def _slab_count(rows, steps):
    units = rows // 16
    return max(c for c in range(1, min(units, steps) + 1) if units % c == 0)


def _matmul_call(x, w, layer, tm, tn, side):
    m, k = x.shape
    n = w.shape[2]
    n_j, n_i = n // tn, m // tm
    blk = (tm * k + tm * tn) * 2 + k * tn * 4
    side_in_specs, side_out_specs, side_shapes = [], [], []
    for arr in side:
        rows, cols = arr.shape[1:]
        slabs = _slab_count(rows, n_j * n_i)
        slab = lambda j, i, slabs=slabs: jnp.minimum(j * n_i + i, slabs - 1)
        side_in_specs.append(pl.BlockSpec((None, rows // slabs, cols),
                                          lambda j, i, slab=slab: (layer, slab(j, i), 0)))
        side_out_specs.append(pl.BlockSpec((rows // slabs, cols), lambda j, i, slab=slab: (slab(j, i), 0)))
        side_shapes.append(jax.ShapeDtypeStruct((rows, cols), BF16))
        blk += (rows // slabs) * cols * 6
    outs = pl.pallas_call(
        functools.partial(_matmul_kernel, n_side=len(side)),
        grid=(n_j, n_i),
        in_specs=[pl.BlockSpec((tm, k), lambda j, i: (i, 0)),
                  pl.BlockSpec((None, k, tn), lambda j, i: (layer, 0, j))] + side_in_specs,
        out_specs=[pl.BlockSpec((tm, tn), lambda j, i: (i, j))] + side_out_specs,
        out_shape=[jax.ShapeDtypeStruct((m, n), BF16)] + side_shapes,
        scratch_shapes=[pltpu.VMEM((k, tn), BF16)],
        compiler_params=_params(("arbitrary", "arbitrary"),
                                _vmem_limit(blk, k * tn * 2, 2 * tm * tn * 4)),
        name="in_proj",
    )(x, w, *side)
    return outs[0], outs[1:]


def _sb_kernel(q_ref, k_ref, v_ref, uu_ref, g_ref, o_ref, acc_ref, carry_ref, *, bq, n_heads, scale):
    i = pl.program_id(2)
    uu = uu_ref[...]
    acc_ref[...] = jnp.zeros_like(acc_ref)
    carry_ref[...] = jnp.zeros_like(carry_ref)
    row = lax.broadcasted_iota(jnp.int32, (bq, bq), 0)
    col = lax.broadcasted_iota(jnp.int32, (bq, bq), 1)
    causal = col < row

    def mask_diag(t):
        lead = t.shape[1] - bq
        tail = jnp.where(causal, t[:, lead:], 0.0)
        return tail if lead == 0 else jnp.concatenate([t[:, :lead], tail], axis=1)

    def block(start, width, diag):
        for hd in range(n_heads):
            lanes = slice(hd * HEAD_DIM, (hd + 1) * HEAD_DIM)
            q = q_ref[0, :, lanes]
            ks = k_ref[0, pl.ds(start, width), lanes]
            vs = v_ref[0, pl.ds(start, width), lanes]
            z = _dot_nt(q, ks) * (scale * LOG2_E)
            sp = jnp.maximum(z, 0.0) + jnp.log2(1.0 + jnp.exp2(-jnp.abs(z)))
            log2_beta = z - sp
            if diag:
                sp = mask_diag(sp)
            carry = carry_ref[hd]
            n_chunks = width // LANES
            between = [None] * n_chunks
            for c in reversed(range(n_chunks)):
                sp_c = sp[:, c * LANES:(c + 1) * LANES]
                hi = sp_c.astype(BF16)
                lo = (sp_c - hi.astype(F32)).astype(BF16)
                cs = _dot(jnp.concatenate([hi, lo], axis=1), uu)
                between[c] = cs[:, :LANES] + carry
                carry = carry + cs[:, LANES:]
            w = jnp.exp2(log2_beta - jnp.concatenate(between, axis=1))
            if diag:
                w = mask_diag(w)
            acc_ref[hd] += _dot(w.astype(BF16), vs)
            carry_ref[hd] = carry

    def carry_min():
        m = carry_ref[0]
        for hd in range(1, n_heads):
            m = jnp.minimum(m, carry_ref[hd])
        return jnp.min(m)

    @pl.when(i == 0)
    def _():
        block(0, bq, True)

    @pl.when(i >= 1)
    def _():
        block(pl.multiple_of((i - 1) * bq, bq), 2 * bq, True)

    def live(state):
        j, lowest = state
        return jnp.logical_and(j >= 0, lowest < -F32_EXP2_ZERO_BELOW)

    def step(state):
        j, _ = state
        block(pl.multiple_of(j * bq, bq), bq, False)
        return j - 1, carry_min()

    lax.while_loop(live, step, (i - 2, carry_min()))

    for hd in range(n_heads):
        out = acc_ref[hd]
        y = out * lax.rsqrt(jnp.mean(out * out, axis=-1, keepdims=True) + RMS_EPS) * g_ref[...]
        o_ref[0, :, hd * HEAD_DIM:(hd + 1) * HEAD_DIM] = y.astype(o_ref.dtype)


def _sb_call(proj, uu, g, n_sb, bq, n_heads):
    b, s, _ = proj.shape
    width = n_heads * HEAD_DIM
    groups = n_sb // n_heads
    blk = (2 * bq * width + 2 * s * width) * 2 + uu.size * 2
    kern = functools.partial(_sb_kernel, bq=bq, n_heads=n_heads, scale=1.0 / math.sqrt(HEAD_DIM))
    return pl.pallas_call(
        kern,
        grid=(b, groups, s // bq),
        in_specs=[pl.BlockSpec((1, bq, width), lambda bi, h, i: (bi, i, h)),
                  pl.BlockSpec((1, s, width), lambda bi, h, i: (bi, 0, groups + h)),
                  pl.BlockSpec((1, s, width), lambda bi, h, i: (bi, 0, 2 * groups + h)),
                  pl.BlockSpec(uu.shape, lambda bi, h, i: (0, 0)),
                  pl.BlockSpec((1, HEAD_DIM), lambda bi, h, i: (0, 0))],
        out_specs=pl.BlockSpec((1, bq, width), lambda bi, h, i: (bi, i, h)),
        out_shape=jax.ShapeDtypeStruct((b, s, n_sb * HEAD_DIM), BF16),
        scratch_shapes=[pltpu.VMEM((n_heads, bq, HEAD_DIM), F32), pltpu.VMEM((n_heads, bq, LANES), F32)],
        compiler_params=_params(("parallel", "parallel", "arbitrary"),
                                _vmem_limit(blk, 2 * n_heads * bq * LANES * 4, 16 * n_heads * bq * bq * 4)),
        name="sb_attn",
    )(proj, proj, proj, uu, g.reshape(1, HEAD_DIM))


def _cumsum_matrix():
    j = jnp.arange(LANES)[:, None]
    s = jnp.arange(LANES)[None, :]
    u = (j > s).astype(BF16)
    blk = jnp.concatenate([u, jnp.ones((LANES, LANES), BF16)], axis=1)
    return jnp.concatenate([blk, blk], axis=0)


def _bias_kernel(rb_ref, o_ref, *, bq):
    h = pl.program_id(0)
    t = lax.broadcasted_iota(jnp.int32, (bq, bq), 0)
    s = lax.broadcasted_iota(jnp.int32, (bq, bq), 1)
    max_exact = N_BUCKETS // 2
    far = rb_ref[N_BUCKETS - 1, h]
    for which in range(2):
        dist = t - s + which * bq
        n = jnp.maximum(dist, 0)
        nf = jnp.maximum(n, 1).astype(F32)
        large = max_exact + (jnp.log(nf / max_exact) / math.log(MAX_DISTANCE / max_exact)
                             * (N_BUCKETS - max_exact)).astype(jnp.int32)
        large = jnp.minimum(large, N_BUCKETS - 1)
        bucket = jnp.where(n < max_exact, n, large)
        bias = jnp.zeros((bq, bq), F32)
        for bkt in range(N_BUCKETS):
            bias = jnp.where(bucket == bkt, (rb_ref[bkt, h] - far) * LOG2_E, bias)
        o_ref[0, which] = bias


def _bias_call(rel_bias, bq):
    n_df = rel_bias.shape[1]
    return pl.pallas_call(
        functools.partial(_bias_kernel, bq=bq),
        grid=(n_df,),
        in_specs=[pl.BlockSpec(memory_space=pltpu.SMEM)],
        out_specs=pl.BlockSpec((1, 2, bq, bq), lambda h: (h, 0, 0, 0)),
        out_shape=jax.ShapeDtypeStruct((n_df, 2, bq, bq), F32),
        compiler_params=_params(("arbitrary",), _vmem_limit(2 * bq * bq * 4, 0, 8 * bq * bq * 4)),
        name="t5_bias",
    )(rel_bias)


def _df_kernel(lq1_ref, lk1_ref, lq2_ref, lk2_ref, q_ref, k_ref, v_ref, bias_ref, g_ref,
               o_ref, s_ref, mx_ref, ls_ref, acc_ref, *, bq, n_heads, scale, lam_init):
    i = pl.program_id(2)
    dv = 2 * HEAD_DIM
    mx_ref[...] = jnp.full_like(mx_ref, -jnp.inf)
    ls_ref[...] = jnp.zeros_like(ls_ref)
    acc_ref[...] = jnp.zeros_like(acc_ref)
    row = lax.broadcasted_iota(jnp.int32, (bq, bq), 0)
    col = lax.broadcasted_iota(jnp.int32, (bq, bq), 1)

    def scores(start, width, mode):
        for hd in range(n_heads):
            for mp in range(2):
                u = 2 * hd + mp
                lanes = slice(u * HEAD_DIM, (u + 1) * HEAD_DIM)
                s = _dot_nt(q_ref[0, :, lanes], k_ref[0, pl.ds(start, width), lanes]) * (scale * LOG2_E)
                if mode == "diag":
                    s = jnp.where(row >= col, s + bias_ref[hd, 0], NEG_BIG)
                elif mode == "near+diag":
                    s = jnp.concatenate(
                        [s[:, :bq] + bias_ref[hd, 1],
                         jnp.where(row >= col, s[:, bq:] + bias_ref[hd, 0], NEG_BIG)], axis=1)
                s_ref[u, :, pl.ds(start, width)] = s
                mx = mx_ref[u]
                for c in range(width // LANES):
                    mx = jnp.maximum(mx, s[:, c * LANES:(c + 1) * LANES])
                mx_ref[u] = mx

    def values(start, width):
        for hd in range(n_heads):
            vv = v_ref[0, pl.ds(start, width), hd * dv:(hd + 1) * dv]
            for mp in range(2):
                u = 2 * hd + mp
                s = s_ref[u, :, pl.ds(start, width)]
                row_max = mx_ref[u]
                ls = ls_ref[u]
                ps = []
                for c in range(width // LANES):
                    p = jnp.exp2(s[:, c * LANES:(c + 1) * LANES] - row_max)
                    ls = ls + p
                    ps.append(p.astype(BF16))
                ls_ref[u] = ls
                acc_ref[u] += _dot(jnp.concatenate(ps, axis=1), vv)

    def looped(n_blocks, fn):
        def body(t, _):
            fn(pl.multiple_of(t * 8 * bq, 8 * bq), 8 * bq)
            return 0

        lax.fori_loop(0, lax.shift_right_logical(n_blocks, 3), body, 0)
        for span in (4, 2, 1):
            @pl.when((n_blocks & span) == span)
            def _(span=span):
                done = n_blocks & ~(2 * span - 1)
                fn(pl.multiple_of(done * bq, span * bq), span * bq)

    looped(jnp.maximum(i - 1, 0), functools.partial(scores, mode="far"))

    @pl.when(i >= 1)
    def _():
        scores(pl.multiple_of((i - 1) * bq, bq), 2 * bq, "near+diag")

    @pl.when(i == 0)
    def _():
        scores(0, bq, "diag")

    for u in range(2 * n_heads):
        m = jnp.max(mx_ref[u], axis=-1, keepdims=True)
        mx_ref[u] = jnp.broadcast_to(m, (bq, LANES))

    looped(i + 1, values)

    lam = (jnp.exp(jnp.sum(lq1_ref[...] * lk1_ref[...], axis=-1, keepdims=True))
           - jnp.exp(jnp.sum(lq2_ref[...] * lk2_ref[...], axis=-1, keepdims=True)) + lam_init)
    for hd in range(n_heads):
        l0 = jnp.sum(ls_ref[2 * hd], axis=-1, keepdims=True)
        l1 = jnp.sum(ls_ref[2 * hd + 1], axis=-1, keepdims=True)
        out = acc_ref[2 * hd] / l0 - lam * (acc_ref[2 * hd + 1] / l1)
        y = out * lax.rsqrt(jnp.mean(out * out, axis=-1, keepdims=True) + RMS_EPS) * g_ref[...]
        o_ref[0, :, hd * dv:(hd + 1) * dv] = (y * (1.0 - lam_init)).astype(o_ref.dtype)


def _df_call(proj, lam_vecs, bias_tiles, g, n_sb, n_df, bq, n_heads, lam_init):
    b, s, _ = proj.shape
    dv = 2 * HEAD_DIM
    width = n_heads * dv
    groups = n_df // n_heads
    q_start = 3 * n_sb * HEAD_DIM
    assert q_start % width == 0 and n_df % n_heads == 0
    q_off = q_start // width
    k_off = q_off + groups
    v_off = k_off + groups
    blk = (2 * bq * width + 2 * s * width) * 2 + n_heads * 2 * bq * bq * 4
    kern = functools.partial(_df_kernel, bq=bq, n_heads=n_heads, scale=1.0 / math.sqrt(HEAD_DIM),
                             lam_init=lam_init)
    vec_spec = pl.BlockSpec((1, HEAD_DIM), lambda bi, h, i: (0, 0))
    n_soft = 2 * n_heads
    return pl.pallas_call(
        kern,
        grid=(b, groups, s // bq),
        in_specs=[vec_spec, vec_spec, vec_spec, vec_spec,
                  pl.BlockSpec((1, bq, width), lambda bi, h, i: (bi, i, q_off + h)),
                  pl.BlockSpec((1, s, width), lambda bi, h, i: (bi, 0, k_off + h)),
                  pl.BlockSpec((1, s, width), lambda bi, h, i: (bi, 0, v_off + h)),
                  pl.BlockSpec((n_heads, 2, bq, bq), lambda bi, h, i: (h, 0, 0, 0)),
                  pl.BlockSpec((1, dv), lambda bi, h, i: (0, 0))],
        out_specs=pl.BlockSpec((1, bq, width), lambda bi, h, i: (bi, i, h)),
        out_shape=jax.ShapeDtypeStruct((b, s, n_df * dv), BF16),
        scratch_shapes=[pltpu.VMEM((n_soft, bq, s), F32), pltpu.VMEM((n_soft, bq, LANES), F32),
                        pltpu.VMEM((n_soft, bq, LANES), F32), pltpu.VMEM((n_soft, bq, dv), F32)],
        compiler_params=_params(("parallel", "parallel", "arbitrary"),
                                _vmem_limit(blk, n_soft * bq * (s + dv + 2 * LANES) * 4,
                                            16 * n_heads * bq * bq * 4)),
        name="df_attn",
    )(*[v.reshape(1, HEAD_DIM) for v in lam_vecs], proj, proj, proj, bias_tiles,
      g.reshape(1, dv))


def _outproj_kernel(sb_ref, df_ref, w_ref, h_ref, g_ref, b_ref, o32_ref, o16_ref, *, alpha):
    tm, half = sb_ref.shape
    sub = tm // 4 if tm % 32 == 0 else tm
    for r in range(tm // sub):
        rows = slice(r * sub, (r + 1) * sub)
        mix = _dot(sb_ref[rows, :], w_ref[:half, :]) + _dot(df_ref[rows, :], w_ref[half:, :])
        y = _layernorm(alpha * h_ref[rows, :] + mix, g_ref[...], b_ref[...])
        o32_ref[rows, :] = y
        o16_ref[rows, :] = y.astype(BF16)


def _outproj_call(sb_o, df_o, w, h, g, b, tm, alpha):
    m, d = h.shape
    half = sb_o.shape[1]
    blk = tm * half * 2 * 2 + tm * d * (4 + 4 + 2) + w.size * 2
    row = lambda i: (i, 0)
    fixed = lambda i: (0, 0)
    return pl.pallas_call(
        functools.partial(_outproj_kernel, alpha=alpha),
        grid=(m // tm,),
        in_specs=[pl.BlockSpec((tm, half), row), pl.BlockSpec((tm, half), row),
                  pl.BlockSpec(w.shape, fixed),
                  pl.BlockSpec((tm, d), row),
                  pl.BlockSpec((1, d), fixed), pl.BlockSpec((1, d), fixed)],
        out_specs=[pl.BlockSpec((tm, d), row), pl.BlockSpec((tm, d), row)],
        out_shape=[jax.ShapeDtypeStruct((m, d), F32), jax.ShapeDtypeStruct((m, d), BF16)],
        compiler_params=_params(("parallel",), _vmem_limit(blk, 0, 4 * tm * d * 4)),
        name="out_proj_ln",
    )(sb_o, df_o, w, h, g.reshape(1, d), b.reshape(1, d))


def _mlp_kernel(x16_ref, x32_ref, wu_ref, wd_ref, g_ref, b_ref, o32_ref, o16_ref, acc0_ref, acc1_ref,
                *, alpha, n_tiles, n_f):
    i = pl.program_id(0)
    f = pl.program_id(1)
    accs = (acc0_ref, acc1_ref)
    rows_per_step = o32_ref.shape[0] // n_f

    @pl.when(jnp.logical_and(i == 0, f == 0))
    def _():
        acc1_ref[...] = jnp.zeros_like(acc1_ref)

    def finish_slice(prev_ref):
        rows = pl.ds(pl.multiple_of(f * rows_per_step, rows_per_step), rows_per_step)
        y = _layernorm(alpha * x32_ref[rows, :] + prev_ref[rows, :], g_ref[...], b_ref[...])
        o32_ref[rows, :] = y
        o16_ref[rows, :] = y.astype(BF16)

    for parity in range(2):
        cur_ref, prev_ref = accs[parity], accs[1 - parity]
        mine = jnp.logical_and(i < n_tiles, lax.rem(i, 2) == parity)

        @pl.when(jnp.logical_and(mine, f == 0))
        def _(cur_ref=cur_ref):
            cur_ref[...] = jnp.zeros_like(cur_ref)

        @pl.when(mine)
        def _(cur_ref=cur_ref, prev_ref=prev_ref):
            u = _dot(x16_ref[...], wu_ref[...])
            a = jnp.square(jnp.maximum(u, 0.0)).astype(BF16)
            cur_ref[...] += _dot(a, wd_ref[...])
            finish_slice(prev_ref)

    @pl.when(i == n_tiles)
    def _():
        finish_slice(accs[(n_tiles - 1) % 2])


def _mlp_call(x16, x32, wu, wd, g, b, tm, tf, alpha):
    m, d = x32.shape
    ff = wu.shape[1]
    n_tiles, n_f = m // tm, ff // tf
    assert tm % n_f == 0 and (tm // n_f) % 16 == 0, (tm, n_f)
    blk = tm * d * (2 + 4 + 4 + 2) + 2 * d * tf * 2
    lagged = lambda i, f: (jnp.maximum(i - 1, 0), 0)
    fixed = lambda i, f: (0, 0)
    f_idx = lambda i, f: jnp.where(i < n_tiles, f, n_f - 1)
    return pl.pallas_call(
        functools.partial(_mlp_kernel, alpha=alpha, n_tiles=n_tiles, n_f=n_f),
        grid=(n_tiles + 1, n_f),
        in_specs=[pl.BlockSpec((tm, d), lambda i, f: (jnp.minimum(i, n_tiles - 1), 0)),
                  pl.BlockSpec((tm, d), lagged),
                  pl.BlockSpec((d, tf), lambda i, f: (0, f_idx(i, f))),
                  pl.BlockSpec((tf, d), lambda i, f: (f_idx(i, f), 0)),
                  pl.BlockSpec((1, d), fixed), pl.BlockSpec((1, d), fixed)],
        out_specs=[pl.BlockSpec((tm, d), lagged), pl.BlockSpec((tm, d), lagged)],
        out_shape=[jax.ShapeDtypeStruct((m, d), F32), jax.ShapeDtypeStruct((m, d), BF16)],
        scratch_shapes=[pltpu.VMEM((tm, d), F32), pltpu.VMEM((tm, d), F32)],
        compiler_params=_params(("arbitrary", "arbitrary"),
                                _vmem_limit(blk, 2 * tm * d * 4, 2 * tm * tf * 4 + 2 * tm * d * 4)),
        name="mlp_ln",
    )(x16, x32, wu, wd, g.reshape(1, d), b.reshape(1, d))


def _tile(n, want):
    t = min(n, want)
    while n % t or t % LANES:
        t -= LANES
        assert t > 0, (n, want)
    return t


def kernel(x, ln0_g, ln0_b, w_in, w_out, sb_norm_g, lam_q1, lam_k1, lam_q2, lam_k2, diff_norm_g,
           rel_bias, ln1_g, ln1_b, w_up, w_down, ln2_g, ln2_b):
    b, s, d = x.shape
    depth = w_in.shape[0]
    n_sb = d // (2 * HEAD_DIM)
    n_df = d // (4 * HEAD_DIM)
    assert w_in.shape[2] == 3 * n_sb * HEAD_DIM + 3 * n_df * 2 * HEAD_DIM
    assert rel_bias.shape == (N_BUCKETS, n_df)
    alpha = (2 * depth) ** 0.25
    m = b * s

    tm_ln = _tile(m, 512)
    tm_proj, tn_proj = _tile(m, 1024), _tile(w_in.shape[2], 1024)
    tm_out = _tile(m, 512)
    tm_mlp, tf_mlp = _tile(m, 512), _tile(w_up.shape[2], 1024)
    bq_sb = _tile(s, 256)
    sb_heads = math.gcd(n_sb, 8)
    bq_df = _tile(s, 256)
    df_heads = math.gcd(n_df, 2)
    assert bq_df >= MAX_DISTANCE

    uu = _cumsum_matrix()
    bias_tiles = _bias_call(rel_bias, bq_df)

    h32, h16 = _ln_call(x.reshape(m, d), ln0_g, ln0_b, tm_ln)
    for l in range(depth):
        lam_init = 0.8 - 0.6 * math.exp(-0.3 * l)
        proj, (w_out16, w_up16, w_down16) = _matmul_call(h16, w_in, l, tm_proj, tn_proj,
                                                         (w_out, w_up, w_down))
        proj = proj.reshape(b, s, -1)
        sb_o = _sb_call(proj, uu, sb_norm_g[l], n_sb, bq_sb, sb_heads)
        df_o = _df_call(proj, (lam_q1[l], lam_k1[l], lam_q2[l], lam_k2[l]), bias_tiles,
                        diff_norm_g[l], n_sb, n_df, bq_df, df_heads, lam_init)
        h32, h16 = _outproj_call(sb_o.reshape(m, -1), df_o.reshape(m, -1), w_out16, h32,
                                 ln1_g[l], ln1_b[l], tm_out, alpha)
        h32, h16 = _mlp_call(h16, h32, w_up16, w_down16, ln2_g[l], ln2_b[l],
                             tm_mlp, tf_mlp, alpha)
    return h32.reshape(b, s, d)
```

```python
import functools
import math

import jax
import jax.numpy as jnp
from jax import lax
from jax.experimental import pallas as pl
from jax.experimental.pallas import tpu as pltpu

HEAD_DIM = 128
N_BUCKETS = 32
MAX_DISTANCE = 128
LN_EPS = 1e-5
RMS_EPS = 1e-5
NEG_BIG = -1e30
F32_EXP2_ZERO_BELOW = -150.5
LOG2_E = 1.4426950408889634

LANES = 128
V7X_VMEM_BYTES = 64 * 1024 * 1024

F32 = jnp.float32
BF16 = jnp.bfloat16


def _vmem_limit(block_bytes, scratch_bytes, temp_bytes):
    need = 2 * block_bytes + scratch_bytes + temp_bytes
    return int(min(max(need, 16 * 1024 * 1024), V7X_VMEM_BYTES - 6 * 1024 * 1024))


def _params(semantics, vmem_bytes):
    return pltpu.CompilerParams(dimension_semantics=semantics, vmem_limit_bytes=vmem_bytes)


def _layernorm(v, g, b):
    mu = jnp.mean(v, axis=-1, keepdims=True)
    c = v - mu
    var = jnp.mean(c * c, axis=-1, keepdims=True)
    return c * lax.rsqrt(var + LN_EPS) * g + b


def _dot(a, b):
    return jnp.dot(a, b, preferred_element_type=F32)


def _dot_nt(a, b):
    return lax.dot_general(a, b, (((1,), (1,)), ((), ())), preferred_element_type=F32)


def _ln_kernel(x_ref, g_ref, b_ref, o32_ref, o16_ref):
    y = _layernorm(x_ref[...], g_ref[...], b_ref[...])
    o32_ref[...] = y
    o16_ref[...] = y.astype(BF16)


def _ln_call(x, g, b, tm):
    m, d = x.shape
    blk = tm * d * (4 + 4 + 2)
    return pl.pallas_call(
        _ln_kernel,
        grid=(m // tm,),
        in_specs=[pl.BlockSpec((tm, d), lambda i: (i, 0)),
                  pl.BlockSpec((1, d), lambda i: (0, 0)),
                  pl.BlockSpec((1, d), lambda i: (0, 0))],
        out_specs=[pl.BlockSpec((tm, d), lambda i: (i, 0)),
                   pl.BlockSpec((tm, d), lambda i: (i, 0))],
        out_shape=[jax.ShapeDtypeStruct((m, d), F32), jax.ShapeDtypeStruct((m, d), BF16)],
        compiler_params=_params(("parallel",), _vmem_limit(blk, 0, 4 * tm * d * 4)),
        name="ln_in",
    )(x, g.reshape(1, d), b.reshape(1, d))


def _matmul_kernel(*refs, n_side):
    x_ref, w_ref = refs[:2]
    side_in = refs[2:2 + n_side]
    o_ref = refs[2 + n_side]
    side_out = refs[3 + n_side:3 + 2 * n_side]
    w16_ref = refs[3 + 2 * n_side]

    @pl.when(pl.program_id(1) == 0)
    def _():
        w16_ref[...] = w_ref[...].astype(BF16)

    o_ref[...] = _dot(x_ref[...], w16_ref[...]).astype(o_ref.dtype)
    for src_ref, dst_ref in zip(side_in, side_out):
        dst_ref[...] = src_ref[...].astype(BF16)


def _slab_count(rows, steps):
    units = rows // 16
    return max(c for c in range(1, min(units, steps) + 1) if units % c == 0)


def _matmul_call(x, w, layer, tm, tn, side):
    m, k = x.shape
    n = w.shape[2]
    n_j, n_i = n // tn, m // tm
    blk = (tm * k + tm * tn) * 2 + k * tn * 4
    side_in_specs, side_out_specs, side_shapes = [], [], []
    for arr in side:
        rows, cols = arr.shape[1:]
        slabs = _slab_count(rows, n_j * n_i)
        slab = lambda j, i, slabs=slabs: jnp.minimum(j * n_i + i, slabs - 1)
        side_in_specs.append(pl.BlockSpec((None, rows // slabs, cols),
                                          lambda j, i, slab=slab: (layer, slab(j, i), 0)))
        side_out_specs.append(pl.BlockSpec((rows // slabs, cols), lambda j, i, slab=slab: (slab(j, i), 0)))
        side_shapes.append(jax.ShapeDtypeStruct((rows, cols), BF16))
        blk += (rows // slabs) * cols * 6
    outs = pl.pallas_call(
        functools.partial(_matmul_kernel, n_side=len(side)),
        grid=(n_j, n_i),
        in_specs=[pl.BlockSpec((tm, k), lambda j, i: (i, 0)),
                  pl.BlockSpec((None, k, tn), lambda j, i: (layer, 0, j))] + side_in_specs,
        out_specs=[pl.BlockSpec((tm, tn), lambda j, i: (i, j))] + side_out_specs,
        out_shape=[jax.ShapeDtypeStruct((m, n), BF16)] + side_shapes,
        scratch_shapes=[pltpu.VMEM((k, tn), BF16)],
        compiler_params=_params(("arbitrary", "arbitrary"),
                                _vmem_limit(blk, k * tn * 2, 2 * tm * tn * 4)),
        name="in_proj",
    )(x, w, *side)
    return outs[0], outs[1:]


def _sb_kernel(q_ref, k_ref, v_ref, uu_ref, g_ref, o_ref, acc_ref, carry_ref, *, bq, n_heads, scale):
    i = pl.program_id(2)
    uu = uu_ref[...]
    row = lax.broadcasted_iota(jnp.int32, (bq, bq), 0)
    col = lax.broadcasted_iota(jnp.int32, (bq, bq), 1)
    causal = col < row

    def mask_diag(t):
        lead = t.shape[1] - bq
        tail = jnp.where(causal, t[:, lead:], 0.0)
        return tail if lead == 0 else jnp.concatenate([t[:, :lead], tail], axis=1)

    def block(start, width, diag):
        for hd in range(n_heads):
            lanes = slice(hd * HEAD_DIM, (hd + 1) * HEAD_DIM)
            q = q_ref[0, :, lanes]
            ks = k_ref[0, pl.ds(start, width), lanes]
            vs = v_ref[0, pl.ds(start, width), lanes]
            z = _dot_nt(q, ks) * (scale * LOG2_E)
            sp = jnp.maximum(z, 0.0) + jnp.log2(1.0 + jnp.exp2(-jnp.abs(z)))
            log2_beta = z - sp
            if diag:
                sp = mask_diag(sp)
            carry = jnp.zeros((bq, LANES), F32) if diag else carry_ref[hd]
            n_chunks = width // LANES
            between = [None] * n_chunks
            for c in reversed(range(n_chunks)):
                sp_c = sp[:, c * LANES:(c + 1) * LANES]
                hi = sp_c.astype(BF16)
                lo = (sp_c - hi.astype(F32)).astype(BF16)
                cs = _dot(jnp.concatenate([hi, lo], axis=1), uu)
                between[c] = cs[:, :LANES] + carry
                carry = carry + cs[:, LANES:]
            w = jnp.exp2(log2_beta - jnp.concatenate(between, axis=1))
            if diag:
                w = mask_diag(w)
            pv = _dot(w.astype(BF16), vs)
            acc_ref[hd] = pv if diag else acc_ref[hd] + pv
            carry_ref[hd] = carry

    def carry_min():
        m = carry_ref[0]
        for hd in range(1, n_heads):
            m = jnp.minimum(m, carry_ref[hd])
        return jnp.min(m)

    @pl.when(i == 0)
    def _():
        block(0, bq, True)

    @pl.when(i >= 1)
    def _():
        block(pl.multiple_of((i - 1) * bq, bq), 2 * bq, True)

    def live(state):
        j, lowest = state
        return jnp.logical_and(j >= 0, lowest < -F32_EXP2_ZERO_BELOW)

    def step(state):
        j, _ = state
        block(pl.multiple_of(j * bq, bq), bq, False)
        return j - 1, carry_min()

    lax.while_loop(live, step, (i - 2, carry_min()))

    for hd in range(n_heads):
        out = acc_ref[hd]
        y = out * lax.rsqrt(jnp.mean(out * out, axis=-1, keepdims=True) + RMS_EPS) * g_ref[...]
        o_ref[0, :, hd * HEAD_DIM:(hd + 1) * HEAD_DIM] = y.astype(o_ref.dtype)


def _sb_call(proj, uu, g, n_sb, bq, n_heads):
    b, s, _ = proj.shape
    width = n_heads * HEAD_DIM
    groups = n_sb // n_heads
    blk = (2 * bq * width + 2 * s * width) * 2 + uu.size * 2
    kern = functools.partial(_sb_kernel, bq=bq, n_heads=n_heads, scale=1.0 / math.sqrt(HEAD_DIM))
    return pl.pallas_call(
        kern,
        grid=(b, groups, s // bq),
        in_specs=[pl.BlockSpec((1, bq, width), lambda bi, h, i: (bi, i, h)),
                  pl.BlockSpec((1, s, width), lambda bi, h, i: (bi, 0, groups + h)),
                  pl.BlockSpec((1, s, width), lambda bi, h, i: (bi, 0, 2 * groups + h)),
                  pl.BlockSpec(uu.shape, lambda bi, h, i: (0, 0)),
                  pl.BlockSpec((1, HEAD_DIM), lambda bi, h, i: (0, 0))],
        out_specs=pl.BlockSpec((1, bq, width), lambda bi, h, i: (bi, i, h)),
        out_shape=jax.ShapeDtypeStruct((b, s, n_sb * HEAD_DIM), BF16),
        scratch_shapes=[pltpu.VMEM((n_heads, bq, HEAD_DIM), F32), pltpu.VMEM((n_heads, bq, LANES), F32)],
        compiler_params=_params(("parallel", "parallel", "arbitrary"),
                                _vmem_limit(blk, 2 * n_heads * bq * LANES * 4, 16 * n_heads * bq * bq * 4)),
        name="sb_attn",
    )(proj, proj, proj, uu, g.reshape(1, HEAD_DIM))


def _cumsum_matrix():
    j = jnp.arange(LANES)[:, None]
    s = jnp.arange(LANES)[None, :]
    u = (j > s).astype(BF16)
    blk = jnp.concatenate([u, jnp.ones((LANES, LANES), BF16)], axis=1)
    return jnp.concatenate([blk, blk], axis=0)


def _bias_kernel(rb_ref, o_ref, *, bq):
    h = pl.program_id(0)
    t = lax.broadcasted_iota(jnp.int32, (bq, bq), 0)
    s = lax.broadcasted_iota(jnp.int32, (bq, bq), 1)
    max_exact = N_BUCKETS // 2
    far = rb_ref[N_BUCKETS - 1, h]
    for which in range(2):
        dist = t - s + which * bq
        n = jnp.maximum(dist, 0)
        nf = jnp.maximum(n, 1).astype(F32)
        large = max_exact + (jnp.log(nf / max_exact) / math.log(MAX_DISTANCE / max_exact)
                             * (N_BUCKETS - max_exact)).astype(jnp.int32)
        large = jnp.minimum(large, N_BUCKETS - 1)
        bucket = jnp.where(n < max_exact, n, large)
        bias = jnp.zeros((bq, bq), F32)
        for bkt in range(N_BUCKETS):
            bias = jnp.where(bucket == bkt, (rb_ref[bkt, h] - far) * LOG2_E, bias)
        o_ref[0, which] = bias


def _bias_call(rel_bias, bq):
    n_df = rel_bias.shape[1]
    return pl.pallas_call(
        functools.partial(_bias_kernel, bq=bq),
        grid=(n_df,),
        in_specs=[pl.BlockSpec(memory_space=pltpu.SMEM)],
        out_specs=pl.BlockSpec((1, 2, bq, bq), lambda h: (h, 0, 0, 0)),
        out_shape=jax.ShapeDtypeStruct((n_df, 2, bq, bq), F32),
        compiler_params=_params(("arbitrary",), _vmem_limit(2 * bq * bq * 4, 0, 8 * bq * bq * 4)),
        name="t5_bias",
    )(rel_bias)


def _df_kernel(lq1_ref, lk1_ref, lq2_ref, lk2_ref, q_ref, k_ref, v_ref, bias_ref, g_ref,
               o_ref, s_ref, mx_ref, ls_ref, acc_ref, *, bq, n_heads, scale, lam_init):
    i = pl.program_id(2)
    dv = 2 * HEAD_DIM
    mx_ref[...] = jnp.full_like(mx_ref, -jnp.inf)
    row = lax.broadcasted_iota(jnp.int32, (bq, bq), 0)
    col = lax.broadcasted_iota(jnp.int32, (bq, bq), 1)

    def scores(start, width, mode):
        for hd in range(n_heads):
            for mp in range(2):
                u = 2 * hd + mp
                lanes = slice(u * HEAD_DIM, (u + 1) * HEAD_DIM)
                s = _dot_nt(q_ref[0, :, lanes], k_ref[0, pl.ds(start, width), lanes]) * (scale * LOG2_E)
                if mode == "diag":
                    s = jnp.where(row >= col, s + bias_ref[hd, 0], NEG_BIG)
                elif mode == "near+diag":
                    s = jnp.concatenate(
                        [s[:, :bq] + bias_ref[hd, 1],
                         jnp.where(row >= col, s[:, bq:] + bias_ref[hd, 0], NEG_BIG)], axis=1)
                s_ref[u, :, pl.ds(start, width)] = s
                mx = mx_ref[u]
                for c in range(width // LANES):
                    mx = jnp.maximum(mx, s[:, c * LANES:(c + 1) * LANES])
                if mode == "far":
                    mx_ref[u] = mx
                else:
                    mx_ref[u] = jnp.broadcast_to(jnp.max(mx, axis=-1, keepdims=True), (bq, LANES))
                    ls_ref[u] = jnp.zeros((bq, LANES), F32)
                    acc_ref[u] = jnp.zeros((bq, dv), F32)

    def values(start, width):
        for hd in range(n_heads):
            vv = v_ref[0, pl.ds(start, width), hd * dv:(hd + 1) * dv]
            for mp in range(2):
                u = 2 * hd + mp
                s = s_ref[u, :, pl.ds(start, width)]
                row_max = mx_ref[u]
                ls = ls_ref[u]
                ps = []
                for c in range(width // LANES):
                    p = jnp.exp2(s[:, c * LANES:(c + 1) * LANES] - row_max)
                    ls = ls + p
                    ps.append(p.astype(BF16))
                ls_ref[u] = ls
                acc_ref[u] += _dot(jnp.concatenate(ps, axis=1), vv)

    def looped(n_blocks, fn):
        def body(t, _):
            fn(pl.multiple_of(t * 8 * bq, 8 * bq), 8 * bq)
            return 0

        lax.fori_loop(0, lax.shift_right_logical(n_blocks, 3), body, 0)
        for span in (4, 2, 1):
            @pl.when((n_blocks & span) == span)
            def _(span=span):
                done = n_blocks & ~(2 * span - 1)
                fn(pl.multiple_of(done * bq, span * bq), span * bq)

    looped(jnp.maximum(i - 1, 0), functools.partial(scores, mode="far"))

    @pl.when(i >= 1)
    def _():
        scores(pl.multiple_of((i - 1) * bq, bq), 2 * bq, "near+diag")

    @pl.when(i == 0)
    def _():
        scores(0, bq, "diag")

    looped(i + 1, values)

    lam = (jnp.exp(jnp.sum(lq1_ref[...] * lk1_ref[...], axis=-1, keepdims=True))
           - jnp.exp(jnp.sum(lq2_ref[...] * lk2_ref[...], axis=-1, keepdims=True)) + lam_init)
    for hd in range(n_heads):
        l0 = jnp.sum(ls_ref[2 * hd], axis=-1, keepdims=True)
        l1 = jnp.sum(ls_ref[2 * hd + 1], axis=-1, keepdims=True)
        out = acc_ref[2 * hd] / l0 - lam * (acc_ref[2 * hd + 1] / l1)
        y = out * lax.rsqrt(jnp.mean(out * out, axis=-1, keepdims=True) + RMS_EPS) * g_ref[...]
        o_ref[0, :, hd * dv:(hd + 1) * dv] = (y * (1.0 - lam_init)).astype(o_ref.dtype)


def _df_call(proj, lam_vecs, bias_tiles, g, n_sb, n_df, bq, n_heads, lam_init):
    b, s, _ = proj.shape
    dv = 2 * HEAD_DIM
    width = n_heads * dv
    groups = n_df // n_heads
    q_start = 3 * n_sb * HEAD_DIM
    assert q_start % width == 0 and n_df % n_heads == 0
    q_off = q_start // width
    k_off = q_off + groups
    v_off = k_off + groups
    blk = (2 * bq * width + 2 * s * width) * 2 + n_heads * 2 * bq * bq * 4
    kern = functools.partial(_df_kernel, bq=bq, n_heads=n_heads, scale=1.0 / math.sqrt(HEAD_DIM),
                             lam_init=lam_init)
    vec_spec = pl.BlockSpec((1, HEAD_DIM), lambda bi, h, i: (0, 0))
    n_soft = 2 * n_heads
    return pl.pallas_call(
        kern,
        grid=(b, groups, s // bq),
        in_specs=[vec_spec, vec_spec, vec_spec, vec_spec,
                  pl.BlockSpec((1, bq, width), lambda bi, h, i: (bi, i, q_off + h)),
                  pl.BlockSpec((1, s, width), lambda bi, h, i: (bi, 0, k_off + h)),
                  pl.BlockSpec((1, s, width), lambda bi, h, i: (bi, 0, v_off + h)),
                  pl.BlockSpec((n_heads, 2, bq, bq), lambda bi, h, i: (h, 0, 0, 0)),
                  pl.BlockSpec((1, dv), lambda bi, h, i: (0, 0))],
        out_specs=pl.BlockSpec((1, bq, width), lambda bi, h, i: (bi, i, h)),
        out_shape=jax.ShapeDtypeStruct((b, s, n_df * dv), BF16),
        scratch_shapes=[pltpu.VMEM((n_soft, bq, s), F32), pltpu.VMEM((n_soft, bq, LANES), F32),
                        pltpu.VMEM((n_soft, bq, LANES), F32), pltpu.VMEM((n_soft, bq, dv), F32)],
        compiler_params=_params(("parallel", "parallel", "arbitrary"),
                                _vmem_limit(blk, n_soft * bq * (s + dv + 2 * LANES) * 4,
                                            16 * n_heads * bq * bq * 4)),
        name="df_attn",
    )(*[v.reshape(1, HEAD_DIM) for v in lam_vecs], proj, proj, proj, bias_tiles,
      g.reshape(1, dv))


def _outproj_kernel(sb_ref, df_ref, w_ref, h_ref, g_ref, b_ref, o32_ref, o16_ref, *, alpha):
    tm, half = sb_ref.shape
    sub = tm // 4 if tm % 32 == 0 else tm
    for r in range(tm // sub):
        rows = slice(r * sub, (r + 1) * sub)
        mix = _dot(sb_ref[rows, :], w_ref[:half, :]) + _dot(df_ref[rows, :], w_ref[half:, :])
        y = _layernorm(alpha * h_ref[rows, :] + mix, g_ref[...], b_ref[...])
        o32_ref[rows, :] = y
        o16_ref[rows, :] = y.astype(BF16)


def _outproj_call(sb_o, df_o, w, h, g, b, tm, alpha):
    m, d = h.shape
    half = sb_o.shape[1]
    blk = tm * half * 2 * 2 + tm * d * (4 + 4 + 2) + w.size * 2
    row = lambda i: (i, 0)
    fixed = lambda i: (0, 0)
    return pl.pallas_call(
        functools.partial(_outproj_kernel, alpha=alpha),
        grid=(m // tm,),
        in_specs=[pl.BlockSpec((tm, half), row), pl.BlockSpec((tm, half), row),
                  pl.BlockSpec(w.shape, fixed),
                  pl.BlockSpec((tm, d), row),
                  pl.BlockSpec((1, d), fixed), pl.BlockSpec((1, d), fixed)],
        out_specs=[pl.BlockSpec((tm, d), row), pl.BlockSpec((tm, d), row)],
        out_shape=[jax.ShapeDtypeStruct((m, d), F32), jax.ShapeDtypeStruct((m, d), BF16)],
        compiler_params=_params(("parallel",), _vmem_limit(blk, 0, 4 * tm * d * 4)),
        name="out_proj_ln",
    )(sb_o, df_o, w, h, g.reshape(1, d), b.reshape(1, d))


def _mlp_kernel(x16_ref, x32_ref, wu_ref, wd_ref, g_ref, b_ref, o32_ref, o16_ref, acc0_ref, acc1_ref,
                *, alpha, n_tiles, n_f):
    i = pl.program_id(0)
    f = pl.program_id(1)
    accs = (acc0_ref, acc1_ref)
    rows_per_step = o32_ref.shape[0] // n_f

    @pl.when(jnp.logical_and(i == 0, f == 0))
    def _():
        acc1_ref[...] = jnp.zeros_like(acc1_ref)

    def finish_slice(prev_ref):
        rows = pl.ds(pl.multiple_of(f * rows_per_step, rows_per_step), rows_per_step)
        y = _layernorm(alpha * x32_ref[rows, :] + prev_ref[rows, :], g_ref[...], b_ref[...])
        o32_ref[rows, :] = y
        o16_ref[rows, :] = y.astype(BF16)

    for parity in range(2):
        cur_ref, prev_ref = accs[parity], accs[1 - parity]
        mine = jnp.logical_and(i < n_tiles, lax.rem(i, 2) == parity)

        @pl.when(jnp.logical_and(mine, f == 0))
        def _(cur_ref=cur_ref):
            cur_ref[...] = jnp.zeros_like(cur_ref)

        @pl.when(mine)
        def _(cur_ref=cur_ref, prev_ref=prev_ref):
            u = _dot(x16_ref[...], wu_ref[...])
            a = jnp.square(jnp.maximum(u, 0.0)).astype(BF16)
            cur_ref[...] += _dot(a, wd_ref[...])
            finish_slice(prev_ref)

    @pl.when(i == n_tiles)
    def _():
        finish_slice(accs[(n_tiles - 1) % 2])


def _mlp_call(x16, x32, wu, wd, g, b, tm, tf, alpha):
    m, d = x32.shape
    ff = wu.shape[1]
    n_tiles, n_f = m // tm, ff // tf
    assert tm % n_f == 0 and (tm // n_f) % 16 == 0, (tm, n_f)
    blk = tm * d * (2 + 4 + 4 + 2) + 2 * d * tf * 2
    lagged = lambda i, f: (jnp.maximum(i - 1, 0), 0)
    fixed = lambda i, f: (0, 0)
    f_idx = lambda i, f: jnp.where(i < n_tiles, f, n_f - 1)
    return pl.pallas_call(
        functools.partial(_mlp_kernel, alpha=alpha, n_tiles=n_tiles, n_f=n_f),
        grid=(n_tiles + 1, n_f),
        in_specs=[pl.BlockSpec((tm, d), lambda i, f: (jnp.minimum(i, n_tiles - 1), 0)),
                  pl.BlockSpec((tm, d), lagged),
                  pl.BlockSpec((d, tf), lambda i, f: (0, f_idx(i, f))),
                  pl.BlockSpec((tf, d), lambda i, f: (f_idx(i, f), 0)),
                  pl.BlockSpec((1, d), fixed), pl.BlockSpec((1, d), fixed)],
        out_specs=[pl.BlockSpec((tm, d), lagged), pl.BlockSpec((tm, d), lagged)],
        out_shape=[jax.ShapeDtypeStruct((m, d), F32), jax.ShapeDtypeStruct((m, d), BF16)],
        scratch_shapes=[pltpu.VMEM((tm, d), F32), pltpu.VMEM((tm, d), F32)],
        compiler_params=_params(("arbitrary", "arbitrary"),
                                _vmem_limit(blk, 2 * tm * d * 4, 2 * tm * tf * 4 + 2 * tm * d * 4)),
        name="mlp_ln",
    )(x16, x32, wu, wd, g.reshape(1, d), b.reshape(1, d))


def _tile(n, want):
    t = min(n, want)
    while n % t or t % LANES:
        t -= LANES
        assert t > 0, (n, want)
    return t


def kernel(x, ln0_g, ln0_b, w_in, w_out, sb_norm_g, lam_q1, lam_k1, lam_q2, lam_k2, diff_norm_g,
           rel_bias, ln1_g, ln1_b, w_up, w_down, ln2_g, ln2_b):
    b, s, d = x.shape
    depth = w_in.shape[0]
    n_sb = d // (2 * HEAD_DIM)
    n_df = d // (4 * HEAD_DIM)
    assert w_in.shape[2] == 3 * n_sb * HEAD_DIM + 3 * n_df * 2 * HEAD_DIM
    assert rel_bias.shape == (N_BUCKETS, n_df)
    alpha = (2 * depth) ** 0.25
    m = b * s

    tm_ln = _tile(m, 512)
    tm_proj, tn_proj = _tile(m, 1024), _tile(w_in.shape[2], 1024)
    tm_out = _tile(m, 512)
    tm_mlp, tf_mlp = _tile(m, 512), _tile(w_up.shape[2], 1024)
    bq_sb = _tile(s, 256)
    sb_heads = math.gcd(n_sb, 8)
    bq_df = _tile(s, 256)
    df_heads = math.gcd(n_df, 2)
    assert bq_df >= MAX_DISTANCE

    uu = _cumsum_matrix()
    bias_tiles = _bias_call(rel_bias, bq_df)

    h32, h16 = _ln_call(x.reshape(m, d), ln0_g, ln0_b, tm_ln)
    for l in range(depth):
        lam_init = 0.8 - 0.6 * math.exp(-0.3 * l)
        proj, (w_out16, w_up16, w_down16) = _matmul_call(h16, w_in, l, tm_proj, tn_proj,
                                                         (w_out, w_up, w_down))
        proj = proj.reshape(b, s, -1)
        sb_o = _sb_call(proj, uu, sb_norm_g[l], n_sb, bq_sb, sb_heads)
        df_o = _df_call(proj, (lam_q1[l], lam_k1[l], lam_q2[l], lam_k2[l]), bias_tiles,
                        diff_norm_g[l], n_sb, n_df, bq_df, df_heads, lam_init)
        h32, h16 = _outproj_call(sb_o.reshape(m, -1), df_o.reshape(m, -1), w_out16, h32,
                                 ln1_g[l], ln1_b[l], tm_out, alpha)
        h32, h16 = _mlp_call(h16, h32, w_up16, w_down16, ln2_g[l], ln2_b[l],
                             tm_mlp, tf_mlp, alpha)
    return h32.reshape(b, s, d)
```

```python
import functools
import math

import jax
import jax.numpy as jnp
from jax import lax
from jax.experimental import pallas as pl
from jax.experimental.pallas import tpu as pltpu

HEAD_DIM = 128
N_BUCKETS = 32
MAX_DISTANCE = 128
LN_EPS = 1e-5
RMS_EPS = 1e-5
NEG_BIG = -1e30
F32_EXP2_ZERO_BELOW = -150.5
LOG2_E = 1.4426950408889634

LANES = 128
BF16_SUBLANES = 16
V7X_VMEM_BYTES = 64 * 1024 * 1024
VMEM_LIMIT_FLOOR = 16 * 1024 * 1024
VMEM_COMPILER_RESERVE = 6 * 1024 * 1024

LN_ROWS = 512
PROJ_ROWS, PROJ_COLS = 1024, 1024
OUT_ROWS = 512
MLP_ROWS, MLP_HIDDEN = 512, 1024
SB_QUERY_ROWS, SB_HEADS_PER_STEP = 256, 8
DF_QUERY_ROWS, DF_HEADS_PER_STEP = 256, 2

F32 = jnp.float32
BF16 = jnp.bfloat16


def _vmem_limit(block_bytes, scratch_bytes, temp_bytes):
    need = 2 * block_bytes + scratch_bytes + temp_bytes
    return int(min(max(need, VMEM_LIMIT_FLOOR), V7X_VMEM_BYTES - VMEM_COMPILER_RESERVE))


def _params(semantics, vmem_bytes):
    return pltpu.CompilerParams(dimension_semantics=semantics, vmem_limit_bytes=vmem_bytes)


def _layernorm(v, g, b):
    mu = jnp.mean(v, axis=-1, keepdims=True)
    c = v - mu
    var = jnp.mean(c * c, axis=-1, keepdims=True)
    return c * lax.rsqrt(var + LN_EPS) * g + b


def _dot(a, b):
    return jnp.dot(a, b, preferred_element_type=F32)


def _dot_nt(a, b):
    return lax.dot_general(a, b, (((1,), (1,)), ((), ())), preferred_element_type=F32)


def _ln_kernel(x_ref, g_ref, b_ref, o32_ref, o16_ref):
    y = _layernorm(x_ref[...], g_ref[...], b_ref[...])
    o32_ref[...] = y
    o16_ref[...] = y.astype(BF16)


def _ln_call(x, g, b, tm):
    m, d = x.shape
    blk = tm * d * (4 + 4 + 2)
    return pl.pallas_call(
        _ln_kernel,
        grid=(m // tm,),
        in_specs=[pl.BlockSpec((tm, d), lambda i: (i, 0)),
                  pl.BlockSpec((1, d), lambda i: (0, 0)),
                  pl.BlockSpec((1, d), lambda i: (0, 0))],
        out_specs=[pl.BlockSpec((tm, d), lambda i: (i, 0)),
                   pl.BlockSpec((tm, d), lambda i: (i, 0))],
        out_shape=[jax.ShapeDtypeStruct((m, d), F32), jax.ShapeDtypeStruct((m, d), BF16)],
        compiler_params=_params(("parallel",), _vmem_limit(blk, 0, 4 * tm * d * 4)),
        name="ln_in",
    )(x, g.reshape(1, d), b.reshape(1, d))


def _matmul_kernel(*refs, n_side):
    x_ref, w_ref = refs[:2]
    side_in = refs[2:2 + n_side]
    o_ref = refs[2 + n_side]
    side_out = refs[3 + n_side:3 + 2 * n_side]
    w16_ref = refs[3 + 2 * n_side]

    @pl.when(pl.program_id(1) == 0)
    def _():
        w16_ref[...] = w_ref[...].astype(BF16)

    o_ref[...] = _dot(x_ref[...], w16_ref[...]).astype(o_ref.dtype)
    for src_ref, dst_ref in zip(side_in, side_out):
        dst_ref[...] = src_ref[...].astype(BF16)


def _slab_count(rows, steps):
    units = rows // BF16_SUBLANES
    return max(c for c in range(1, min(units, steps) + 1) if units % c == 0)


def _matmul_call(x, w, layer, tm, tn, side):
    m, k = x.shape
    n = w.shape[2]
    n_j, n_i = n // tn, m // tm
    blk = (tm * k + tm * tn) * 2 + k * tn * 4
    side_in_specs, side_out_specs, side_shapes = [], [], []
    for arr in side:
        rows, cols = arr.shape[1:]
        slabs = _slab_count(rows, n_j * n_i)
        slab = lambda j, i, slabs=slabs: jnp.minimum(j * n_i + i, slabs - 1)
        side_in_specs.append(pl.BlockSpec((None, rows // slabs, cols),
                                          lambda j, i, slab=slab: (layer, slab(j, i), 0)))
        side_out_specs.append(pl.BlockSpec((rows // slabs, cols), lambda j, i, slab=slab: (slab(j, i), 0)))
        side_shapes.append(jax.ShapeDtypeStruct((rows, cols), BF16))
        blk += (rows // slabs) * cols * 6
    outs = pl.pallas_call(
        functools.partial(_matmul_kernel, n_side=len(side)),
        grid=(n_j, n_i),
        in_specs=[pl.BlockSpec((tm, k), lambda j, i: (i, 0)),
                  pl.BlockSpec((None, k, tn), lambda j, i: (layer, 0, j))] + side_in_specs,
        out_specs=[pl.BlockSpec((tm, tn), lambda j, i: (i, j))] + side_out_specs,
        out_shape=[jax.ShapeDtypeStruct((m, n), BF16)] + side_shapes,
        scratch_shapes=[pltpu.VMEM((k, tn), BF16)],
        compiler_params=_params(("arbitrary", "arbitrary"),
                                _vmem_limit(blk, k * tn * 2, 2 * tm * tn * 4)),
        name="in_proj",
    )(x, w, *side)
    return outs[0], outs[1:]


def _sb_kernel(q_ref, k_ref, v_ref, uu_ref, g_ref, o_ref, acc_ref, carry_ref, *, bq, n_heads, scale):
    i = pl.program_id(2)
    uu = uu_ref[...]
    row = lax.broadcasted_iota(jnp.int32, (bq, bq), 0)
    col = lax.broadcasted_iota(jnp.int32, (bq, bq), 1)
    causal = col < row

    def mask_diag(t):
        lead = t.shape[1] - bq
        tail = jnp.where(causal, t[:, lead:], 0.0)
        return tail if lead == 0 else jnp.concatenate([t[:, :lead], tail], axis=1)

    def block(start, width, diag):
        for hd in range(n_heads):
            lanes = slice(hd * HEAD_DIM, (hd + 1) * HEAD_DIM)
            q = q_ref[0, :, lanes]
            ks = k_ref[0, pl.ds(start, width), lanes]
            vs = v_ref[0, pl.ds(start, width), lanes]
            z = _dot_nt(q, ks) * (scale * LOG2_E)
            neg_abs = lax.bitcast_convert_type(
                lax.bitcast_convert_type(z, jnp.uint32) | jnp.uint32(0x80000000), F32)
            sp = jnp.maximum(z, 0.0) + jnp.log2(1.0 + jnp.exp2(neg_abs))
            log2_beta = z - sp
            if diag:
                sp = mask_diag(sp)
            carry = jnp.zeros((bq, LANES), F32) if diag else carry_ref[hd]
            n_chunks = width // LANES
            between = [None] * n_chunks
            for c in reversed(range(n_chunks)):
                sp_c = sp[:, c * LANES:(c + 1) * LANES]
                hi = sp_c.astype(BF16)
                lo = (sp_c - hi.astype(F32)).astype(BF16)
                cs = _dot(jnp.concatenate([hi, lo], axis=1), uu)
                between[c] = cs[:, :LANES] + carry
                carry = carry + cs[:, LANES:]
            w = jnp.exp2(log2_beta - jnp.concatenate(between, axis=1))
            if diag:
                w = mask_diag(w)
            pv = _dot(w.astype(BF16), vs)
            acc_ref[hd] = pv if diag else acc_ref[hd] + pv
            carry_ref[hd] = carry

    def carry_min():
        m = carry_ref[0]
        for hd in range(1, n_heads):
            m = jnp.minimum(m, carry_ref[hd])
        return jnp.min(m)

    @pl.when(i == 0)
    def _():
        block(0, bq, True)

    @pl.when(i >= 1)
    def _():
        block(pl.multiple_of((i - 1) * bq, bq), 2 * bq, True)

    def live(state):
        j, lowest = state
        return jnp.logical_and(j >= 0, lowest < -F32_EXP2_ZERO_BELOW)

    def step(state):
        j, _ = state
        block(pl.multiple_of(j * bq, bq), bq, False)
        return j - 1, carry_min()

    lax.while_loop(live, step, (i - 2, carry_min()))

    for hd in range(n_heads):
        out = acc_ref[hd]
        y = out * lax.rsqrt(jnp.mean(out * out, axis=-1, keepdims=True) + RMS_EPS) * g_ref[...]
        o_ref[0, :, hd * HEAD_DIM:(hd + 1) * HEAD_DIM] = y.astype(o_ref.dtype)


def _sb_call(proj, uu, g, n_sb, bq, n_heads):
    b, s, _ = proj.shape
    width = n_heads * HEAD_DIM
    groups = n_sb // n_heads
    blk = (2 * bq * width + 2 * s * width) * 2 + uu.size * 2
    kern = functools.partial(_sb_kernel, bq=bq, n_heads=n_heads, scale=1.0 / math.sqrt(HEAD_DIM))
    return pl.pallas_call(
        kern,
        grid=(b, groups, s // bq),
        in_specs=[pl.BlockSpec((1, bq, width), lambda bi, h, i: (bi, i, h)),
                  pl.BlockSpec((1, s, width), lambda bi, h, i: (bi, 0, groups + h)),
                  pl.BlockSpec((1, s, width), lambda bi, h, i: (bi, 0, 2 * groups + h)),
                  pl.BlockSpec(uu.shape, lambda bi, h, i: (0, 0)),
                  pl.BlockSpec((1, HEAD_DIM), lambda bi, h, i: (0, 0))],
        out_specs=pl.BlockSpec((1, bq, width), lambda bi, h, i: (bi, i, h)),
        out_shape=jax.ShapeDtypeStruct((b, s, n_sb * HEAD_DIM), BF16),
        scratch_shapes=[pltpu.VMEM((n_heads, bq, HEAD_DIM), F32), pltpu.VMEM((n_heads, bq, LANES), F32)],
        compiler_params=_params(("parallel", "parallel", "arbitrary"),
                                _vmem_limit(blk, 2 * n_heads * bq * LANES * 4, 16 * n_heads * bq * bq * 4)),
        name="sb_attn",
    )(proj, proj, proj, uu, g.reshape(1, HEAD_DIM))


def _cumsum_matrix():
    j = jnp.arange(LANES)[:, None]
    s = jnp.arange(LANES)[None, :]
    u = (j > s).astype(BF16)
    blk = jnp.concatenate([u, jnp.ones((LANES, LANES), BF16)], axis=1)
    return jnp.concatenate([blk, blk], axis=0)


def _bias_kernel(rb_ref, o_ref, *, bq):
    h = pl.program_id(0)
    t = lax.broadcasted_iota(jnp.int32, (bq, bq), 0)
    s = lax.broadcasted_iota(jnp.int32, (bq, bq), 1)
    max_exact = N_BUCKETS // 2
    far = rb_ref[N_BUCKETS - 1, h]
    for which in range(2):
        dist = t - s + which * bq
        n = jnp.maximum(dist, 0)
        nf = jnp.maximum(n, 1).astype(F32)
        large = max_exact + (jnp.log(nf / max_exact) / math.log(MAX_DISTANCE / max_exact)
                             * (N_BUCKETS - max_exact)).astype(jnp.int32)
        large = jnp.minimum(large, N_BUCKETS - 1)
        bucket = jnp.where(n < max_exact, n, large)
        bias = jnp.zeros((bq, bq), F32)
        for bkt in range(N_BUCKETS):
            bias = jnp.where(bucket == bkt, (rb_ref[bkt, h] - far) * LOG2_E, bias)
        o_ref[0, which] = bias


def _bias_call(rel_bias, bq):
    n_df = rel_bias.shape[1]
    return pl.pallas_call(
        functools.partial(_bias_kernel, bq=bq),
        grid=(n_df,),
        in_specs=[pl.BlockSpec(memory_space=pltpu.SMEM)],
        out_specs=pl.BlockSpec((1, 2, bq, bq), lambda h: (h, 0, 0, 0)),
        out_shape=jax.ShapeDtypeStruct((n_df, 2, bq, bq), F32),
        compiler_params=_params(("arbitrary",), _vmem_limit(2 * bq * bq * 4, 0, 8 * bq * bq * 4)),
        name="t5_bias",
    )(rel_bias)


def _df_kernel(lq1_ref, lk1_ref, lq2_ref, lk2_ref, q_ref, k_ref, v_ref, bias_ref, g_ref,
               o_ref, s_ref, mx_ref, ls_ref, acc_ref, *, bq, n_heads, scale, lam_init):
    i = pl.program_id(2)
    dv = 2 * HEAD_DIM
    mx_ref[...] = jnp.full_like(mx_ref, -jnp.inf)
    row = lax.broadcasted_iota(jnp.int32, (bq, bq), 0)
    col = lax.broadcasted_iota(jnp.int32, (bq, bq), 1)

    def scores(start, width, mode):
        for hd in range(n_heads):
            for mp in range(2):
                u = 2 * hd + mp
                lanes = slice(u * HEAD_DIM, (u + 1) * HEAD_DIM)
                s = _dot_nt(q_ref[0, :, lanes], k_ref[0, pl.ds(start, width), lanes]) * (scale * LOG2_E)
                if mode == "diag":
                    s = jnp.where(row >= col, s + bias_ref[hd, 0], NEG_BIG)
                elif mode == "near+diag":
                    s = jnp.concatenate(
                        [s[:, :bq] + bias_ref[hd, 1],
                         jnp.where(row >= col, s[:, bq:] + bias_ref[hd, 0], NEG_BIG)], axis=1)
                s_ref[u, :, pl.ds(start, width)] = s
                mx = mx_ref[u]
                for c in range(width // LANES):
                    mx = jnp.maximum(mx, s[:, c * LANES:(c + 1) * LANES])
                if mode == "far":
                    mx_ref[u] = mx
                else:
                    mx_ref[u] = jnp.broadcast_to(jnp.max(mx, axis=-1, keepdims=True), (bq, LANES))
                    ls_ref[u] = jnp.zeros((bq, LANES), F32)
                    acc_ref[u] = jnp.zeros((bq, dv), F32)

    def values(start, width):
        for hd in range(n_heads):
            vv = v_ref[0, pl.ds(start, width), hd * dv:(hd + 1) * dv]
            for mp in range(2):
                u = 2 * hd + mp
                s = s_ref[u, :, pl.ds(start, width)]
                row_max = mx_ref[u]
                ls = ls_ref[u]
                ps = []
                for c in range(width // LANES):
                    p = jnp.exp2(s[:, c * LANES:(c + 1) * LANES] - row_max)
                    ls = ls + p
                    ps.append(p.astype(BF16))
                ls_ref[u] = ls
                acc_ref[u] += _dot(jnp.concatenate(ps, axis=1), vv)

    def looped(n_blocks, fn):
        def body(t, _):
            fn(pl.multiple_of(t * 8 * bq, 8 * bq), 8 * bq)
            return 0

        lax.fori_loop(0, lax.shift_right_logical(n_blocks, 3), body, 0)
        for span in (4, 2, 1):
            @pl.when((n_blocks & span) == span)
            def _(span=span):
                done = n_blocks & ~(2 * span - 1)
                fn(pl.multiple_of(done * bq, span * bq), span * bq)

    looped(jnp.maximum(i - 1, 0), functools.partial(scores, mode="far"))

    @pl.when(i >= 1)
    def _():
        scores(pl.multiple_of((i - 1) * bq, bq), 2 * bq, "near+diag")

    @pl.when(i == 0)
    def _():
        scores(0, bq, "diag")

    looped(i + 1, values)

    lam = (jnp.exp(jnp.sum(lq1_ref[...] * lk1_ref[...], axis=-1, keepdims=True))
           - jnp.exp(jnp.sum(lq2_ref[...] * lk2_ref[...], axis=-1, keepdims=True)) + lam_init)
    for hd in range(n_heads):
        l0 = jnp.sum(ls_ref[2 * hd], axis=-1, keepdims=True)
        l1 = jnp.sum(ls_ref[2 * hd + 1], axis=-1, keepdims=True)
        out = acc_ref[2 * hd] / l0 - lam * (acc_ref[2 * hd + 1] / l1)
        y = out * lax.rsqrt(jnp.mean(out * out, axis=-1, keepdims=True) + RMS_EPS) * g_ref[...]
        o_ref[0, :, hd * dv:(hd + 1) * dv] = (y * (1.0 - lam_init)).astype(o_ref.dtype)


def _df_call(proj, lam_vecs, bias_tiles, g, n_sb, n_df, bq, n_heads, lam_init):
    b, s, _ = proj.shape
    dv = 2 * HEAD_DIM
    width = n_heads * dv
    groups = n_df // n_heads
    q_start = 3 * n_sb * HEAD_DIM
    assert q_start % width == 0 and n_df % n_heads == 0
    q_off = q_start // width
    k_off = q_off + groups
    v_off = k_off + groups
    blk = (2 * bq * width + 2 * s * width) * 2 + n_heads * 2 * bq * bq * 4
    kern = functools.partial(_df_kernel, bq=bq, n_heads=n_heads, scale=1.0 / math.sqrt(HEAD_DIM),
                             lam_init=lam_init)
    vec_spec = pl.BlockSpec((1, HEAD_DIM), lambda bi, h, i: (0, 0))
    n_soft = 2 * n_heads
    return pl.pallas_call(
        kern,
        grid=(b, groups, s // bq),
        in_specs=[vec_spec, vec_spec, vec_spec, vec_spec,
                  pl.BlockSpec((1, bq, width), lambda bi, h, i: (bi, i, q_off + h)),
                  pl.BlockSpec((1, s, width), lambda bi, h, i: (bi, 0, k_off + h)),
                  pl.BlockSpec((1, s, width), lambda bi, h, i: (bi, 0, v_off + h)),
                  pl.BlockSpec((n_heads, 2, bq, bq), lambda bi, h, i: (h, 0, 0, 0)),
                  pl.BlockSpec((1, dv), lambda bi, h, i: (0, 0))],
        out_specs=pl.BlockSpec((1, bq, width), lambda bi, h, i: (bi, i, h)),
        out_shape=jax.ShapeDtypeStruct((b, s, n_df * dv), BF16),
        scratch_shapes=[pltpu.VMEM((n_soft, bq, s), F32), pltpu.VMEM((n_soft, bq, LANES), F32),
                        pltpu.VMEM((n_soft, bq, LANES), F32), pltpu.VMEM((n_soft, bq, dv), F32)],
        compiler_params=_params(("parallel", "parallel", "arbitrary"),
                                _vmem_limit(blk, n_soft * bq * (s + dv + 2 * LANES) * 4,
                                            16 * n_heads * bq * bq * 4)),
        name="df_attn",
    )(*[v.reshape(1, HEAD_DIM) for v in lam_vecs], proj, proj, proj, bias_tiles,
      g.reshape(1, dv))


def _outproj_kernel(sb_ref, df_ref, w_ref, h_ref, g_ref, b_ref, o32_ref, o16_ref, *, alpha):
    tm, half = sb_ref.shape
    sub = tm // 4 if tm % 32 == 0 else tm
    for r in range(tm // sub):
        rows = slice(r * sub, (r + 1) * sub)
        mix = _dot(sb_ref[rows, :], w_ref[:half, :]) + _dot(df_ref[rows, :], w_ref[half:, :])
        y = _layernorm(alpha * h_ref[rows, :] + mix, g_ref[...], b_ref[...])
        o32_ref[rows, :] = y
        o16_ref[rows, :] = y.astype(BF16)


def _outproj_call(sb_o, df_o, w, h, g, b, tm, alpha):
    m, d = h.shape
    half = sb_o.shape[1]
    blk = tm * half * 2 * 2 + tm * d * (4 + 4 + 2) + w.size * 2
    row = lambda i: (i, 0)
    fixed = lambda i: (0, 0)
    return pl.pallas_call(
        functools.partial(_outproj_kernel, alpha=alpha),
        grid=(m // tm,),
        in_specs=[pl.BlockSpec((tm, half), row), pl.BlockSpec((tm, half), row),
                  pl.BlockSpec(w.shape, fixed),
                  pl.BlockSpec((tm, d), row),
                  pl.BlockSpec((1, d), fixed), pl.BlockSpec((1, d), fixed)],
        out_specs=[pl.BlockSpec((tm, d), row), pl.BlockSpec((tm, d), row)],
        out_shape=[jax.ShapeDtypeStruct((m, d), F32), jax.ShapeDtypeStruct((m, d), BF16)],
        compiler_params=_params(("parallel",), _vmem_limit(blk, 0, 4 * tm * d * 4)),
        name="out_proj_ln",
    )(sb_o, df_o, w, h, g.reshape(1, d), b.reshape(1, d))


def _mlp_kernel(x16_ref, x32_ref, wu_ref, wd_ref, g_ref, b_ref, o32_ref, o16_ref, acc0_ref, acc1_ref,
                *, alpha, n_tiles, n_f):
    i = pl.program_id(0)
    f = pl.program_id(1)
    accs = (acc0_ref, acc1_ref)
    rows_per_step = o32_ref.shape[0] // n_f

    @pl.when(jnp.logical_and(i == 0, f == 0))
    def _():
        acc1_ref[...] = jnp.zeros_like(acc1_ref)

    def finish_slice(prev_ref):
        rows = pl.ds(pl.multiple_of(f * rows_per_step, rows_per_step), rows_per_step)
        y = _layernorm(alpha * x32_ref[rows, :] + prev_ref[rows, :], g_ref[...], b_ref[...])
        o32_ref[rows, :] = y
        o16_ref[rows, :] = y.astype(BF16)

    for parity in range(2):
        cur_ref, prev_ref = accs[parity], accs[1 - parity]
        mine = jnp.logical_and(i < n_tiles, lax.rem(i, 2) == parity)

        @pl.when(jnp.logical_and(mine, f == 0))
        def _(cur_ref=cur_ref):
            cur_ref[...] = jnp.zeros_like(cur_ref)

        @pl.when(mine)
        def _(cur_ref=cur_ref, prev_ref=prev_ref):
            u = _dot(x16_ref[...], wu_ref[...])
            a = jnp.square(jnp.maximum(u, 0.0)).astype(BF16)
            cur_ref[...] += _dot(a, wd_ref[...])
            finish_slice(prev_ref)

    @pl.when(i == n_tiles)
    def _():
        finish_slice(accs[(n_tiles - 1) % 2])


def _mlp_call(x16, x32, wu, wd, g, b, tm, tf, alpha):
    m, d = x32.shape
    ff = wu.shape[1]
    n_tiles, n_f = m // tm, ff // tf
    assert tm % n_f == 0 and (tm // n_f) % BF16_SUBLANES == 0, (tm, n_f)
    blk = tm * d * (2 + 4 + 4 + 2) + 2 * d * tf * 2
    lagged = lambda i, f: (jnp.maximum(i - 1, 0), 0)
    fixed = lambda i, f: (0, 0)
    f_idx = lambda i, f: jnp.where(i < n_tiles, f, n_f - 1)
    return pl.pallas_call(
        functools.partial(_mlp_kernel, alpha=alpha, n_tiles=n_tiles, n_f=n_f),
        grid=(n_tiles + 1, n_f),
        in_specs=[pl.BlockSpec((tm, d), lambda i, f: (jnp.minimum(i, n_tiles - 1), 0)),
                  pl.BlockSpec((tm, d), lagged),
                  pl.BlockSpec((d, tf), lambda i, f: (0, f_idx(i, f))),
                  pl.BlockSpec((tf, d), lambda i, f: (f_idx(i, f), 0)),
                  pl.BlockSpec((1, d), fixed), pl.BlockSpec((1, d), fixed)],
        out_specs=[pl.BlockSpec((tm, d), lagged), pl.BlockSpec((tm, d), lagged)],
        out_shape=[jax.ShapeDtypeStruct((m, d), F32), jax.ShapeDtypeStruct((m, d), BF16)],
        scratch_shapes=[pltpu.VMEM((tm, d), F32), pltpu.VMEM((tm, d), F32)],
        compiler_params=_params(("arbitrary", "arbitrary"),
                                _vmem_limit(blk, 2 * tm * d * 4, 2 * tm * tf * 4 + 2 * tm * d * 4)),
        name="mlp_ln",
    )(x16, x32, wu, wd, g.reshape(1, d), b.reshape(1, d))


def _tile(n, want):
    t = min(n, want)
    while n % t or t % LANES:
        t -= LANES
        assert t > 0, (n, want)
    return t


def kernel(x, ln0_g, ln0_b, w_in, w_out, sb_norm_g, lam_q1, lam_k1, lam_q2, lam_k2, diff_norm_g,
           rel_bias, ln1_g, ln1_b, w_up, w_down, ln2_g, ln2_b):
    b, s, d = x.shape
    depth = w_in.shape[0]
    n_sb = d // (2 * HEAD_DIM)
    n_df = d // (4 * HEAD_DIM)
    assert w_in.shape[2] == 3 * n_sb * HEAD_DIM + 3 * n_df * 2 * HEAD_DIM
    assert rel_bias.shape == (N_BUCKETS, n_df)
    alpha = (2 * depth) ** 0.25
    m = b * s

    tm_ln = _tile(m, LN_ROWS)
    tm_proj, tn_proj = _tile(m, PROJ_ROWS), _tile(w_in.shape[2], PROJ_COLS)
    tm_out = _tile(m, OUT_ROWS)
    tm_mlp, tf_mlp = _tile(m, MLP_ROWS), _tile(w_up.shape[2], MLP_HIDDEN)
    bq_sb = _tile(s, SB_QUERY_ROWS)
    sb_heads = math.gcd(n_sb, SB_HEADS_PER_STEP)
    bq_df = _tile(s, DF_QUERY_ROWS)
    df_heads = math.gcd(n_df, DF_HEADS_PER_STEP)
    assert bq_df >= MAX_DISTANCE

    uu = _cumsum_matrix()
    bias_tiles = _bias_call(rel_bias, bq_df)

    h32, h16 = _ln_call(x.reshape(m, d), ln0_g, ln0_b, tm_ln)
    for l in range(depth):
        lam_init = 0.8 - 0.6 * math.exp(-0.3 * l)
        proj, (w_out16, w_up16, w_down16) = _matmul_call(h16, w_in, l, tm_proj, tn_proj,
                                                         (w_out, w_up, w_down))
        proj = proj.reshape(b, s, -1)
        sb_o = _sb_call(proj, uu, sb_norm_g[l], n_sb, bq_sb, sb_heads)
        df_o = _df_call(proj, (lam_q1[l], lam_k1[l], lam_q2[l], lam_k2[l]), bias_tiles,
                        diff_norm_g[l], n_sb, n_df, bq_df, df_heads, lam_init)
        h32, h16 = _outproj_call(sb_o.reshape(m, -1), df_o.reshape(m, -1), w_out16, h32,
                                 ln1_g[l], ln1_b[l], tm_out, alpha)
        h32, h16 = _mlp_call(h16, h32, w_up16, w_down16, ln2_g[l], ln2_b[l],
                             tm_mlp, tf_mlp, alpha)
    return h32.reshape(b, s, d)
```

```python
import functools
import math

import jax
import jax.numpy as jnp
from jax import lax
from jax.experimental import pallas as pl
from jax.experimental.pallas import tpu as pltpu

HEAD_DIM = 128
N_BUCKETS = 32
MAX_DISTANCE = 128
LN_EPS = 1e-5
RMS_EPS = 1e-5
NEG_BIG = -1e30
F32_EXP2_ZERO_BELOW = -150.5
LOG2_E = 1.4426950408889634

LANES = 128
BF16_SUBLANES = 16
V7X_VMEM_BYTES = 64 * 1024 * 1024
VMEM_LIMIT_FLOOR = 16 * 1024 * 1024
VMEM_COMPILER_RESERVE = 6 * 1024 * 1024

LN_ROWS = 512
PROJ_ROWS, PROJ_COLS = 1024, 1536
OUT_ROWS = 512
MLP_ROWS, MLP_HIDDEN = 512, 1024
SB_QUERY_ROWS, SB_HEADS_PER_STEP = 256, 8
DF_QUERY_ROWS, DF_HEADS_PER_STEP = 256, 2

F32 = jnp.float32
BF16 = jnp.bfloat16


def _vmem_limit(block_bytes, scratch_bytes, temp_bytes):
    need = 2 * block_bytes + scratch_bytes + temp_bytes
    return int(min(max(need, VMEM_LIMIT_FLOOR), V7X_VMEM_BYTES - VMEM_COMPILER_RESERVE))


def _params(semantics, vmem_bytes):
    return pltpu.CompilerParams(dimension_semantics=semantics, vmem_limit_bytes=vmem_bytes)


def _layernorm(v, g, b):
    mu = jnp.mean(v, axis=-1, keepdims=True)
    c = v - mu
    var = jnp.mean(c * c, axis=-1, keepdims=True)
    return c * lax.rsqrt(var + LN_EPS) * g + b


def _dot(a, b):
    return jnp.dot(a, b, preferred_element_type=F32)


def _dot_nt(a, b):
    return lax.dot_general(a, b, (((1,), (1,)), ((), ())), preferred_element_type=F32)


def _ln_kernel(x_ref, g_ref, b_ref, o32_ref, o16_ref):
    y = _layernorm(x_ref[...], g_ref[...], b_ref[...])
    o32_ref[...] = y
    o16_ref[...] = y.astype(BF16)


def _ln_call(x, g, b, tm):
    m, d = x.shape
    blk = tm * d * (4 + 4 + 2)
    return pl.pallas_call(
        _ln_kernel,
        grid=(m // tm,),
        in_specs=[pl.BlockSpec((tm, d), lambda i: (i, 0)),
                  pl.BlockSpec((1, d), lambda i: (0, 0)),
                  pl.BlockSpec((1, d), lambda i: (0, 0))],
        out_specs=[pl.BlockSpec((tm, d), lambda i: (i, 0)),
                   pl.BlockSpec((tm, d), lambda i: (i, 0))],
        out_shape=[jax.ShapeDtypeStruct((m, d), F32), jax.ShapeDtypeStruct((m, d), BF16)],
        compiler_params=_params(("parallel",), _vmem_limit(blk, 0, 4 * tm * d * 4)),
        name="ln_in",
    )(x, g.reshape(1, d), b.reshape(1, d))


def _matmul_kernel(*refs, n_side):
    x_ref, w_ref = refs[:2]
    side_in = refs[2:2 + n_side]
    o_ref = refs[2 + n_side]
    side_out = refs[3 + n_side:3 + 2 * n_side]
    w16_ref = refs[3 + 2 * n_side]

    @pl.when(pl.program_id(1) == 0)
    def _():
        w16_ref[...] = w_ref[...].astype(BF16)

    o_ref[...] = _dot(x_ref[...], w16_ref[...]).astype(o_ref.dtype)
    for src_ref, dst_ref in zip(side_in, side_out):
        dst_ref[...] = src_ref[...].astype(BF16)


def _slab_count(rows, steps):
    units = rows // BF16_SUBLANES
    return max(c for c in range(1, min(units, steps) + 1) if units % c == 0)


def _matmul_call(x, w, layer, tm, tn, side):
    m, k = x.shape
    n = w.shape[2]
    n_j, n_i = n // tn, m // tm
    blk = (tm * k + tm * tn) * 2 + k * tn * 4
    side_in_specs, side_out_specs, side_shapes = [], [], []
    for arr in side:
        rows, cols = arr.shape[1:]
        slabs = _slab_count(rows, n_j * n_i)
        slab = lambda j, i, slabs=slabs: jnp.minimum(j * n_i + i, slabs - 1)
        side_in_specs.append(pl.BlockSpec((None, rows // slabs, cols),
                                          lambda j, i, slab=slab: (layer, slab(j, i), 0)))
        side_out_specs.append(pl.BlockSpec((rows // slabs, cols), lambda j, i, slab=slab: (slab(j, i), 0)))
        side_shapes.append(jax.ShapeDtypeStruct((rows, cols), BF16))
        blk += (rows // slabs) * cols * 6
    outs = pl.pallas_call(
        functools.partial(_matmul_kernel, n_side=len(side)),
        grid=(n_j, n_i),
        in_specs=[pl.BlockSpec((tm, k), lambda j, i: (i, 0)),
                  pl.BlockSpec((None, k, tn), lambda j, i: (layer, 0, j))] + side_in_specs,
        out_specs=[pl.BlockSpec((tm, tn), lambda j, i: (i, j))] + side_out_specs,
        out_shape=[jax.ShapeDtypeStruct((m, n), BF16)] + side_shapes,
        scratch_shapes=[pltpu.VMEM((k, tn), BF16)],
        compiler_params=_params(("arbitrary", "arbitrary"),
                                _vmem_limit(blk, k * tn * 2, 2 * tm * tn * 4)),
        name="in_proj",
    )(x, w, *side)
    return outs[0], outs[1:]


def _sb_kernel(q_ref, k_ref, v_ref, uu_ref, g_ref, o_ref, acc_ref, carry_ref, *, bq, n_heads, scale):
    i = pl.program_id(2)
    uu = uu_ref[...]
    row = lax.broadcasted_iota(jnp.int32, (bq, bq), 0)
    col = lax.broadcasted_iota(jnp.int32, (bq, bq), 1)
    causal = col < row

    def mask_diag(t):
        lead = t.shape[1] - bq
        tail = jnp.where(causal, t[:, lead:], 0.0)
        return tail if lead == 0 else jnp.concatenate([t[:, :lead], tail], axis=1)

    def block(start, width, diag):
        for hd in range(n_heads):
            lanes = slice(hd * HEAD_DIM, (hd + 1) * HEAD_DIM)
            q = q_ref[0, :, lanes]
            ks = k_ref[0, pl.ds(start, width), lanes]
            vs = v_ref[0, pl.ds(start, width), lanes]
            z = _dot_nt(q, ks) * (scale * LOG2_E)
            neg_abs = lax.bitcast_convert_type(
                lax.bitcast_convert_type(z, jnp.uint32) | jnp.uint32(0x80000000), F32)
            sp = jnp.maximum(z, 0.0) + jnp.log2(1.0 + jnp.exp2(neg_abs))
            log2_beta = z - sp
            if diag:
                sp = mask_diag(sp)
            carry = jnp.zeros((bq, LANES), F32) if diag else carry_ref[hd]
            n_chunks = width // LANES
            between = [None] * n_chunks
            for c in reversed(range(n_chunks)):
                sp_c = sp[:, c * LANES:(c + 1) * LANES]
                hi = sp_c.astype(BF16)
                lo = (sp_c - hi.astype(F32)).astype(BF16)
                cs = _dot(jnp.concatenate([hi, lo], axis=1), uu)
                between[c] = cs[:, :LANES] + carry
                carry = carry + cs[:, LANES:]
            w = jnp.exp2(log2_beta - jnp.concatenate(between, axis=1))
            if diag:
                w = mask_diag(w)
            pv = _dot(w.astype(BF16), vs)
            acc_ref[hd] = pv if diag else acc_ref[hd] + pv
            carry_ref[hd] = carry

    def carry_min():
        m = carry_ref[0]
        for hd in range(1, n_heads):
            m = jnp.minimum(m, carry_ref[hd])
        return jnp.min(m)

    @pl.when(i == 0)
    def _():
        block(0, bq, True)

    @pl.when(i >= 1)
    def _():
        block(pl.multiple_of((i - 1) * bq, bq), 2 * bq, True)

    def live(state):
        j, lowest = state
        return jnp.logical_and(j >= 0, lowest < -F32_EXP2_ZERO_BELOW)

    def step(state):
        j, _ = state
        block(pl.multiple_of(j * bq, bq), bq, False)
        return j - 1, carry_min()

    lax.while_loop(live, step, (i - 2, carry_min()))

    for hd in range(n_heads):
        out = acc_ref[hd]
        y = out * lax.rsqrt(jnp.mean(out * out, axis=-1, keepdims=True) + RMS_EPS) * g_ref[...]
        o_ref[0, :, hd * HEAD_DIM:(hd + 1) * HEAD_DIM] = y.astype(o_ref.dtype)


def _sb_call(proj, uu, g, n_sb, bq, n_heads):
    b, s, _ = proj.shape
    width = n_heads * HEAD_DIM
    groups = n_sb // n_heads
    blk = (2 * bq * width + 2 * s * width) * 2 + uu.size * 2
    kern = functools.partial(_sb_kernel, bq=bq, n_heads=n_heads, scale=1.0 / math.sqrt(HEAD_DIM))
    return pl.pallas_call(
        kern,
        grid=(b, groups, s // bq),
        in_specs=[pl.BlockSpec((1, bq, width), lambda bi, h, i: (bi, i, h)),
                  pl.BlockSpec((1, s, width), lambda bi, h, i: (bi, 0, groups + h)),
                  pl.BlockSpec((1, s, width), lambda bi, h, i: (bi, 0, 2 * groups + h)),
                  pl.BlockSpec(uu.shape, lambda bi, h, i: (0, 0)),
                  pl.BlockSpec((1, HEAD_DIM), lambda bi, h, i: (0, 0))],
        out_specs=pl.BlockSpec((1, bq, width), lambda bi, h, i: (bi, i, h)),
        out_shape=jax.ShapeDtypeStruct((b, s, n_sb * HEAD_DIM), BF16),
        scratch_shapes=[pltpu.VMEM((n_heads, bq, HEAD_DIM), F32), pltpu.VMEM((n_heads, bq, LANES), F32)],
        compiler_params=_params(("parallel", "parallel", "arbitrary"),
                                _vmem_limit(blk, 2 * n_heads * bq * LANES * 4, 16 * n_heads * bq * bq * 4)),
        name="sb_attn",
    )(proj, proj, proj, uu, g.reshape(1, HEAD_DIM))


def _cumsum_matrix():
    j = jnp.arange(LANES)[:, None]
    s = jnp.arange(LANES)[None, :]
    u = (j > s).astype(BF16)
    blk = jnp.concatenate([u, jnp.ones((LANES, LANES), BF16)], axis=1)
    return jnp.concatenate([blk, blk], axis=0)


def _bias_kernel(rb_ref, o_ref, *, bq):
    h = pl.program_id(0)
    t = lax.broadcasted_iota(jnp.int32, (bq, bq), 0)
    s = lax.broadcasted_iota(jnp.int32, (bq, bq), 1)
    max_exact = N_BUCKETS // 2
    far = rb_ref[N_BUCKETS - 1, h]
    for which in range(2):
        dist = t - s + which * bq
        n = jnp.maximum(dist, 0)
        nf = jnp.maximum(n, 1).astype(F32)
        large = max_exact + (jnp.log(nf / max_exact) / math.log(MAX_DISTANCE / max_exact)
                             * (N_BUCKETS - max_exact)).astype(jnp.int32)
        large = jnp.minimum(large, N_BUCKETS - 1)
        bucket = jnp.where(n < max_exact, n, large)
        bias = jnp.zeros((bq, bq), F32)
        for bkt in range(N_BUCKETS):
            bias = jnp.where(bucket == bkt, (rb_ref[bkt, h] - far) * LOG2_E, bias)
        o_ref[0, which] = bias


def _bias_call(rel_bias, bq):
    n_df = rel_bias.shape[1]
    return pl.pallas_call(
        functools.partial(_bias_kernel, bq=bq),
        grid=(n_df,),
        in_specs=[pl.BlockSpec(memory_space=pltpu.SMEM)],
        out_specs=pl.BlockSpec((1, 2, bq, bq), lambda h: (h, 0, 0, 0)),
        out_shape=jax.ShapeDtypeStruct((n_df, 2, bq, bq), F32),
        compiler_params=_params(("arbitrary",), _vmem_limit(2 * bq * bq * 4, 0, 8 * bq * bq * 4)),
        name="t5_bias",
    )(rel_bias)


def _df_kernel(lq1_ref, lk1_ref, lq2_ref, lk2_ref, q_ref, k_ref, v_ref, bias_ref, g_ref,
               o_ref, s_ref, mx_ref, ls_ref, acc_ref, *, bq, n_heads, scale, lam_init):
    i = pl.program_id(2)
    dv = 2 * HEAD_DIM
    mx_ref[...] = jnp.full_like(mx_ref, -jnp.inf)
    row = lax.broadcasted_iota(jnp.int32, (bq, bq), 0)
    col = lax.broadcasted_iota(jnp.int32, (bq, bq), 1)

    def scores(start, width, mode):
        for hd in range(n_heads):
            for mp in range(2):
                u = 2 * hd + mp
                lanes = slice(u * HEAD_DIM, (u + 1) * HEAD_DIM)
                s = _dot_nt(q_ref[0, :, lanes], k_ref[0, pl.ds(start, width), lanes]) * (scale * LOG2_E)
                if mode == "diag":
                    s = jnp.where(row >= col, s + bias_ref[hd, 0], NEG_BIG)
                elif mode == "near+diag":
                    s = jnp.concatenate(
                        [s[:, :bq] + bias_ref[hd, 1],
                         jnp.where(row >= col, s[:, bq:] + bias_ref[hd, 0], NEG_BIG)], axis=1)
                s_ref[u, :, pl.ds(start, width)] = s
                mx = mx_ref[u]
                for c in range(width // LANES):
                    mx = jnp.maximum(mx, s[:, c * LANES:(c + 1) * LANES])
                if mode == "far":
                    mx_ref[u] = mx
                else:
                    mx_ref[u] = jnp.broadcast_to(jnp.max(mx, axis=-1, keepdims=True), (bq, LANES))
                    ls_ref[u] = jnp.zeros((bq, LANES), F32)
                    acc_ref[u] = jnp.zeros((bq, dv), F32)

    def values(start, width):
        for hd in range(n_heads):
            vv = v_ref[0, pl.ds(start, width), hd * dv:(hd + 1) * dv]
            for mp in range(2):
                u = 2 * hd + mp
                s = s_ref[u, :, pl.ds(start, width)]
                row_max = mx_ref[u]
                ls = ls_ref[u]
                ps = []
                for c in range(width // LANES):
                    p = jnp.exp2(s[:, c * LANES:(c + 1) * LANES] - row_max)
                    ls = ls + p
                    ps.append(p.astype(BF16))
                ls_ref[u] = ls
                acc_ref[u] += _dot(jnp.concatenate(ps, axis=1), vv)

    def looped(n_blocks, fn):
        def body(t, _):
            fn(pl.multiple_of(t * 8 * bq, 8 * bq), 8 * bq)
            return 0

        lax.fori_loop(0, lax.shift_right_logical(n_blocks, 3), body, 0)
        for span in (4, 2, 1):
            @pl.when((n_blocks & span) == span)
            def _(span=span):
                done = n_blocks & ~(2 * span - 1)
                fn(pl.multiple_of(done * bq, span * bq), span * bq)

    looped(jnp.maximum(i - 1, 0), functools.partial(scores, mode="far"))

    @pl.when(i >= 1)
    def _():
        scores(pl.multiple_of((i - 1) * bq, bq), 2 * bq, "near+diag")

    @pl.when(i == 0)
    def _():
        scores(0, bq, "diag")

    looped(i + 1, values)

    lam = (jnp.exp(jnp.sum(lq1_ref[...] * lk1_ref[...], axis=-1, keepdims=True))
           - jnp.exp(jnp.sum(lq2_ref[...] * lk2_ref[...], axis=-1, keepdims=True)) + lam_init)
    for hd in range(n_heads):
        l0 = jnp.sum(ls_ref[2 * hd], axis=-1, keepdims=True)
        l1 = jnp.sum(ls_ref[2 * hd + 1], axis=-1, keepdims=True)
        out = acc_ref[2 * hd] / l0 - lam * (acc_ref[2 * hd + 1] / l1)
        y = out * lax.rsqrt(jnp.mean(out * out, axis=-1, keepdims=True) + RMS_EPS) * g_ref[...]
        o_ref[0, :, hd * dv:(hd + 1) * dv] = (y * (1.0 - lam_init)).astype(o_ref.dtype)


def _df_call(proj, lam_vecs, bias_tiles, g, n_sb, n_df, bq, n_heads, lam_init):
    b, s, _ = proj.shape
    dv = 2 * HEAD_DIM
    width = n_heads * dv
    groups = n_df // n_heads
    q_start = 3 * n_sb * HEAD_DIM
    assert q_start % width == 0 and n_df % n_heads == 0
    q_off = q_start // width
    k_off = q_off + groups
    v_off = k_off + groups
    blk = (2 * bq * width + 2 * s * width) * 2 + n_heads * 2 * bq * bq * 4
    kern = functools.partial(_df_kernel, bq=bq, n_heads=n_heads, scale=1.0 / math.sqrt(HEAD_DIM),
                             lam_init=lam_init)
    vec_spec = pl.BlockSpec((1, HEAD_DIM), lambda bi, h, i: (0, 0))
    n_soft = 2 * n_heads
    return pl.pallas_call(
        kern,
        grid=(b, groups, s // bq),
        in_specs=[vec_spec, vec_spec, vec_spec, vec_spec,
                  pl.BlockSpec((1, bq, width), lambda bi, h, i: (bi, i, q_off + h)),
                  pl.BlockSpec((1, s, width), lambda bi, h, i: (bi, 0, k_off + h)),
                  pl.BlockSpec((1, s, width), lambda bi, h, i: (bi, 0, v_off + h)),
                  pl.BlockSpec((n_heads, 2, bq, bq), lambda bi, h, i: (h, 0, 0, 0)),
                  pl.BlockSpec((1, dv), lambda bi, h, i: (0, 0))],
        out_specs=pl.BlockSpec((1, bq, width), lambda bi, h, i: (bi, i, h)),
        out_shape=jax.ShapeDtypeStruct((b, s, n_df * dv), BF16),
        scratch_shapes=[pltpu.VMEM((n_soft, bq, s), F32), pltpu.VMEM((n_soft, bq, LANES), F32),
                        pltpu.VMEM((n_soft, bq, LANES), F32), pltpu.VMEM((n_soft, bq, dv), F32)],
        compiler_params=_params(("parallel", "parallel", "arbitrary"),
                                _vmem_limit(blk, n_soft * bq * (s + dv + 2 * LANES) * 4,
                                            16 * n_heads * bq * bq * 4)),
        name="df_attn",
    )(*[v.reshape(1, HEAD_DIM) for v in lam_vecs], proj, proj, proj, bias_tiles,
      g.reshape(1, dv))


def _outproj_kernel(sb_ref, df_ref, w_ref, h_ref, g_ref, b_ref, o32_ref, o16_ref, *, alpha):
    tm, half = sb_ref.shape
    sub = tm // 4 if tm % 32 == 0 else tm
    for r in range(tm // sub):
        rows = slice(r * sub, (r + 1) * sub)
        mix = _dot(sb_ref[rows, :], w_ref[:half, :]) + _dot(df_ref[rows, :], w_ref[half:, :])
        y = _layernorm(alpha * h_ref[rows, :] + mix, g_ref[...], b_ref[...])
        o32_ref[rows, :] = y
        o16_ref[rows, :] = y.astype(BF16)


def _outproj_call(sb_o, df_o, w, h, g, b, tm, alpha):
    m, d = h.shape
    half = sb_o.shape[1]
    blk = tm * half * 2 * 2 + tm * d * (4 + 4 + 2) + w.size * 2
    row = lambda i: (i, 0)
    fixed = lambda i: (0, 0)
    return pl.pallas_call(
        functools.partial(_outproj_kernel, alpha=alpha),
        grid=(m // tm,),
        in_specs=[pl.BlockSpec((tm, half), row), pl.BlockSpec((tm, half), row),
                  pl.BlockSpec(w.shape, fixed),
                  pl.BlockSpec((tm, d), row),
                  pl.BlockSpec((1, d), fixed), pl.BlockSpec((1, d), fixed)],
        out_specs=[pl.BlockSpec((tm, d), row), pl.BlockSpec((tm, d), row)],
        out_shape=[jax.ShapeDtypeStruct((m, d), F32), jax.ShapeDtypeStruct((m, d), BF16)],
        compiler_params=_params(("parallel",), _vmem_limit(blk, 0, 4 * tm * d * 4)),
        name="out_proj_ln",
    )(sb_o, df_o, w, h, g.reshape(1, d), b.reshape(1, d))


def _mlp_kernel(x16_ref, x32_ref, wu_ref, wd_ref, g_ref, b_ref, o32_ref, o16_ref, acc0_ref, acc1_ref,
                *, alpha, n_tiles, n_f):
    i = pl.program_id(0)
    f = pl.program_id(1)
    accs = (acc0_ref, acc1_ref)
    rows_per_step = o32_ref.shape[0] // n_f

    @pl.when(jnp.logical_and(i == 0, f == 0))
    def _():
        acc0_ref[...] = jnp.zeros_like(acc0_ref)
        acc1_ref[...] = jnp.zeros_like(acc1_ref)

    def finish_slice(prev_ref):
        rows = pl.ds(pl.multiple_of(f * rows_per_step, rows_per_step), rows_per_step)
        y = _layernorm(alpha * x32_ref[rows, :] + prev_ref[rows, :], g_ref[...], b_ref[...])
        o32_ref[rows, :] = y
        o16_ref[rows, :] = y.astype(BF16)
        prev_ref[rows, :] = jnp.zeros((rows_per_step, prev_ref.shape[1]), F32)

    for parity in range(2):
        cur_ref, prev_ref = accs[parity], accs[1 - parity]
        mine = jnp.logical_and(i < n_tiles, lax.rem(i, 2) == parity)

        @pl.when(mine)
        def _(cur_ref=cur_ref, prev_ref=prev_ref):
            u = _dot(x16_ref[...], wu_ref[...])
            a = jnp.square(jnp.maximum(u, 0.0)).astype(BF16)
            cur_ref[...] += _dot(a, wd_ref[...])
            finish_slice(prev_ref)

    @pl.when(i == n_tiles)
    def _():
        finish_slice(accs[(n_tiles - 1) % 2])


def _mlp_call(x16, x32, wu, wd, g, b, tm, tf, alpha):
    m, d = x32.shape
    ff = wu.shape[1]
    n_tiles, n_f = m // tm, ff // tf
    assert tm % n_f == 0 and (tm // n_f) % BF16_SUBLANES == 0, (tm, n_f)
    blk = tm * d * (2 + 4 + 4 + 2) + 2 * d * tf * 2
    lagged = lambda i, f: (jnp.maximum(i - 1, 0), 0)
    fixed = lambda i, f: (0, 0)
    f_idx = lambda i, f: jnp.where(i < n_tiles, f, n_f - 1)
    return pl.pallas_call(
        functools.partial(_mlp_kernel, alpha=alpha, n_tiles=n_tiles, n_f=n_f),
        grid=(n_tiles + 1, n_f),
        in_specs=[pl.BlockSpec((tm, d), lambda i, f: (jnp.minimum(i, n_tiles - 1), 0)),
                  pl.BlockSpec((tm, d), lagged),
                  pl.BlockSpec((d, tf), lambda i, f: (0, f_idx(i, f))),
                  pl.BlockSpec((tf, d), lambda i, f: (f_idx(i, f), 0)),
                  pl.BlockSpec((1, d), fixed), pl.BlockSpec((1, d), fixed)],
        out_specs=[pl.BlockSpec((tm, d), lagged), pl.BlockSpec((tm, d), lagged)],
        out_shape=[jax.ShapeDtypeStruct((m, d), F32), jax.ShapeDtypeStruct((m, d), BF16)],
        scratch_shapes=[pltpu.VMEM((tm, d), F32), pltpu.VMEM((tm, d), F32)],
        compiler_params=_params(("arbitrary", "arbitrary"),
                                _vmem_limit(blk, 2 * tm * d * 4, 2 * tm * tf * 4 + 2 * tm * d * 4)),
        name="mlp_ln",
    )(x16, x32, wu, wd, g.reshape(1, d), b.reshape(1, d))


def _tile(n, want):
    t = min(n, want)
    while n % t or t % LANES:
        t -= LANES
        assert t > 0, (n, want)
    return t


def kernel(x, ln0_g, ln0_b, w_in, w_out, sb_norm_g, lam_q1, lam_k1, lam_q2, lam_k2, diff_norm_g,
           rel_bias, ln1_g, ln1_b, w_up, w_down, ln2_g, ln2_b):
    b, s, d = x.shape
    depth = w_in.shape[0]
    n_sb = d // (2 * HEAD_DIM)
    n_df = d // (4 * HEAD_DIM)
    assert w_in.shape[2] == 3 * n_sb * HEAD_DIM + 3 * n_df * 2 * HEAD_DIM
    assert rel_bias.shape == (N_BUCKETS, n_df)
    alpha = (2 * depth) ** 0.25
    m = b * s

    tm_ln = _tile(m, LN_ROWS)
    tm_proj, tn_proj = _tile(m, PROJ_ROWS), _tile(w_in.shape[2], PROJ_COLS)
    tm_out = _tile(m, OUT_ROWS)
    tm_mlp, tf_mlp = _tile(m, MLP_ROWS), _tile(w_up.shape[2], MLP_HIDDEN)
    bq_sb = _tile(s, SB_QUERY_ROWS)
    sb_heads = math.gcd(n_sb, SB_HEADS_PER_STEP)
    bq_df = _tile(s, DF_QUERY_ROWS)
    df_heads = math.gcd(n_df, DF_HEADS_PER_STEP)
    assert bq_df >= MAX_DISTANCE

    uu = _cumsum_matrix()
    bias_tiles = _bias_call(rel_bias, bq_df)

    h32, h16 = _ln_call(x.reshape(m, d), ln0_g, ln0_b, tm_ln)
    for l in range(depth):
        lam_init = 0.8 - 0.6 * math.exp(-0.3 * l)
        proj, (w_out16, w_up16, w_down16) = _matmul_call(h16, w_in, l, tm_proj, tn_proj,
                                                         (w_out, w_up, w_down))
        proj = proj.reshape(b, s, -1)
        sb_o = _sb_call(proj, uu, sb_norm_g[l], n_sb, bq_sb, sb_heads)
        df_o = _df_call(proj, (lam_q1[l], lam_k1[l], lam_q2[l], lam_k2[l]), bias_tiles,
                        diff_norm_g[l], n_sb, n_df, bq_df, df_heads, lam_init)
        h32, h16 = _outproj_call(sb_o.reshape(m, -1), df_o.reshape(m, -1), w_out16, h32,
                                 ln1_g[l], ln1_b[l], tm_out, alpha)
        h32, h16 = _mlp_call(h16, h32, w_up16, w_down16, ln2_g[l], ln2_b[l],
                             tm_mlp, tf_mlp, alpha)
    return h32.reshape(b, s, d)
```

```python
import functools
import math

import jax
import jax.numpy as jnp
from jax import lax
from jax.experimental import pallas as pl
from jax.experimental.pallas import tpu as pltpu

HEAD_DIM = 128
N_BUCKETS = 32
MAX_DISTANCE = 128
LN_EPS = 1e-5
RMS_EPS = 1e-5
NEG_BIG = -1e30
F32_EXP2_ZERO_BELOW = -150.5
LOG2_E = 1.4426950408889634

LANES = 128
BF16_SUBLANES = 16
V7X_VMEM_BYTES = 64 * 1024 * 1024
VMEM_LIMIT_FLOOR = 16 * 1024 * 1024
VMEM_COMPILER_RESERVE = 6 * 1024 * 1024

LN_ROWS = 512
PROJ_ROWS, PROJ_COLS = 1024, 1536
OUT_ROWS = 512
MLP_ROWS, MLP_HIDDEN = 512, 1024
SB_QUERY_ROWS, SB_HEADS_PER_STEP = 256, 8
DF_QUERY_ROWS, DF_HEADS_PER_STEP = 256, 2

F32 = jnp.float32
BF16 = jnp.bfloat16


def _vmem_limit(block_bytes, scratch_bytes, temp_bytes):
    need = 2 * block_bytes + scratch_bytes + temp_bytes
    return int(min(max(need, VMEM_LIMIT_FLOOR), V7X_VMEM_BYTES - VMEM_COMPILER_RESERVE))


def _params(semantics, vmem_bytes):
    return pltpu.CompilerParams(dimension_semantics=semantics, vmem_limit_bytes=vmem_bytes)


def _layernorm(v, g, b):
    mu = jnp.mean(v, axis=-1, keepdims=True)
    c = v - mu
    var = jnp.mean(c * c, axis=-1, keepdims=True)
    return c * lax.rsqrt(var + LN_EPS) * g + b


def _dot(a, b):
    return jnp.dot(a, b, preferred_element_type=F32)


def _dot_nt(a, b):
    return lax.dot_general(a, b, (((1,), (1,)), ((), ())), preferred_element_type=F32)


def _ln_kernel(x_ref, g_ref, b_ref, o32_ref, o16_ref):
    y = _layernorm(x_ref[...], g_ref[...], b_ref[...])
    o32_ref[...] = y
    o16_ref[...] = y.astype(BF16)


def _ln_call(x, g, b, tm):
    m, d = x.shape
    blk = tm * d * (4 + 4 + 2)
    return pl.pallas_call(
        _ln_kernel,
        grid=(m // tm,),
        in_specs=[pl.BlockSpec((tm, d), lambda i: (i, 0)),
                  pl.BlockSpec((1, d), lambda i: (0, 0)),
                  pl.BlockSpec((1, d), lambda i: (0, 0))],
        out_specs=[pl.BlockSpec((tm, d), lambda i: (i, 0)),
                   pl.BlockSpec((tm, d), lambda i: (i, 0))],
        out_shape=[jax.ShapeDtypeStruct((m, d), F32), jax.ShapeDtypeStruct((m, d), BF16)],
        compiler_params=_params(("parallel",), _vmem_limit(blk, 0, 4 * tm * d * 4)),
        name="ln_in",
    )(x, g.reshape(1, d), b.reshape(1, d))


def _matmul_kernel(*refs, n_side):
    x_ref, w_ref = refs[:2]
    side_in = refs[2:2 + n_side]
    o_ref = refs[2 + n_side]
    side_out = refs[3 + n_side:3 + 2 * n_side]
    w16_ref = refs[3 + 2 * n_side]

    @pl.when(pl.program_id(1) == 0)
    def _():
        w16_ref[...] = w_ref[...].astype(BF16)

    o_ref[...] = _dot(x_ref[...], w16_ref[...]).astype(o_ref.dtype)
    for src_ref, dst_ref in zip(side_in, side_out):
        dst_ref[...] = src_ref[...].astype(BF16)


def _slab_count(rows, steps):
    units = rows // BF16_SUBLANES
    return max(c for c in range(1, min(units, steps) + 1) if units % c == 0)


def _matmul_call(x, w, layer, tm, tn, side):
    m, k = x.shape
    n = w.shape[2]
    n_j, n_i = n // tn, m // tm
    blk = (tm * k + tm * tn) * 2 + k * tn * 4
    side_in_specs, side_out_specs, side_shapes = [], [], []
    for arr in side:
        rows, cols = arr.shape[1:]
        slabs = _slab_count(rows, n_j * n_i)
        slab = lambda j, i, slabs=slabs: jnp.minimum(j * n_i + i, slabs - 1)
        side_in_specs.append(pl.BlockSpec((None, rows // slabs, cols),
                                          lambda j, i, slab=slab: (layer, slab(j, i), 0)))
        side_out_specs.append(pl.BlockSpec((rows // slabs, cols), lambda j, i, slab=slab: (slab(j, i), 0)))
        side_shapes.append(jax.ShapeDtypeStruct((rows, cols), BF16))
        blk += (rows // slabs) * cols * 6
    outs = pl.pallas_call(
        functools.partial(_matmul_kernel, n_side=len(side)),
        grid=(n_j, n_i),
        in_specs=[pl.BlockSpec((tm, k), lambda j, i: (i, 0)),
                  pl.BlockSpec((None, k, tn), lambda j, i: (layer, 0, j))] + side_in_specs,
        out_specs=[pl.BlockSpec((tm, tn), lambda j, i: (i, j))] + side_out_specs,
        out_shape=[jax.ShapeDtypeStruct((m, n), BF16)] + side_shapes,
        scratch_shapes=[pltpu.VMEM((k, tn), BF16)],
        compiler_params=_params(("arbitrary", "arbitrary"),
                                _vmem_limit(blk, k * tn * 2, 2 * tm * tn * 4)),
        name="in_proj",
    )(x, w, *side)
    return outs[0], outs[1:]


def _sb_kernel(q_ref, k_ref, v_ref, ll_ref, g_ref, o_ref, acc_ref, carry_ref, *, bq, n_heads, scale):
    i = pl.program_id(2)
    ll = ll_ref[...]
    causal_t = (lax.broadcasted_iota(jnp.int32, (bq, bq), 0)
                < lax.broadcasted_iota(jnp.int32, (bq, bq), 1))

    def mask_diag(t):
        lead = t.shape[0] - bq
        tail = jnp.where(causal_t, t[lead:], 0.0)
        return tail if lead == 0 else jnp.concatenate([t[:lead], tail], axis=0)

    def block(start, width, diag):
        for hd in range(n_heads):
            lanes = slice(hd * HEAD_DIM, (hd + 1) * HEAD_DIM)
            q = q_ref[0, :, lanes]
            ks = k_ref[0, pl.ds(start, width), lanes]
            vs = v_ref[0, pl.ds(start, width), lanes]
            z = _dot_nt(ks, q) * (scale * LOG2_E)
            neg_abs = lax.bitcast_convert_type(
                lax.bitcast_convert_type(z, jnp.uint32) | jnp.uint32(0x80000000), F32)
            sp = jnp.maximum(z, 0.0) + jnp.log2(1.0 + jnp.exp2(neg_abs))
            log2_beta = z - sp
            if diag:
                sp = mask_diag(sp)
            carry = jnp.zeros((1, bq), F32) if diag else carry_ref[hd, 0:1, :]
            n_chunks = width // LANES
            between = [None] * n_chunks
            for c in reversed(range(n_chunks)):
                sp_c = sp[c * LANES:(c + 1) * LANES]
                hi = sp_c.astype(BF16)
                lo = (sp_c - hi.astype(F32)).astype(BF16)
                between[c] = _dot(ll, jnp.concatenate([hi, lo], axis=0)) + carry
                carry = carry + jnp.sum(sp_c, axis=0, keepdims=True)
            w = jnp.exp2(log2_beta - jnp.concatenate(between, axis=0))
            if diag:
                w = mask_diag(w)
            pv = lax.dot_general(vs, w.astype(BF16), (((0,), (0,)), ((), ())), preferred_element_type=F32)
            acc_ref[hd] = pv if diag else acc_ref[hd] + pv
            carry_ref[hd] = jnp.broadcast_to(carry, (carry_ref.shape[1], bq))

    def carry_min():
        m = carry_ref[0]
        for hd in range(1, n_heads):
            m = jnp.minimum(m, carry_ref[hd])
        return jnp.min(m)

    @pl.when(i == 0)
    def _():
        block(0, bq, True)

    @pl.when(i >= 1)
    def _():
        block(pl.multiple_of((i - 1) * bq, bq), 2 * bq, True)

    def live(state):
        j, lowest = state
        return jnp.logical_and(j >= 0, lowest < -F32_EXP2_ZERO_BELOW)

    def step(state):
        j, _ = state
        block(pl.multiple_of(j * bq, bq), bq, False)
        return j - 1, carry_min()

    lax.while_loop(live, step, (i - 2, carry_min()))

    for hd in range(n_heads):
        out_t = acc_ref[hd]
        y_t = out_t * lax.rsqrt(jnp.mean(out_t * out_t, axis=0, keepdims=True) + RMS_EPS) * g_ref[...]
        o_ref[0, :, hd * HEAD_DIM:(hd + 1) * HEAD_DIM] = y_t.T.astype(o_ref.dtype)


def _sb_call(proj, ll, g, n_sb, bq, n_heads):
    b, s, _ = proj.shape
    width = n_heads * HEAD_DIM
    groups = n_sb // n_heads
    blk = (2 * bq * width + 2 * s * width) * 2 + ll.size * 2 + HEAD_DIM * bq * 4
    kern = functools.partial(_sb_kernel, bq=bq, n_heads=n_heads, scale=1.0 / math.sqrt(HEAD_DIM))
    g_cols = jnp.broadcast_to(g.astype(F32)[:, None], (HEAD_DIM, bq))
    return pl.pallas_call(
        kern,
        grid=(b, groups, s // bq),
        in_specs=[pl.BlockSpec((1, bq, width), lambda bi, h, i: (bi, i, h)),
                  pl.BlockSpec((1, s, width), lambda bi, h, i: (bi, 0, groups + h)),
                  pl.BlockSpec((1, s, width), lambda bi, h, i: (bi, 0, 2 * groups + h)),
                  pl.BlockSpec(ll.shape, lambda bi, h, i: (0, 0)),
                  pl.BlockSpec((HEAD_DIM, bq), lambda bi, h, i: (0, 0))],
        out_specs=pl.BlockSpec((1, bq, width), lambda bi, h, i: (bi, i, h)),
        out_shape=jax.ShapeDtypeStruct((b, s, n_sb * HEAD_DIM), BF16),
        scratch_shapes=[pltpu.VMEM((n_heads, HEAD_DIM, bq), F32), pltpu.VMEM((n_heads, 8, bq), F32)],
        compiler_params=_params(("parallel", "parallel", "arbitrary"),
                                _vmem_limit(blk, n_heads * (HEAD_DIM + 8) * bq * 4, 16 * n_heads * bq * bq * 4)),
        name="sb_attn",
    )(proj, proj, proj, ll, g_cols)


def _cumsum_matrix():
    s = jnp.arange(LANES)[:, None]
    j = jnp.arange(LANES)[None, :]
    later = (j > s).astype(BF16)
    return jnp.concatenate([later, later], axis=1)


def _bias_kernel(rb_ref, o_ref, *, bq):
    h = pl.program_id(0)
    t = lax.broadcasted_iota(jnp.int32, (bq, bq), 0)
    s = lax.broadcasted_iota(jnp.int32, (bq, bq), 1)
    max_exact = N_BUCKETS // 2
    far = rb_ref[N_BUCKETS - 1, h]
    for which in range(2):
        dist = t - s + which * bq
        n = jnp.maximum(dist, 0)
        nf = jnp.maximum(n, 1).astype(F32)
        large = max_exact + (jnp.log(nf / max_exact) / math.log(MAX_DISTANCE / max_exact)
                             * (N_BUCKETS - max_exact)).astype(jnp.int32)
        large = jnp.minimum(large, N_BUCKETS - 1)
        bucket = jnp.where(n < max_exact, n, large)
        bias = jnp.zeros((bq, bq), F32)
        for bkt in range(N_BUCKETS):
            bias = jnp.where(bucket == bkt, (rb_ref[bkt, h] - far) * LOG2_E, bias)
        o_ref[0, which] = bias


def _bias_call(rel_bias, bq):
    n_df = rel_bias.shape[1]
    return pl.pallas_call(
        functools.partial(_bias_kernel, bq=bq),
        grid=(n_df,),
        in_specs=[pl.BlockSpec(memory_space=pltpu.SMEM)],
        out_specs=pl.BlockSpec((1, 2, bq, bq), lambda h: (h, 0, 0, 0)),
        out_shape=jax.ShapeDtypeStruct((n_df, 2, bq, bq), F32),
        compiler_params=_params(("arbitrary",), _vmem_limit(2 * bq * bq * 4, 0, 8 * bq * bq * 4)),
        name="t5_bias",
    )(rel_bias)


def _df_kernel(lq1_ref, lk1_ref, lq2_ref, lk2_ref, q_ref, k_ref, v_ref, bias_ref, g_ref,
               o_ref, s_ref, mx_ref, ls_ref, acc_ref, *, bq, n_heads, scale, lam_init):
    i = pl.program_id(2)
    dv = 2 * HEAD_DIM
    mx_ref[...] = jnp.full_like(mx_ref, -jnp.inf)
    row = lax.broadcasted_iota(jnp.int32, (bq, bq), 0)
    col = lax.broadcasted_iota(jnp.int32, (bq, bq), 1)

    def scores(start, width, mode):
        for hd in range(n_heads):
            for mp in range(2):
                u = 2 * hd + mp
                lanes = slice(u * HEAD_DIM, (u + 1) * HEAD_DIM)
                s = _dot_nt(q_ref[0, :, lanes], k_ref[0, pl.ds(start, width), lanes]) * (scale * LOG2_E)
                if mode == "diag":
                    s = jnp.where(row >= col, s + bias_ref[hd, 0], NEG_BIG)
                elif mode == "near+diag":
                    s = jnp.concatenate(
                        [s[:, :bq] + bias_ref[hd, 1],
                         jnp.where(row >= col, s[:, bq:] + bias_ref[hd, 0], NEG_BIG)], axis=1)
                s_ref[u, :, pl.ds(start, width)] = s
                mx = mx_ref[u]
                for c in range(width // LANES):
                    mx = jnp.maximum(mx, s[:, c * LANES:(c + 1) * LANES])
                if mode == "far":
                    mx_ref[u] = mx
                else:
                    mx_ref[u] = jnp.broadcast_to(jnp.max(mx, axis=-1, keepdims=True), (bq, LANES))
                    ls_ref[u] = jnp.zeros((bq, LANES), F32)
                    acc_ref[u] = jnp.zeros((bq, dv), F32)

    def values(start, width):
        for hd in range(n_heads):
            vv = v_ref[0, pl.ds(start, width), hd * dv:(hd + 1) * dv]
            for mp in range(2):
                u = 2 * hd + mp
                s = s_ref[u, :, pl.ds(start, width)]
                row_max = mx_ref[u]
                ls = ls_ref[u]
                ps = []
                for c in range(width // LANES):
                    p = jnp.exp2(s[:, c * LANES:(c + 1) * LANES] - row_max)
                    ls = ls + p
                    ps.append(p.astype(BF16))
                ls_ref[u] = ls
                acc_ref[u] += _dot(jnp.concatenate(ps, axis=1), vv)

    def looped(n_blocks, fn):
        def body(t, _):
            fn(pl.multiple_of(t * 8 * bq, 8 * bq), 8 * bq)
            return 0

        lax.fori_loop(0, lax.shift_right_logical(n_blocks, 3), body, 0)
        for span in (4, 2, 1):
            @pl.when((n_blocks & span) == span)
            def _(span=span):
                done = n_blocks & ~(2 * span - 1)
                fn(pl.multiple_of(done * bq, span * bq), span * bq)

    looped(jnp.maximum(i - 1, 0), functools.partial(scores, mode="far"))

    @pl.when(i >= 1)
    def _():
        scores(pl.multiple_of((i - 1) * bq, bq), 2 * bq, "near+diag")

    @pl.when(i == 0)
    def _():
        scores(0, bq, "diag")

    looped(i + 1, values)

    lam = (jnp.exp(jnp.sum(lq1_ref[...] * lk1_ref[...], axis=-1, keepdims=True))
           - jnp.exp(jnp.sum(lq2_ref[...] * lk2_ref[...], axis=-1, keepdims=True)) + lam_init)
    for hd in range(n_heads):
        l0 = jnp.sum(ls_ref[2 * hd], axis=-1, keepdims=True)
        l1 = jnp.sum(ls_ref[2 * hd + 1], axis=-1, keepdims=True)
        out = acc_ref[2 * hd] / l0 - lam * (acc_ref[2 * hd + 1] / l1)
        y = out * lax.rsqrt(jnp.mean(out * out, axis=-1, keepdims=True) + RMS_EPS) * g_ref[...]
        o_ref[0, :, hd * dv:(hd + 1) * dv] = (y * (1.0 - lam_init)).astype(o_ref.dtype)


def _df_call(proj, lam_vecs, bias_tiles, g, n_sb, n_df, bq, n_heads, lam_init):
    b, s, _ = proj.shape
    dv = 2 * HEAD_DIM
    width = n_heads * dv
    groups = n_df // n_heads
    q_start = 3 * n_sb * HEAD_DIM
    assert q_start % width == 0 and n_df % n_heads == 0
    q_off = q_start // width
    k_off = q_off + groups
    v_off = k_off + groups
    blk = (2 * bq * width + 2 * s * width) * 2 + n_heads * 2 * bq * bq * 4
    kern = functools.partial(_df_kernel, bq=bq, n_heads=n_heads, scale=1.0 / math.sqrt(HEAD_DIM),
                             lam_init=lam_init)
    vec_spec = pl.BlockSpec((1, HEAD_DIM), lambda bi, h, i: (0, 0))
    n_soft = 2 * n_heads
    return pl.pallas_call(
        kern,
        grid=(b, groups, s // bq),
        in_specs=[vec_spec, vec_spec, vec_spec, vec_spec,
                  pl.BlockSpec((1, bq, width), lambda bi, h, i: (bi, i, q_off + h)),
                  pl.BlockSpec((1, s, width), lambda bi, h, i: (bi, 0, k_off + h)),
                  pl.BlockSpec((1, s, width), lambda bi, h, i: (bi, 0, v_off + h)),
                  pl.BlockSpec((n_heads, 2, bq, bq), lambda bi, h, i: (h, 0, 0, 0)),
                  pl.BlockSpec((1, dv), lambda bi, h, i: (0, 0))],
        out_specs=pl.BlockSpec((1, bq, width), lambda bi, h, i: (bi, i, h)),
        out_shape=jax.ShapeDtypeStruct((b, s, n_df * dv), BF16),
        scratch_shapes=[pltpu.VMEM((n_soft, bq, s), F32), pltpu.VMEM((n_soft, bq, LANES), F32),
                        pltpu.VMEM((n_soft, bq, LANES), F32), pltpu.VMEM((n_soft, bq, dv), F32)],
        compiler_params=_params(("parallel", "parallel", "arbitrary"),
                                _vmem_limit(blk, n_soft * bq * (s + dv + 2 * LANES) * 4,
                                            16 * n_heads * bq * bq * 4)),
        name="df_attn",
    )(*[v.reshape(1, HEAD_DIM) for v in lam_vecs], proj, proj, proj, bias_tiles,
      g.reshape(1, dv))


def _outproj_kernel(sb_ref, df_ref, w_ref, h_ref, g_ref, b_ref, o32_ref, o16_ref, *, alpha):
    tm, half = sb_ref.shape
    sub = tm // 4 if tm % 32 == 0 else tm
    for r in range(tm // sub):
        rows = slice(r * sub, (r + 1) * sub)
        mix = _dot(sb_ref[rows, :], w_ref[:half, :]) + _dot(df_ref[rows, :], w_ref[half:, :])
        y = _layernorm(alpha * h_ref[rows, :] + mix, g_ref[...], b_ref[...])
        o32_ref[rows, :] = y
        o16_ref[rows, :] = y.astype(BF16)


def _outproj_call(sb_o, df_o, w, h, g, b, tm, alpha):
    m, d = h.shape
    half = sb_o.shape[1]
    blk = tm * half * 2 * 2 + tm * d * (4 + 4 + 2) + w.size * 2
    row = lambda i: (i, 0)
    fixed = lambda i: (0, 0)
    return pl.pallas_call(
        functools.partial(_outproj_kernel, alpha=alpha),
        grid=(m // tm,),
        in_specs=[pl.BlockSpec((tm, half), row), pl.BlockSpec((tm, half), row),
                  pl.BlockSpec(w.shape, fixed),
                  pl.BlockSpec((tm, d), row),
                  pl.BlockSpec((1, d), fixed), pl.BlockSpec((1, d), fixed)],
        out_specs=[pl.BlockSpec((tm, d), row), pl.BlockSpec((tm, d), row)],
        out_shape=[jax.ShapeDtypeStruct((m, d), F32), jax.ShapeDtypeStruct((m, d), BF16)],
        compiler_params=_params(("parallel",), _vmem_limit(blk, 0, 4 * tm * d * 4)),
        name="out_proj_ln",
    )(sb_o, df_o, w, h, g.reshape(1, d), b.reshape(1, d))


def _mlp_kernel(x16_ref, x32_ref, wu_ref, wd_ref, g_ref, b_ref, o32_ref, o16_ref, acc0_ref, acc1_ref,
                *, alpha, n_tiles, n_f):
    i = pl.program_id(0)
    f = pl.program_id(1)
    accs = (acc0_ref, acc1_ref)
    rows_per_step = o32_ref.shape[0] // n_f

    @pl.when(jnp.logical_and(i == 0, f == 0))
    def _():
        acc0_ref[...] = jnp.zeros_like(acc0_ref)
        acc1_ref[...] = jnp.zeros_like(acc1_ref)

    def finish_slice(prev_ref):
        rows = pl.ds(pl.multiple_of(f * rows_per_step, rows_per_step), rows_per_step)
        y = _layernorm(alpha * x32_ref[rows, :] + prev_ref[rows, :], g_ref[...], b_ref[...])
        o32_ref[rows, :] = y
        o16_ref[rows, :] = y.astype(BF16)
        prev_ref[rows, :] = jnp.zeros((rows_per_step, prev_ref.shape[1]), F32)

    for parity in range(2):
        cur_ref, prev_ref = accs[parity], accs[1 - parity]
        mine = jnp.logical_and(i < n_tiles, lax.rem(i, 2) == parity)

        @pl.when(mine)
        def _(cur_ref=cur_ref, prev_ref=prev_ref):
            u = _dot(x16_ref[...], wu_ref[...])
            a = jnp.square(jnp.maximum(u, 0.0)).astype(BF16)
            cur_ref[...] += _dot(a, wd_ref[...])
            finish_slice(prev_ref)

    @pl.when(i == n_tiles)
    def _():
        finish_slice(accs[(n_tiles - 1) % 2])


def _mlp_call(x16, x32, wu, wd, g, b, tm, tf, alpha):
    m, d = x32.shape
    ff = wu.shape[1]
    n_tiles, n_f = m // tm, ff // tf
    assert tm % n_f == 0 and (tm // n_f) % BF16_SUBLANES == 0, (tm, n_f)
    blk = tm * d * (2 + 4 + 4 + 2) + 2 * d * tf * 2
    lagged = lambda i, f: (jnp.maximum(i - 1, 0), 0)
    fixed = lambda i, f: (0, 0)
    f_idx = lambda i, f: jnp.where(i < n_tiles, f, n_f - 1)
    return pl.pallas_call(
        functools.partial(_mlp_kernel, alpha=alpha, n_tiles=n_tiles, n_f=n_f),
        grid=(n_tiles + 1, n_f),
        in_specs=[pl.BlockSpec((tm, d), lambda i, f: (jnp.minimum(i, n_tiles - 1), 0)),
                  pl.BlockSpec((tm, d), lagged),
                  pl.BlockSpec((d, tf), lambda i, f: (0, f_idx(i, f))),
                  pl.BlockSpec((tf, d), lambda i, f: (f_idx(i, f), 0)),
                  pl.BlockSpec((1, d), fixed), pl.BlockSpec((1, d), fixed)],
        out_specs=[pl.BlockSpec((tm, d), lagged), pl.BlockSpec((tm, d), lagged)],
        out_shape=[jax.ShapeDtypeStruct((m, d), F32), jax.ShapeDtypeStruct((m, d), BF16)],
        scratch_shapes=[pltpu.VMEM((tm, d), F32), pltpu.VMEM((tm, d), F32)],
        compiler_params=_params(("arbitrary", "arbitrary"),
                                _vmem_limit(blk, 2 * tm * d * 4, 2 * tm * tf * 4 + 2 * tm * d * 4)),
        name="mlp_ln",
    )(x16, x32, wu, wd, g.reshape(1, d), b.reshape(1, d))


def _tile(n, want):
    t = min(n, want)
    while n % t or t % LANES:
        t -= LANES
        assert t > 0, (n, want)
    return t


def kernel(x, ln0_g, ln0_b, w_in, w_out, sb_norm_g, lam_q1, lam_k1, lam_q2, lam_k2, diff_norm_g,
           rel_bias, ln1_g, ln1_b, w_up, w_down, ln2_g, ln2_b):
    b, s, d = x.shape
    depth = w_in.shape[0]
    n_sb = d // (2 * HEAD_DIM)
    n_df = d // (4 * HEAD_DIM)
    assert w_in.shape[2] == 3 * n_sb * HEAD_DIM + 3 * n_df * 2 * HEAD_DIM
    assert rel_bias.shape == (N_BUCKETS, n_df)
    alpha = (2 * depth) ** 0.25
    m = b * s

    tm_ln = _tile(m, LN_ROWS)
    tm_proj, tn_proj = _tile(m, PROJ_ROWS), _tile(w_in.shape[2], PROJ_COLS)
    tm_out = _tile(m, OUT_ROWS)
    tm_mlp, tf_mlp = _tile(m, MLP_ROWS), _tile(w_up.shape[2], MLP_HIDDEN)
    bq_sb = _tile(s, SB_QUERY_ROWS)
    sb_heads = math.gcd(n_sb, SB_HEADS_PER_STEP)
    bq_df = _tile(s, DF_QUERY_ROWS)
    df_heads = math.gcd(n_df, DF_HEADS_PER_STEP)
    assert bq_df >= MAX_DISTANCE

    uu = _cumsum_matrix()
    bias_tiles = _bias_call(rel_bias, bq_df)

    h32, h16 = _ln_call(x.reshape(m, d), ln0_g, ln0_b, tm_ln)
    for l in range(depth):
        lam_init = 0.8 - 0.6 * math.exp(-0.3 * l)
        proj, (w_out16, w_up16, w_down16) = _matmul_call(h16, w_in, l, tm_proj, tn_proj,
                                                         (w_out, w_up, w_down))
        proj = proj.reshape(b, s, -1)
        sb_o = _sb_call(proj, uu, sb_norm_g[l], n_sb, bq_sb, sb_heads)
        df_o = _df_call(proj, (lam_q1[l], lam_k1[l], lam_q2[l], lam_k2[l]), bias_tiles,
                        diff_norm_g[l], n_sb, n_df, bq_df, df_heads, lam_init)
        h32, h16 = _outproj_call(sb_o.reshape(m, -1), df_o.reshape(m, -1), w_out16, h32,
                                 ln1_g[l], ln1_b[l], tm_out, alpha)
        h32, h16 = _mlp_call(h16, h32, w_up16, w_down16, ln2_g[l], ln2_b[l],
                             tm_mlp, tf_mlp, alpha)
    return h32.reshape(b, s, d)
```

```python
import functools
import math

import jax
import jax.numpy as jnp
from jax import lax
from jax.experimental import pallas as pl
from jax.experimental.pallas import tpu as pltpu

HEAD_DIM = 128
N_BUCKETS = 32
MAX_DISTANCE = 128
LN_EPS = 1e-5
RMS_EPS = 1e-5
NEG_BIG = -1e30
F32_EXP2_ZERO_BELOW = -150.5
LOG2_E = 1.4426950408889634

LANES = 128
BF16_SUBLANES = 16
V7X_VMEM_BYTES = 64 * 1024 * 1024
VMEM_LIMIT_FLOOR = 16 * 1024 * 1024
VMEM_COMPILER_RESERVE = 6 * 1024 * 1024

LN_ROWS = 1024
PROJ_ROWS, PROJ_COLS = 1024, 1536
OUT_ROWS = 512
MLP_ROWS, MLP_HIDDEN = 512, 1024
SB_QUERY_ROWS, SB_HEADS_PER_STEP = 256, 8
DF_QUERY_ROWS, DF_HEADS_PER_STEP = 256, 2

F32 = jnp.float32
BF16 = jnp.bfloat16


def _vmem_limit(block_bytes, scratch_bytes, temp_bytes):
    need = 2 * block_bytes + scratch_bytes + temp_bytes
    return int(min(max(need, VMEM_LIMIT_FLOOR), V7X_VMEM_BYTES - VMEM_COMPILER_RESERVE))


def _params(semantics, vmem_bytes):
    return pltpu.CompilerParams(dimension_semantics=semantics, vmem_limit_bytes=vmem_bytes)


def _layernorm(v, g, b):
    mu = jnp.mean(v, axis=-1, keepdims=True)
    c = v - mu
    var = jnp.mean(c * c, axis=-1, keepdims=True)
    return c * lax.rsqrt(var + LN_EPS) * g + b


def _dot(a, b):
    return jnp.dot(a, b, preferred_element_type=F32)


def _dot_nt(a, b):
    return lax.dot_general(a, b, (((1,), (1,)), ((), ())), preferred_element_type=F32)


def _ln_kernel(x_ref, g_ref, b_ref, o32_ref, o16_ref):
    y = _layernorm(x_ref[...], g_ref[...], b_ref[...])
    o32_ref[...] = y
    o16_ref[...] = y.astype(BF16)


def _ln_call(x, g, b, tm):
    m, d = x.shape
    blk = tm * d * (4 + 4 + 2)
    return pl.pallas_call(
        _ln_kernel,
        grid=(m // tm,),
        in_specs=[pl.BlockSpec((tm, d), lambda i: (i, 0)),
                  pl.BlockSpec((1, d), lambda i: (0, 0)),
                  pl.BlockSpec((1, d), lambda i: (0, 0))],
        out_specs=[pl.BlockSpec((tm, d), lambda i: (i, 0)),
                   pl.BlockSpec((tm, d), lambda i: (i, 0))],
        out_shape=[jax.ShapeDtypeStruct((m, d), F32), jax.ShapeDtypeStruct((m, d), BF16)],
        compiler_params=_params(("parallel",), _vmem_limit(blk, 0, 4 * tm * d * 4)),
        name="ln_in",
    )(x, g.reshape(1, d), b.reshape(1, d))


def _matmul_kernel(*refs, n_side):
    x_ref, w_ref = refs[:2]
    side_in = refs[2:2 + n_side]
    o_ref = refs[2 + n_side]
    side_out = refs[3 + n_side:3 + 2 * n_side]
    w16_ref = refs[3 + 2 * n_side]

    @pl.when(pl.program_id(1) == 0)
    def _():
        w16_ref[...] = w_ref[...].astype(BF16)

    o_ref[...] = _dot(x_ref[...], w16_ref[...]).astype(o_ref.dtype)
    for src_ref, dst_ref in zip(side_in, side_out):
        dst_ref[...] = src_ref[...].astype(BF16)


def _slab_count(rows, steps):
    units = rows // BF16_SUBLANES
    return max(c for c in range(1, min(units, steps) + 1) if units % c == 0)


def _matmul_call(x, w, layer, tm, tn, side):
    m, k = x.shape
    n = w.shape[2]
    n_j, n_i = n // tn, m // tm
    blk = (tm * k + tm * tn) * 2 + k * tn * 4
    side_in_specs, side_out_specs, side_shapes = [], [], []
    for arr in side:
        rows, cols = arr.shape[1:]
        slabs = _slab_count(rows, n_j * n_i)
        slab = lambda j, i, slabs=slabs: jnp.minimum(j * n_i + i, slabs - 1)
        side_in_specs.append(pl.BlockSpec((None, rows // slabs, cols),
                                          lambda j, i, slab=slab: (layer, slab(j, i), 0)))
        side_out_specs.append(pl.BlockSpec((rows // slabs, cols), lambda j, i, slab=slab: (slab(j, i), 0)))
        side_shapes.append(jax.ShapeDtypeStruct((rows, cols), BF16))
        blk += (rows // slabs) * cols * 6
    outs = pl.pallas_call(
        functools.partial(_matmul_kernel, n_side=len(side)),
        grid=(n_j, n_i),
        in_specs=[pl.BlockSpec((tm, k), lambda j, i: (i, 0)),
                  pl.BlockSpec((None, k, tn), lambda j, i: (layer, 0, j))] + side_in_specs,
        out_specs=[pl.BlockSpec((tm, tn), lambda j, i: (i, j))] + side_out_specs,
        out_shape=[jax.ShapeDtypeStruct((m, n), BF16)] + side_shapes,
        scratch_shapes=[pltpu.VMEM((k, tn), BF16)],
        compiler_params=_params(("arbitrary", "arbitrary"),
                                _vmem_limit(blk, k * tn * 2, 2 * tm * tn * 4)),
        name="in_proj",
    )(x, w, *side)
    return outs[0], outs[1:]


def _sb_kernel(q_ref, k_ref, v_ref, ll_ref, g_ref, o_ref, acc_ref, carry_ref, *, bq, n_heads, scale):
    i = pl.program_id(2)
    ll = ll_ref[...]
    causal_t = (lax.broadcasted_iota(jnp.int32, (bq, bq), 0)
                < lax.broadcasted_iota(jnp.int32, (bq, bq), 1))

    def mask_diag(t):
        lead = t.shape[0] - bq
        tail = jnp.where(causal_t, t[lead:], 0.0)
        return tail if lead == 0 else jnp.concatenate([t[:lead], tail], axis=0)

    def block(start, width, diag):
        for hd in range(n_heads):
            lanes = slice(hd * HEAD_DIM, (hd + 1) * HEAD_DIM)
            q = q_ref[0, :, lanes]
            ks = k_ref[0, pl.ds(start, width), lanes]
            vs = v_ref[0, pl.ds(start, width), lanes]
            z = _dot_nt(ks, q) * (scale * LOG2_E)
            neg_abs = lax.bitcast_convert_type(
                lax.bitcast_convert_type(z, jnp.uint32) | jnp.uint32(0x80000000), F32)
            sp = jnp.maximum(z, 0.0) + jnp.log2(1.0 + jnp.exp2(neg_abs))
            log2_beta = z - sp
            if diag:
                sp = mask_diag(sp)
            carry = jnp.zeros((1, bq), F32) if diag else carry_ref[hd, 0:1, :]
            n_chunks = width // LANES
            between = [None] * n_chunks
            for c in reversed(range(n_chunks)):
                sp_c = sp[c * LANES:(c + 1) * LANES]
                hi = sp_c.astype(BF16)
                lo = (sp_c - hi.astype(F32)).astype(BF16)
                between[c] = _dot(ll, jnp.concatenate([hi, lo], axis=0)) + carry
                carry = carry + jnp.sum(sp_c, axis=0, keepdims=True)
            w = jnp.exp2(log2_beta - jnp.concatenate(between, axis=0))
            if diag:
                w = mask_diag(w)
            pv = lax.dot_general(vs, w.astype(BF16), (((0,), (0,)), ((), ())), preferred_element_type=F32)
            acc_ref[hd] = pv if diag else acc_ref[hd] + pv
            carry_ref[hd] = jnp.broadcast_to(carry, (carry_ref.shape[1], bq))

    def carry_min():
        m = carry_ref[0]
        for hd in range(1, n_heads):
            m = jnp.minimum(m, carry_ref[hd])
        return jnp.min(m)

    @pl.when(i == 0)
    def _():
        block(0, bq, True)

    @pl.when(i >= 1)
    def _():
        block(pl.multiple_of((i - 1) * bq, bq), 2 * bq, True)

    def live(state):
        j, lowest = state
        return jnp.logical_and(j >= 0, lowest < -F32_EXP2_ZERO_BELOW)

    def step(state):
        j, _ = state
        block(pl.multiple_of(j * bq, bq), bq, False)
        return j - 1, carry_min()

    lax.while_loop(live, step, (i - 2, carry_min()))

    for hd in range(n_heads):
        out_t = acc_ref[hd]
        y_t = out_t * lax.rsqrt(jnp.mean(out_t * out_t, axis=0, keepdims=True) + RMS_EPS) * g_ref[...]
        o_ref[0, :, hd * HEAD_DIM:(hd + 1) * HEAD_DIM] = y_t.T.astype(o_ref.dtype)


def _sb_call(proj, ll, g, n_sb, bq, n_heads):
    b, s, _ = proj.shape
    width = n_heads * HEAD_DIM
    groups = n_sb // n_heads
    blk = (2 * bq * width + 2 * s * width) * 2 + ll.size * 2 + HEAD_DIM * bq * 4
    kern = functools.partial(_sb_kernel, bq=bq, n_heads=n_heads, scale=1.0 / math.sqrt(HEAD_DIM))
    g_cols = jnp.broadcast_to(g.astype(F32)[:, None], (HEAD_DIM, bq))
    return pl.pallas_call(
        kern,
        grid=(b, groups, s // bq),
        in_specs=[pl.BlockSpec((1, bq, width), lambda bi, h, i: (bi, i, h)),
                  pl.BlockSpec((1, s, width), lambda bi, h, i: (bi, 0, groups + h)),
                  pl.BlockSpec((1, s, width), lambda bi, h, i: (bi, 0, 2 * groups + h)),
                  pl.BlockSpec(ll.shape, lambda bi, h, i: (0, 0)),
                  pl.BlockSpec((HEAD_DIM, bq), lambda bi, h, i: (0, 0))],
        out_specs=pl.BlockSpec((1, bq, width), lambda bi, h, i: (bi, i, h)),
        out_shape=jax.ShapeDtypeStruct((b, s, n_sb * HEAD_DIM), BF16),
        scratch_shapes=[pltpu.VMEM((n_heads, HEAD_DIM, bq), F32), pltpu.VMEM((n_heads, 8, bq), F32)],
        compiler_params=_params(("parallel", "parallel", "arbitrary"),
                                _vmem_limit(blk, n_heads * (HEAD_DIM + 8) * bq * 4, 16 * n_heads * bq * bq * 4)),
        name="sb_attn",
    )(proj, proj, proj, ll, g_cols)


def _cumsum_matrix():
    s = jnp.arange(LANES)[:, None]
    j = jnp.arange(LANES)[None, :]
    later = (j > s).astype(BF16)
    return jnp.concatenate([later, later], axis=1)


def _bias_kernel(rb_ref, o_ref, *, bq):
    h = pl.program_id(0)
    t = lax.broadcasted_iota(jnp.int32, (bq, bq), 0)
    s = lax.broadcasted_iota(jnp.int32, (bq, bq), 1)
    max_exact = N_BUCKETS // 2
    far = rb_ref[N_BUCKETS - 1, h]
    for which in range(2):
        dist = t - s + which * bq
        n = jnp.maximum(dist, 0)
        nf = jnp.maximum(n, 1).astype(F32)
        large = max_exact + (jnp.log(nf / max_exact) / math.log(MAX_DISTANCE / max_exact)
                             * (N_BUCKETS - max_exact)).astype(jnp.int32)
        large = jnp.minimum(large, N_BUCKETS - 1)
        bucket = jnp.where(n < max_exact, n, large)
        bias = jnp.zeros((bq, bq), F32)
        for bkt in range(N_BUCKETS):
            bias = jnp.where(bucket == bkt, (rb_ref[bkt, h] - far) * LOG2_E, bias)
        o_ref[0, which] = bias


def _bias_call(rel_bias, bq):
    n_df = rel_bias.shape[1]
    return pl.pallas_call(
        functools.partial(_bias_kernel, bq=bq),
        grid=(n_df,),
        in_specs=[pl.BlockSpec(memory_space=pltpu.SMEM)],
        out_specs=pl.BlockSpec((1, 2, bq, bq), lambda h: (h, 0, 0, 0)),
        out_shape=jax.ShapeDtypeStruct((n_df, 2, bq, bq), F32),
        compiler_params=_params(("arbitrary",), _vmem_limit(2 * bq * bq * 4, 0, 8 * bq * bq * 4)),
        name="t5_bias",
    )(rel_bias)


def _df_kernel(lq1_ref, lk1_ref, lq2_ref, lk2_ref, q_ref, k_ref, v_ref, bias_ref, g_ref,
               o_ref, s_ref, mx_ref, ls_ref, acc_ref, *, bq, n_heads, scale, lam_init):
    i = pl.program_id(2)
    dv = 2 * HEAD_DIM
    mx_ref[...] = jnp.full_like(mx_ref, -jnp.inf)
    row = lax.broadcasted_iota(jnp.int32, (bq, bq), 0)
    col = lax.broadcasted_iota(jnp.int32, (bq, bq), 1)

    def scores(start, width, mode):
        for hd in range(n_heads):
            for mp in range(2):
                u = 2 * hd + mp
                lanes = slice(u * HEAD_DIM, (u + 1) * HEAD_DIM)
                s = _dot_nt(q_ref[0, :, lanes], k_ref[0, pl.ds(start, width), lanes]) * (scale * LOG2_E)
                if mode == "diag":
                    s = jnp.where(row >= col, s + bias_ref[hd, 0], NEG_BIG)
                elif mode == "near+diag":
                    s = jnp.concatenate(
                        [s[:, :bq] + bias_ref[hd, 1],
                         jnp.where(row >= col, s[:, bq:] + bias_ref[hd, 0], NEG_BIG)], axis=1)
                s_ref[u, :, pl.ds(start, width)] = s
                mx = mx_ref[u]
                for c in range(width // LANES):
                    mx = jnp.maximum(mx, s[:, c * LANES:(c + 1) * LANES])
                if mode == "far":
                    mx_ref[u] = mx
                else:
                    mx_ref[u] = jnp.broadcast_to(jnp.max(mx, axis=-1, keepdims=True), (bq, LANES))
                    ls_ref[u] = jnp.zeros((bq, LANES), F32)
                    acc_ref[u] = jnp.zeros((bq, dv), F32)

    def values(start, width):
        for hd in range(n_heads):
            vv = v_ref[0, pl.ds(start, width), hd * dv:(hd + 1) * dv]
            for mp in range(2):
                u = 2 * hd + mp
                s = s_ref[u, :, pl.ds(start, width)]
                row_max = mx_ref[u]
                ls = ls_ref[u]
                ps = []
                for c in range(width // LANES):
                    p = jnp.exp2(s[:, c * LANES:(c + 1) * LANES] - row_max)
                    ls = ls + p
                    ps.append(p.astype(BF16))
                ls_ref[u] = ls
                acc_ref[u] += _dot(jnp.concatenate(ps, axis=1), vv)

    def looped(n_blocks, fn):
        def body(t, _):
            fn(pl.multiple_of(t * 8 * bq, 8 * bq), 8 * bq)
            return 0

        lax.fori_loop(0, lax.shift_right_logical(n_blocks, 3), body, 0)
        for span in (4, 2, 1):
            @pl.when((n_blocks & span) == span)
            def _(span=span):
                done = n_blocks & ~(2 * span - 1)
                fn(pl.multiple_of(done * bq, span * bq), span * bq)

    looped(jnp.maximum(i - 1, 0), functools.partial(scores, mode="far"))

    @pl.when(i >= 1)
    def _():
        scores(pl.multiple_of((i - 1) * bq, bq), 2 * bq, "near+diag")

    @pl.when(i == 0)
    def _():
        scores(0, bq, "diag")

    looped(i + 1, values)

    lam = (jnp.exp(jnp.sum(lq1_ref[...] * lk1_ref[...], axis=-1, keepdims=True))
           - jnp.exp(jnp.sum(lq2_ref[...] * lk2_ref[...], axis=-1, keepdims=True)) + lam_init)
    for hd in range(n_heads):
        l0 = jnp.sum(ls_ref[2 * hd], axis=-1, keepdims=True)
        l1 = jnp.sum(ls_ref[2 * hd + 1], axis=-1, keepdims=True)
        out = acc_ref[2 * hd] / l0 - lam * (acc_ref[2 * hd + 1] / l1)
        y = out * lax.rsqrt(jnp.mean(out * out, axis=-1, keepdims=True) + RMS_EPS) * g_ref[...]
        o_ref[0, :, hd * dv:(hd + 1) * dv] = (y * (1.0 - lam_init)).astype(o_ref.dtype)


def _df_call(proj, lam_vecs, bias_tiles, g, n_sb, n_df, bq, n_heads, lam_init):
    b, s, _ = proj.shape
    dv = 2 * HEAD_DIM
    width = n_heads * dv
    groups = n_df // n_heads
    q_start = 3 * n_sb * HEAD_DIM
    assert q_start % width == 0 and n_df % n_heads == 0
    q_off = q_start // width
    k_off = q_off + groups
    v_off = k_off + groups
    blk = (2 * bq * width + 2 * s * width) * 2 + n_heads * 2 * bq * bq * 4
    kern = functools.partial(_df_kernel, bq=bq, n_heads=n_heads, scale=1.0 / math.sqrt(HEAD_DIM),
                             lam_init=lam_init)
    vec_spec = pl.BlockSpec((1, HEAD_DIM), lambda bi, h, i: (0, 0))
    n_soft = 2 * n_heads
    return pl.pallas_call(
        kern,
        grid=(b, groups, s // bq),
        in_specs=[vec_spec, vec_spec, vec_spec, vec_spec,
                  pl.BlockSpec((1, bq, width), lambda bi, h, i: (bi, i, q_off + h)),
                  pl.BlockSpec((1, s, width), lambda bi, h, i: (bi, 0, k_off + h)),
                  pl.BlockSpec((1, s, width), lambda bi, h, i: (bi, 0, v_off + h)),
                  pl.BlockSpec((n_heads, 2, bq, bq), lambda bi, h, i: (h, 0, 0, 0)),
                  pl.BlockSpec((1, dv), lambda bi, h, i: (0, 0))],
        out_specs=pl.BlockSpec((1, bq, width), lambda bi, h, i: (bi, i, h)),
        out_shape=jax.ShapeDtypeStruct((b, s, n_df * dv), BF16),
        scratch_shapes=[pltpu.VMEM((n_soft, bq, s), F32), pltpu.VMEM((n_soft, bq, LANES), F32),
                        pltpu.VMEM((n_soft, bq, LANES), F32), pltpu.VMEM((n_soft, bq, dv), F32)],
        compiler_params=_params(("parallel", "parallel", "arbitrary"),
                                _vmem_limit(blk, n_soft * bq * (s + dv + 2 * LANES) * 4,
                                            16 * n_heads * bq * bq * 4)),
        name="df_attn",
    )(*[v.reshape(1, HEAD_DIM) for v in lam_vecs], proj, proj, proj, bias_tiles,
      g.reshape(1, dv))


def _outproj_kernel(sb_ref, df_ref, w_ref, h_ref, g_ref, b_ref, o32_ref, o16_ref, *, alpha):
    tm, half = sb_ref.shape
    sub = tm // 4 if tm % 32 == 0 else tm
    for r in range(tm // sub):
        rows = slice(r * sub, (r + 1) * sub)
        mix = _dot(sb_ref[rows, :], w_ref[:half, :]) + _dot(df_ref[rows, :], w_ref[half:, :])
        y = _layernorm(alpha * h_ref[rows, :] + mix, g_ref[...], b_ref[...])
        o32_ref[rows, :] = y
        o16_ref[rows, :] = y.astype(BF16)


def _outproj_call(sb_o, df_o, w, h, g, b, tm, alpha):
    m, d = h.shape
    half = sb_o.shape[1]
    blk = tm * half * 2 * 2 + tm * d * (4 + 4 + 2) + w.size * 2
    row = lambda i: (i, 0)
    fixed = lambda i: (0, 0)
    return pl.pallas_call(
        functools.partial(_outproj_kernel, alpha=alpha),
        grid=(m // tm,),
        in_specs=[pl.BlockSpec((tm, half), row), pl.BlockSpec((tm, half), row),
                  pl.BlockSpec(w.shape, fixed),
                  pl.BlockSpec((tm, d), row),
                  pl.BlockSpec((1, d), fixed), pl.BlockSpec((1, d), fixed)],
        out_specs=[pl.BlockSpec((tm, d), row), pl.BlockSpec((tm, d), row)],
        out_shape=[jax.ShapeDtypeStruct((m, d), F32), jax.ShapeDtypeStruct((m, d), BF16)],
        compiler_params=_params(("parallel",), _vmem_limit(blk, 0, 4 * tm * d * 4)),
        name="out_proj_ln",
    )(sb_o, df_o, w, h, g.reshape(1, d), b.reshape(1, d))


def _mlp_kernel(x16_ref, x32_ref, wu_ref, wd_ref, g_ref, b_ref, o32_ref, o16_ref, acc0_ref, acc1_ref,
                *, alpha, n_tiles, n_f):
    i = pl.program_id(0)
    f = pl.program_id(1)
    accs = (acc0_ref, acc1_ref)
    rows_per_step = o32_ref.shape[0] // n_f

    @pl.when(jnp.logical_and(i == 0, f == 0))
    def _():
        acc0_ref[...] = jnp.zeros_like(acc0_ref)
        acc1_ref[...] = jnp.zeros_like(acc1_ref)

    def finish_slice(prev_ref):
        rows = pl.ds(pl.multiple_of(f * rows_per_step, rows_per_step), rows_per_step)
        y = _layernorm(alpha * x32_ref[rows, :] + prev_ref[rows, :], g_ref[...], b_ref[...])
        o32_ref[rows, :] = y
        o16_ref[rows, :] = y.astype(BF16)
        prev_ref[rows, :] = jnp.zeros((rows_per_step, prev_ref.shape[1]), F32)

    for parity in range(2):
        cur_ref, prev_ref = accs[parity], accs[1 - parity]
        mine = jnp.logical_and(i < n_tiles, lax.rem(i, 2) == parity)

        @pl.when(mine)
        def _(cur_ref=cur_ref, prev_ref=prev_ref):
            u = _dot(x16_ref[...], wu_ref[...])
            a = jnp.square(jnp.maximum(u, 0.0)).astype(BF16)
            cur_ref[...] += _dot(a, wd_ref[...])
            finish_slice(prev_ref)

    @pl.when(i == n_tiles)
    def _():
        finish_slice(accs[(n_tiles - 1) % 2])


def _mlp_call(x16, x32, wu, wd, g, b, tm, tf, alpha):
    m, d = x32.shape
    ff = wu.shape[1]
    n_tiles, n_f = m // tm, ff // tf
    assert tm % n_f == 0 and (tm // n_f) % BF16_SUBLANES == 0, (tm, n_f)
    blk = tm * d * (2 + 4 + 4 + 2) + 2 * d * tf * 2
    lagged = lambda i, f: (jnp.maximum(i - 1, 0), 0)
    fixed = lambda i, f: (0, 0)
    f_idx = lambda i, f: jnp.where(i < n_tiles, f, n_f - 1)
    return pl.pallas_call(
        functools.partial(_mlp_kernel, alpha=alpha, n_tiles=n_tiles, n_f=n_f),
        grid=(n_tiles + 1, n_f),
        in_specs=[pl.BlockSpec((tm, d), lambda i, f: (jnp.minimum(i, n_tiles - 1), 0)),
                  pl.BlockSpec((tm, d), lagged),
                  pl.BlockSpec((d, tf), lambda i, f: (0, f_idx(i, f))),
                  pl.BlockSpec((tf, d), lambda i, f: (f_idx(i, f), 0)),
                  pl.BlockSpec((1, d), fixed), pl.BlockSpec((1, d), fixed)],
        out_specs=[pl.BlockSpec((tm, d), lagged), pl.BlockSpec((tm, d), lagged)],
        out_shape=[jax.ShapeDtypeStruct((m, d), F32), jax.ShapeDtypeStruct((m, d), BF16)],
        scratch_shapes=[pltpu.VMEM((tm, d), F32), pltpu.VMEM((tm, d), F32)],
        compiler_params=_params(("arbitrary", "arbitrary"),
                                _vmem_limit(blk, 2 * tm * d * 4, 2 * tm * tf * 4 + 2 * tm * d * 4)),
        name="mlp_ln",
    )(x16, x32, wu, wd, g.reshape(1, d), b.reshape(1, d))


def _tile(n, want):
    t = min(n, want)
    while n % t or t % LANES:
        t -= LANES
        assert t > 0, (n, want)
    return t


def kernel(x, ln0_g, ln0_b, w_in, w_out, sb_norm_g, lam_q1, lam_k1, lam_q2, lam_k2, diff_norm_g,
           rel_bias, ln1_g, ln1_b, w_up, w_down, ln2_g, ln2_b):
    b, s, d = x.shape
    depth = w_in.shape[0]
    n_sb = d // (2 * HEAD_DIM)
    n_df = d // (4 * HEAD_DIM)
    assert w_in.shape[2] == 3 * n_sb * HEAD_DIM + 3 * n_df * 2 * HEAD_DIM
    assert rel_bias.shape == (N_BUCKETS, n_df)
    alpha = (2 * depth) ** 0.25
    m = b * s

    tm_ln = _tile(m, LN_ROWS)
    tm_proj, tn_proj = _tile(m, PROJ_ROWS), _tile(w_in.shape[2], PROJ_COLS)
    tm_out = _tile(m, OUT_ROWS)
    tm_mlp, tf_mlp = _tile(m, MLP_ROWS), _tile(w_up.shape[2], MLP_HIDDEN)
    bq_sb = _tile(s, SB_QUERY_ROWS)
    sb_heads = math.gcd(n_sb, SB_HEADS_PER_STEP)
    bq_df = _tile(s, DF_QUERY_ROWS)
    df_heads = math.gcd(n_df, DF_HEADS_PER_STEP)
    assert bq_df >= MAX_DISTANCE

    later_keys = _cumsum_matrix()
    bias_tiles = _bias_call(rel_bias, bq_df)

    h32, h16 = _ln_call(x.reshape(m, d), ln0_g, ln0_b, tm_ln)
    for l in range(depth):
        lam_init = 0.8 - 0.6 * math.exp(-0.3 * l)
        proj, (w_out16, w_up16, w_down16) = _matmul_call(h16, w_in, l, tm_proj, tn_proj,
                                                         (w_out, w_up, w_down))
        proj = proj.reshape(b, s, -1)
        sb_o = _sb_call(proj, later_keys, sb_norm_g[l], n_sb, bq_sb, sb_heads)
        df_o = _df_call(proj, (lam_q1[l], lam_k1[l], lam_q2[l], lam_k2[l]), bias_tiles,
                        diff_norm_g[l], n_sb, n_df, bq_df, df_heads, lam_init)
        h32, h16 = _outproj_call(sb_o.reshape(m, -1), df_o.reshape(m, -1), w_out16, h32,
                                 ln1_g[l], ln1_b[l], tm_out, alpha)
        h32, h16 = _mlp_call(h16, h32, w_up16, w_down16, ln2_g[l], ln2_b[l],
                             tm_mlp, tf_mlp, alpha)
    return h32.reshape(b, s, d)
```

```python
import functools
import math

import jax
import jax.numpy as jnp
from jax import lax
from jax.experimental import pallas as pl
from jax.experimental.pallas import tpu as pltpu

HEAD_DIM = 128
N_BUCKETS = 32
MAX_DISTANCE = 128
LN_EPS = 1e-5
RMS_EPS = 1e-5
NEG_BIG = -1e30
F32_EXP2_ZERO_BELOW = -150.5
LOG2_E = 1.4426950408889634

LANES = 128
BF16_SUBLANES = 16
V7X_VMEM_BYTES = 64 * 1024 * 1024
VMEM_LIMIT_FLOOR = 16 * 1024 * 1024
VMEM_COMPILER_RESERVE = 6 * 1024 * 1024

LN_ROWS = 512
PROJ_ROWS, PROJ_COLS = 1024, 1536
OUT_ROWS = 512
MLP_ROWS, MLP_HIDDEN = 512, 1024
SB_QUERY_ROWS, SB_HEADS_PER_STEP = 256, 8
DF_QUERY_ROWS, DF_HEADS_PER_STEP = 512, 1

F32 = jnp.float32
BF16 = jnp.bfloat16


def _vmem_limit(block_bytes, scratch_bytes, temp_bytes):
    need = 2 * block_bytes + scratch_bytes + temp_bytes
    return int(min(max(need, VMEM_LIMIT_FLOOR), V7X_VMEM_BYTES - VMEM_COMPILER_RESERVE))


def _params(semantics, vmem_bytes):
    return pltpu.CompilerParams(dimension_semantics=semantics, vmem_limit_bytes=vmem_bytes)


def _layernorm(v, g, b):
    mu = jnp.mean(v, axis=-1, keepdims=True)
    c = v - mu
    var = jnp.mean(c * c, axis=-1, keepdims=True)
    return c * lax.rsqrt(var + LN_EPS) * g + b


def _dot(a, b):
    return jnp.dot(a, b, preferred_element_type=F32)


def _dot_nt(a, b):
    return lax.dot_general(a, b, (((1,), (1,)), ((), ())), preferred_element_type=F32)


def _ln_kernel(x_ref, g_ref, b_ref, o32_ref, o16_ref):
    y = _layernorm(x_ref[...], g_ref[...], b_ref[...])
    o32_ref[...] = y
    o16_ref[...] = y.astype(BF16)


def _ln_call(x, g, b, tm):
    m, d = x.shape
    blk = tm * d * (4 + 4 + 2)
    return pl.pallas_call(
        _ln_kernel,
        grid=(m // tm,),
        in_specs=[pl.BlockSpec((tm, d), lambda i: (i, 0)),
                  pl.BlockSpec((1, d), lambda i: (0, 0)),
                  pl.BlockSpec((1, d), lambda i: (0, 0))],
        out_specs=[pl.BlockSpec((tm, d), lambda i: (i, 0)),
                   pl.BlockSpec((tm, d), lambda i: (i, 0))],
        out_shape=[jax.ShapeDtypeStruct((m, d), F32), jax.ShapeDtypeStruct((m, d), BF16)],
        compiler_params=_params(("parallel",), _vmem_limit(blk, 0, 4 * tm * d * 4)),
        name="ln_in",
    )(x, g.reshape(1, d), b.reshape(1, d))


def _matmul_kernel(*refs, n_side):
    x_ref, w_ref = refs[:2]
    side_in = refs[2:2 + n_side]
    o_ref = refs[2 + n_side]
    side_out = refs[3 + n_side:3 + 2 * n_side]
    w16_ref = refs[3 + 2 * n_side]

    @pl.when(pl.program_id(1) == 0)
    def _():
        w16_ref[...] = w_ref[...].astype(BF16)

    o_ref[...] = _dot(x_ref[...], w16_ref[...]).astype(o_ref.dtype)
    for src_ref, dst_ref in zip(side_in, side_out):
        dst_ref[...] = src_ref[...].astype(BF16)


def _slab_count(rows, steps):
    units = rows // BF16_SUBLANES
    return max(c for c in range(1, min(units, steps) + 1) if units % c == 0)


def _matmul_call(x, w, layer, tm, tn, side):
    m, k = x.shape
    n = w.shape[2]
    n_j, n_i = n // tn, m // tm
    blk = (tm * k + tm * tn) * 2 + k * tn * 4
    side_in_specs, side_out_specs, side_shapes = [], [], []
    for arr in side:
        rows, cols = arr.shape[1:]
        slabs = _slab_count(rows, n_j * n_i)
        slab = lambda j, i, slabs=slabs: jnp.minimum(j * n_i + i, slabs - 1)
        side_in_specs.append(pl.BlockSpec((None, rows // slabs, cols),
                                          lambda j, i, slab=slab: (layer, slab(j, i), 0)))
        side_out_specs.append(pl.BlockSpec((rows // slabs, cols), lambda j, i, slab=slab: (slab(j, i), 0)))
        side_shapes.append(jax.ShapeDtypeStruct((rows, cols), BF16))
        blk += (rows // slabs) * cols * 6
    outs = pl.pallas_call(
        functools.partial(_matmul_kernel, n_side=len(side)),
        grid=(n_j, n_i),
        in_specs=[pl.BlockSpec((tm, k), lambda j, i: (i, 0)),
                  pl.BlockSpec((None, k, tn), lambda j, i: (layer, 0, j))] + side_in_specs,
        out_specs=[pl.BlockSpec((tm, tn), lambda j, i: (i, j))] + side_out_specs,
        out_shape=[jax.ShapeDtypeStruct((m, n), BF16)] + side_shapes,
        scratch_shapes=[pltpu.VMEM((k, tn), BF16)],
        compiler_params=_params(("arbitrary", "arbitrary"),
                                _vmem_limit(blk, k * tn * 2, 2 * tm * tn * 4)),
        name="in_proj",
    )(x, w, *side)
    return outs[0], outs[1:]


def _sb_kernel(q_ref, k_ref, v_ref, ll_ref, g_ref, o_ref, acc_ref, carry_ref, *, bq, n_heads, scale):
    i = pl.program_id(2)
    ll = ll_ref[...]
    causal_t = (lax.broadcasted_iota(jnp.int32, (bq, bq), 0)
                < lax.broadcasted_iota(jnp.int32, (bq, bq), 1))

    def mask_diag(t):
        lead = t.shape[0] - bq
        tail = jnp.where(causal_t, t[lead:], 0.0)
        return tail if lead == 0 else jnp.concatenate([t[:lead], tail], axis=0)

    def block(start, width, diag):
        for hd in range(n_heads):
            lanes = slice(hd * HEAD_DIM, (hd + 1) * HEAD_DIM)
            q = q_ref[0, :, lanes]
            ks = k_ref[0, pl.ds(start, width), lanes]
            vs = v_ref[0, pl.ds(start, width), lanes]
            z = _dot_nt(ks, q) * (scale * LOG2_E)
            neg_abs = lax.bitcast_convert_type(
                lax.bitcast_convert_type(z, jnp.uint32) | jnp.uint32(0x80000000), F32)
            sp = jnp.maximum(z, 0.0) + jnp.log2(1.0 + jnp.exp2(neg_abs))
            log2_beta = z - sp
            if diag:
                sp = mask_diag(sp)
            carry = jnp.zeros((1, bq), F32) if diag else carry_ref[hd, 0:1, :]
            n_chunks = width // LANES
            between = [None] * n_chunks
            for c in reversed(range(n_chunks)):
                sp_c = sp[c * LANES:(c + 1) * LANES]
                hi = sp_c.astype(BF16)
                lo = (sp_c - hi.astype(F32)).astype(BF16)
                between[c] = _dot(ll, jnp.concatenate([hi, lo], axis=0)) + carry
                carry = carry + jnp.sum(sp_c, axis=0, keepdims=True)
            w = jnp.exp2(log2_beta - jnp.concatenate(between, axis=0))
            if diag:
                w = mask_diag(w)
            pv = lax.dot_general(vs, w.astype(BF16), (((0,), (0,)), ((), ())), preferred_element_type=F32)
            acc_ref[hd] = pv if diag else acc_ref[hd] + pv
            carry_ref[hd] = jnp.broadcast_to(carry, (carry_ref.shape[1], bq))

    def carry_min():
        m = carry_ref[0]
        for hd in range(1, n_heads):
            m = jnp.minimum(m, carry_ref[hd])
        return jnp.min(m)

    @pl.when(i == 0)
    def _():
        block(0, bq, True)

    @pl.when(i >= 1)
    def _():
        block(pl.multiple_of((i - 1) * bq, bq), 2 * bq, True)

    def live(state):
        j, lowest = state
        return jnp.logical_and(j >= 0, lowest < -F32_EXP2_ZERO_BELOW)

    def step(state):
        j, _ = state
        block(pl.multiple_of(j * bq, bq), bq, False)
        return j - 1, carry_min()

    lax.while_loop(live, step, (i - 2, carry_min()))

    for hd in range(n_heads):
        out_t = acc_ref[hd]
        y_t = out_t * lax.rsqrt(jnp.mean(out_t * out_t, axis=0, keepdims=True) + RMS_EPS) * g_ref[...]
        o_ref[0, :, hd * HEAD_DIM:(hd + 1) * HEAD_DIM] = y_t.T.astype(o_ref.dtype)


def _sb_call(proj, ll, g, n_sb, bq, n_heads):
    b, s, _ = proj.shape
    width = n_heads * HEAD_DIM
    groups = n_sb // n_heads
    blk = (2 * bq * width + 2 * s * width) * 2 + ll.size * 2 + HEAD_DIM * bq * 4
    kern = functools.partial(_sb_kernel, bq=bq, n_heads=n_heads, scale=1.0 / math.sqrt(HEAD_DIM))
    g_cols = jnp.broadcast_to(g.astype(F32)[:, None], (HEAD_DIM, bq))
    return pl.pallas_call(
        kern,
        grid=(b, groups, s // bq),
        in_specs=[pl.BlockSpec((1, bq, width), lambda bi, h, i: (bi, i, h)),
                  pl.BlockSpec((1, s, width), lambda bi, h, i: (bi, 0, groups + h)),
                  pl.BlockSpec((1, s, width), lambda bi, h, i: (bi, 0, 2 * groups + h)),
                  pl.BlockSpec(ll.shape, lambda bi, h, i: (0, 0)),
                  pl.BlockSpec((HEAD_DIM, bq), lambda bi, h, i: (0, 0))],
        out_specs=pl.BlockSpec((1, bq, width), lambda bi, h, i: (bi, i, h)),
        out_shape=jax.ShapeDtypeStruct((b, s, n_sb * HEAD_DIM), BF16),
        scratch_shapes=[pltpu.VMEM((n_heads, HEAD_DIM, bq), F32), pltpu.VMEM((n_heads, 8, bq), F32)],
        compiler_params=_params(("parallel", "parallel", "arbitrary"),
                                _vmem_limit(blk, n_heads * (HEAD_DIM + 8) * bq * 4, 16 * n_heads * bq * bq * 4)),
        name="sb_attn",
    )(proj, proj, proj, ll, g_cols)


def _cumsum_matrix():
    s = jnp.arange(LANES)[:, None]
    j = jnp.arange(LANES)[None, :]
    later = (j > s).astype(BF16)
    return jnp.concatenate([later, later], axis=1)


def _bias_kernel(rb_ref, o_ref, *, bq):
    h = pl.program_id(0)
    t = lax.broadcasted_iota(jnp.int32, (bq, bq), 0)
    s = lax.broadcasted_iota(jnp.int32, (bq, bq), 1)
    max_exact = N_BUCKETS // 2
    far = rb_ref[N_BUCKETS - 1, h]
    for which in range(2):
        dist = t - s + which * bq
        n = jnp.maximum(dist, 0)
        nf = jnp.maximum(n, 1).astype(F32)
        large = max_exact + (jnp.log(nf / max_exact) / math.log(MAX_DISTANCE / max_exact)
                             * (N_BUCKETS - max_exact)).astype(jnp.int32)
        large = jnp.minimum(large, N_BUCKETS - 1)
        bucket = jnp.where(n < max_exact, n, large)
        bias = jnp.zeros((bq, bq), F32)
        for bkt in range(N_BUCKETS):
            bias = jnp.where(bucket == bkt, (rb_ref[bkt, h] - far) * LOG2_E, bias)
        o_ref[0, which] = bias


def _bias_call(rel_bias, bq):
    n_df = rel_bias.shape[1]
    return pl.pallas_call(
        functools.partial(_bias_kernel, bq=bq),
        grid=(n_df,),
        in_specs=[pl.BlockSpec(memory_space=pltpu.SMEM)],
        out_specs=pl.BlockSpec((1, 2, bq, bq), lambda h: (h, 0, 0, 0)),
        out_shape=jax.ShapeDtypeStruct((n_df, 2, bq, bq), F32),
        compiler_params=_params(("arbitrary",), _vmem_limit(2 * bq * bq * 4, 0, 8 * bq * bq * 4)),
        name="t5_bias",
    )(rel_bias)


def _df_kernel(lq1_ref, lk1_ref, lq2_ref, lk2_ref, q_ref, k_ref, v_ref, bias_ref, g_ref,
               o_ref, s_ref, mx_ref, ls_ref, acc_ref, *, bq, n_heads, scale, lam_init):
    i = pl.program_id(2)
    dv = 2 * HEAD_DIM
    mx_ref[...] = jnp.full_like(mx_ref, -jnp.inf)
    row = lax.broadcasted_iota(jnp.int32, (bq, bq), 0)
    col = lax.broadcasted_iota(jnp.int32, (bq, bq), 1)

    def scores(start, width, mode):
        for hd in range(n_heads):
            for mp in range(2):
                u = 2 * hd + mp
                lanes = slice(u * HEAD_DIM, (u + 1) * HEAD_DIM)
                s = _dot_nt(q_ref[0, :, lanes], k_ref[0, pl.ds(start, width), lanes]) * (scale * LOG2_E)
                if mode == "diag":
                    s = jnp.where(row >= col, s + bias_ref[hd, 0], NEG_BIG)
                elif mode == "near+diag":
                    s = jnp.concatenate(
                        [s[:, :bq] + bias_ref[hd, 1],
                         jnp.where(row >= col, s[:, bq:] + bias_ref[hd, 0], NEG_BIG)], axis=1)
                s_ref[u, :, pl.ds(start, width)] = s
                mx = mx_ref[u]
                for c in range(width // LANES):
                    mx = jnp.maximum(mx, s[:, c * LANES:(c + 1) * LANES])
                if mode == "far":
                    mx_ref[u] = mx
                else:
                    mx_ref[u] = jnp.broadcast_to(jnp.max(mx, axis=-1, keepdims=True), (bq, LANES))
                    ls_ref[u] = jnp.zeros((bq, LANES), F32)
                    acc_ref[u] = jnp.zeros((bq, dv), F32)

    def values(start, width):
        for hd in range(n_heads):
            vv = v_ref[0, pl.ds(start, width), hd * dv:(hd + 1) * dv]
            for mp in range(2):
                u = 2 * hd + mp
                s = s_ref[u, :, pl.ds(start, width)]
                row_max = mx_ref[u]
                ls = ls_ref[u]
                ps = []
                for c in range(width // LANES):
                    p = jnp.exp2(s[:, c * LANES:(c + 1) * LANES] - row_max)
                    ls = ls + p
                    ps.append(p.astype(BF16))
                ls_ref[u] = ls
                acc_ref[u] += _dot(jnp.concatenate(ps, axis=1), vv)

    def looped(n_blocks, fn):
        def body(t, _):
            fn(pl.multiple_of(t * 8 * bq, 8 * bq), 8 * bq)
            return 0

        lax.fori_loop(0, lax.shift_right_logical(n_blocks, 3), body, 0)
        for span in (4, 2, 1):
            @pl.when((n_blocks & span) == span)
            def _(span=span):
                done = n_blocks & ~(2 * span - 1)
                fn(pl.multiple_of(done * bq, span * bq), span * bq)

    looped(jnp.maximum(i - 1, 0), functools.partial(scores, mode="far"))

    @pl.when(i >= 1)
    def _():
        scores(pl.multiple_of((i - 1) * bq, bq), 2 * bq, "near+diag")

    @pl.when(i == 0)
    def _():
        scores(0, bq, "diag")

    looped(i + 1, values)

    lam = (jnp.exp(jnp.sum(lq1_ref[...] * lk1_ref[...], axis=-1, keepdims=True))
           - jnp.exp(jnp.sum(lq2_ref[...] * lk2_ref[...], axis=-1, keepdims=True)) + lam_init)
    for hd in range(n_heads):
        l0 = jnp.sum(ls_ref[2 * hd], axis=-1, keepdims=True)
        l1 = jnp.sum(ls_ref[2 * hd + 1], axis=-1, keepdims=True)
        out = acc_ref[2 * hd] / l0 - lam * (acc_ref[2 * hd + 1] / l1)
        y = out * lax.rsqrt(jnp.mean(out * out, axis=-1, keepdims=True) + RMS_EPS) * g_ref[...]
        o_ref[0, :, hd * dv:(hd + 1) * dv] = (y * (1.0 - lam_init)).astype(o_ref.dtype)


def _df_call(proj, lam_vecs, bias_tiles, g, n_sb, n_df, bq, n_heads, lam_init):
    b, s, _ = proj.shape
    dv = 2 * HEAD_DIM
    width = n_heads * dv
    groups = n_df // n_heads
    q_start = 3 * n_sb * HEAD_DIM
    assert q_start % width == 0 and n_df % n_heads == 0
    q_off = q_start // width
    k_off = q_off + groups
    v_off = k_off + groups
    blk = (2 * bq * width + 2 * s * width) * 2 + n_heads * 2 * bq * bq * 4
    kern = functools.partial(_df_kernel, bq=bq, n_heads=n_heads, scale=1.0 / math.sqrt(HEAD_DIM),
                             lam_init=lam_init)
    vec_spec = pl.BlockSpec((1, HEAD_DIM), lambda bi, h, i: (0, 0))
    n_soft = 2 * n_heads
    return pl.pallas_call(
        kern,
        grid=(b, groups, s // bq),
        in_specs=[vec_spec, vec_spec, vec_spec, vec_spec,
                  pl.BlockSpec((1, bq, width), lambda bi, h, i: (bi, i, q_off + h)),
                  pl.BlockSpec((1, s, width), lambda bi, h, i: (bi, 0, k_off + h)),
                  pl.BlockSpec((1, s, width), lambda bi, h, i: (bi, 0, v_off + h)),
                  pl.BlockSpec((n_heads, 2, bq, bq), lambda bi, h, i: (h, 0, 0, 0)),
                  pl.BlockSpec((1, dv), lambda bi, h, i: (0, 0))],
        out_specs=pl.BlockSpec((1, bq, width), lambda bi, h, i: (bi, i, h)),
        out_shape=jax.ShapeDtypeStruct((b, s, n_df * dv), BF16),
        scratch_shapes=[pltpu.VMEM((n_soft, bq, s), F32), pltpu.VMEM((n_soft, bq, LANES), F32),
                        pltpu.VMEM((n_soft, bq, LANES), F32), pltpu.VMEM((n_soft, bq, dv), F32)],
        compiler_params=_params(("parallel", "parallel", "arbitrary"),
                                _vmem_limit(blk, n_soft * bq * (s + dv + 2 * LANES) * 4,
                                            16 * n_heads * bq * bq * 4)),
        name="df_attn",
    )(*[v.reshape(1, HEAD_DIM) for v in lam_vecs], proj, proj, proj, bias_tiles,
      g.reshape(1, dv))


def _outproj_kernel(sb_ref, df_ref, w_ref, h_ref, g_ref, b_ref, o32_ref, o16_ref, *, alpha):
    tm, half = sb_ref.shape
    sub = tm // 4 if tm % 32 == 0 else tm
    for r in range(tm // sub):
        rows = slice(r * sub, (r + 1) * sub)
        mix = _dot(sb_ref[rows, :], w_ref[:half, :]) + _dot(df_ref[rows, :], w_ref[half:, :])
        y = _layernorm(alpha * h_ref[rows, :] + mix, g_ref[...], b_ref[...])
        o32_ref[rows, :] = y
        o16_ref[rows, :] = y.astype(BF16)


def _outproj_call(sb_o, df_o, w, h, g, b, tm, alpha):
    m, d = h.shape
    half = sb_o.shape[1]
    blk = tm * half * 2 * 2 + tm * d * (4 + 4 + 2) + w.size * 2
    row = lambda i: (i, 0)
    fixed = lambda i: (0, 0)
    return pl.pallas_call(
        functools.partial(_outproj_kernel, alpha=alpha),
        grid=(m // tm,),
        in_specs=[pl.BlockSpec((tm, half), row), pl.BlockSpec((tm, half), row),
                  pl.BlockSpec(w.shape, fixed),
                  pl.BlockSpec((tm, d), row),
                  pl.BlockSpec((1, d), fixed), pl.BlockSpec((1, d), fixed)],
        out_specs=[pl.BlockSpec((tm, d), row), pl.BlockSpec((tm, d), row)],
        out_shape=[jax.ShapeDtypeStruct((m, d), F32), jax.ShapeDtypeStruct((m, d), BF16)],
        compiler_params=_params(("parallel",), _vmem_limit(blk, 0, 4 * tm * d * 4)),
        name="out_proj_ln",
    )(sb_o, df_o, w, h, g.reshape(1, d), b.reshape(1, d))


def _mlp_kernel(x16_ref, x32_ref, wu_ref, wd_ref, g_ref, b_ref, o32_ref, o16_ref, acc0_ref, acc1_ref,
                *, alpha, n_tiles, n_f):
    i = pl.program_id(0)
    f = pl.program_id(1)
    accs = (acc0_ref, acc1_ref)
    rows_per_step = o32_ref.shape[0] // n_f

    @pl.when(jnp.logical_and(i == 0, f == 0))
    def _():
        acc0_ref[...] = jnp.zeros_like(acc0_ref)
        acc1_ref[...] = jnp.zeros_like(acc1_ref)

    def finish_slice(prev_ref):
        rows = pl.ds(pl.multiple_of(f * rows_per_step, rows_per_step), rows_per_step)
        y = _layernorm(alpha * x32_ref[rows, :] + prev_ref[rows, :], g_ref[...], b_ref[...])
        o32_ref[rows, :] = y
        o16_ref[rows, :] = y.astype(BF16)
        prev_ref[rows, :] = jnp.zeros((rows_per_step, prev_ref.shape[1]), F32)

    for parity in range(2):
        cur_ref, prev_ref = accs[parity], accs[1 - parity]
        mine = jnp.logical_and(i < n_tiles, lax.rem(i, 2) == parity)

        @pl.when(mine)
        def _(cur_ref=cur_ref, prev_ref=prev_ref):
            u = _dot(x16_ref[...], wu_ref[...])
            a = jnp.square(jnp.maximum(u, 0.0)).astype(BF16)
            cur_ref[...] += _dot(a, wd_ref[...])
            finish_slice(prev_ref)

    @pl.when(i == n_tiles)
    def _():
        finish_slice(accs[(n_tiles - 1) % 2])


def _mlp_call(x16, x32, wu, wd, g, b, tm, tf, alpha):
    m, d = x32.shape
    ff = wu.shape[1]
    n_tiles, n_f = m // tm, ff // tf
    assert tm % n_f == 0 and (tm // n_f) % BF16_SUBLANES == 0, (tm, n_f)
    blk = tm * d * (2 + 4 + 4 + 2) + 2 * d * tf * 2
    lagged = lambda i, f: (jnp.maximum(i - 1, 0), 0)
    fixed = lambda i, f: (0, 0)
    f_idx = lambda i, f: jnp.where(i < n_tiles, f, n_f - 1)
    return pl.pallas_call(
        functools.partial(_mlp_kernel, alpha=alpha, n_tiles=n_tiles, n_f=n_f),
        grid=(n_tiles + 1, n_f),
        in_specs=[pl.BlockSpec((tm, d), lambda i, f: (jnp.minimum(i, n_tiles - 1), 0)),
                  pl.BlockSpec((tm, d), lagged),
                  pl.BlockSpec((d, tf), lambda i, f: (0, f_idx(i, f))),
                  pl.BlockSpec((tf, d), lambda i, f: (f_idx(i, f), 0)),
                  pl.BlockSpec((1, d), fixed), pl.BlockSpec((1, d), fixed)],
        out_specs=[pl.BlockSpec((tm, d), lagged), pl.BlockSpec((tm, d), lagged)],
        out_shape=[jax.ShapeDtypeStruct((m, d), F32), jax.ShapeDtypeStruct((m, d), BF16)],
        scratch_shapes=[pltpu.VMEM((tm, d), F32), pltpu.VMEM((tm, d), F32)],
        compiler_params=_params(("arbitrary", "arbitrary"),
                                _vmem_limit(blk, 2 * tm * d * 4, 2 * tm * tf * 4 + 2 * tm * d * 4)),
        name="mlp_ln",
    )(x16, x32, wu, wd, g.reshape(1, d), b.reshape(1, d))


def _tile(n, want):
    t = min(n, want)
    while n % t or t % LANES:
        t -= LANES
        assert t > 0, (n, want)
    return t


def kernel(x, ln0_g, ln0_b, w_in, w_out, sb_norm_g, lam_q1, lam_k1, lam_q2, lam_k2, diff_norm_g,
           rel_bias, ln1_g, ln1_b, w_up, w_down, ln2_g, ln2_b):
    b, s, d = x.shape
    depth = w_in.shape[0]
    n_sb = d // (2 * HEAD_DIM)
    n_df = d // (4 * HEAD_DIM)
    assert w_in.shape[2] == 3 * n_sb * HEAD_DIM + 3 * n_df * 2 * HEAD_DIM
    assert rel_bias.shape == (N_BUCKETS, n_df)
    alpha = (2 * depth) ** 0.25
    m = b * s

    tm_ln = _tile(m, LN_ROWS)
    tm_proj, tn_proj = _tile(m, PROJ_ROWS), _tile(w_in.shape[2], PROJ_COLS)
    tm_out = _tile(m, OUT_ROWS)
    tm_mlp, tf_mlp = _tile(m, MLP_ROWS), _tile(w_up.shape[2], MLP_HIDDEN)
    bq_sb = _tile(s, SB_QUERY_ROWS)
    sb_heads = math.gcd(n_sb, SB_HEADS_PER_STEP)
    bq_df = _tile(s, DF_QUERY_ROWS)
    df_heads = math.gcd(n_df, DF_HEADS_PER_STEP)
    assert bq_df >= MAX_DISTANCE

    uu = _cumsum_matrix()
    bias_tiles = _bias_call(rel_bias, bq_df)

    h32, h16 = _ln_call(x.reshape(m, d), ln0_g, ln0_b, tm_ln)
    for l in range(depth):
        lam_init = 0.8 - 0.6 * math.exp(-0.3 * l)
        proj, (w_out16, w_up16, w_down16) = _matmul_call(h16, w_in, l, tm_proj, tn_proj,
                                                         (w_out, w_up, w_down))
        proj = proj.reshape(b, s, -1)
        sb_o = _sb_call(proj, uu, sb_norm_g[l], n_sb, bq_sb, sb_heads)
        df_o = _df_call(proj, (lam_q1[l], lam_k1[l], lam_q2[l], lam_k2[l]), bias_tiles,
                        diff_norm_g[l], n_sb, n_df, bq_df, df_heads, lam_init)
        h32, h16 = _outproj_call(sb_o.reshape(m, -1), df_o.reshape(m, -1), w_out16, h32,
                                 ln1_g[l], ln1_b[l], tm_out, alpha)
        h32, h16 = _mlp_call(h16, h32, w_up16, w_down16, ln2_g[l], ln2_b[l],
                             tm_mlp, tf_mlp, alpha)
    return h32.reshape(b, s, d)
```

```python
import functools
import math

import jax
import jax.numpy as jnp
from jax import lax
from jax.experimental import pallas as pl
from jax.experimental.pallas import tpu as pltpu

HEAD_DIM = 128
N_BUCKETS = 32
MAX_DISTANCE = 128
LN_EPS = 1e-5
RMS_EPS = 1e-5
NEG_BIG = -1e30
F32_EXP2_ZERO_BELOW = -150.5
LOG2_E = 1.4426950408889634

LANES = 128
BF16_SUBLANES = 16
V7X_VMEM_BYTES = 64 * 1024 * 1024
VMEM_LIMIT_FLOOR = 16 * 1024 * 1024
VMEM_COMPILER_RESERVE = 6 * 1024 * 1024

LN_ROWS = 512
PROJ_ROWS, PROJ_COLS = 1024, 1536
OUT_ROWS = 512
MLP_ROWS, MLP_HIDDEN = 512, 1024
SB_QUERY_ROWS, SB_HEADS_PER_STEP, SB_BLOCKS_PER_STEP = 256, 8, 2
DF_QUERY_ROWS, DF_HEADS_PER_STEP = 512, 1

F32 = jnp.float32
BF16 = jnp.bfloat16


def _vmem_limit(block_bytes, scratch_bytes, temp_bytes):
    need = 2 * block_bytes + scratch_bytes + temp_bytes
    return int(min(max(need, VMEM_LIMIT_FLOOR), V7X_VMEM_BYTES - VMEM_COMPILER_RESERVE))


def _params(semantics, vmem_bytes):
    return pltpu.CompilerParams(dimension_semantics=semantics, vmem_limit_bytes=vmem_bytes)


def _layernorm(v, g, b):
    mu = jnp.mean(v, axis=-1, keepdims=True)
    c = v - mu
    var = jnp.mean(c * c, axis=-1, keepdims=True)
    return c * lax.rsqrt(var + LN_EPS) * g + b


def _dot(a, b):
    return jnp.dot(a, b, preferred_element_type=F32)


def _dot_nt(a, b):
    return lax.dot_general(a, b, (((1,), (1,)), ((), ())), preferred_element_type=F32)


def _ln_kernel(x_ref, g_ref, b_ref, o32_ref, o16_ref):
    y = _layernorm(x_ref[...], g_ref[...], b_ref[...])
    o32_ref[...] = y
    o16_ref[...] = y.astype(BF16)


def _ln_call(x, g, b, tm):
    m, d = x.shape
    blk = tm * d * (4 + 4 + 2)
    return pl.pallas_call(
        _ln_kernel,
        grid=(m // tm,),
        in_specs=[pl.BlockSpec((tm, d), lambda i: (i, 0)),
                  pl.BlockSpec((1, d), lambda i: (0, 0)),
                  pl.BlockSpec((1, d), lambda i: (0, 0))],
        out_specs=[pl.BlockSpec((tm, d), lambda i: (i, 0)),
                   pl.BlockSpec((tm, d), lambda i: (i, 0))],
        out_shape=[jax.ShapeDtypeStruct((m, d), F32), jax.ShapeDtypeStruct((m, d), BF16)],
        compiler_params=_params(("parallel",), _vmem_limit(blk, 0, 4 * tm * d * 4)),
        name="ln_in",
    )(x, g.reshape(1, d), b.reshape(1, d))


def _matmul_kernel(*refs, n_side):
    x_ref, w_ref = refs[:2]
    side_in = refs[2:2 + n_side]
    o_ref = refs[2 + n_side]
    side_out = refs[3 + n_side:3 + 2 * n_side]
    w16_ref = refs[3 + 2 * n_side]

    @pl.when(pl.program_id(1) == 0)
    def _():
        w16_ref[...] = w_ref[...].astype(BF16)

    o_ref[...] = _dot(x_ref[...], w16_ref[...]).astype(o_ref.dtype)
    for src_ref, dst_ref in zip(side_in, side_out):
        dst_ref[...] = src_ref[...].astype(BF16)


def _slab_count(rows, steps):
    units = rows // BF16_SUBLANES
    return max(c for c in range(1, min(units, steps) + 1) if units % c == 0)


def _matmul_call(x, w, layer, tm, tn, side):
    m, k = x.shape
    n = w.shape[2]
    n_j, n_i = n // tn, m // tm
    blk = (tm * k + tm * tn) * 2 + k * tn * 4
    side_in_specs, side_out_specs, side_shapes = [], [], []
    for arr in side:
        rows, cols = arr.shape[1:]
        slabs = _slab_count(rows, n_j * n_i)
        slab = lambda j, i, slabs=slabs: jnp.minimum(j * n_i + i, slabs - 1)
        side_in_specs.append(pl.BlockSpec((None, rows // slabs, cols),
                                          lambda j, i, slab=slab: (layer, slab(j, i), 0)))
        side_out_specs.append(pl.BlockSpec((rows // slabs, cols), lambda j, i, slab=slab: (slab(j, i), 0)))
        side_shapes.append(jax.ShapeDtypeStruct((rows, cols), BF16))
        blk += (rows // slabs) * cols * 6
    outs = pl.pallas_call(
        functools.partial(_matmul_kernel, n_side=len(side)),
        grid=(n_j, n_i),
        in_specs=[pl.BlockSpec((tm, k), lambda j, i: (i, 0)),
                  pl.BlockSpec((None, k, tn), lambda j, i: (layer, 0, j))] + side_in_specs,
        out_specs=[pl.BlockSpec((tm, tn), lambda j, i: (i, j))] + side_out_specs,
        out_shape=[jax.ShapeDtypeStruct((m, n), BF16)] + side_shapes,
        scratch_shapes=[pltpu.VMEM((k, tn), BF16)],
        compiler_params=_params(("arbitrary", "arbitrary"),
                                _vmem_limit(blk, k * tn * 2, 2 * tm * tn * 4)),
        name="in_proj",
    )(x, w, *side)
    return outs[0], outs[1:]


def _sb_kernel(q_ref, *refs, bq, n_heads, n_sub, scale):
    first = pl.program_id(2) * n_sub
    for sub in range(n_sub):
        _sb_query_block(first + sub, slice(sub * bq, (sub + 1) * bq), q_ref, *refs,
                        bq=bq, n_heads=n_heads, scale=scale)


def _sb_query_block(i, rows, q_ref, k_ref, v_ref, ll_ref, g_ref, o_ref, acc_ref, carry_ref, *, bq, n_heads, scale):
    ll = ll_ref[...]
    causal_t = (lax.broadcasted_iota(jnp.int32, (bq, bq), 0)
                < lax.broadcasted_iota(jnp.int32, (bq, bq), 1))

    def mask_diag(t):
        lead = t.shape[0] - bq
        tail = jnp.where(causal_t, t[lead:], 0.0)
        return tail if lead == 0 else jnp.concatenate([t[:lead], tail], axis=0)

    def block(start, width, diag):
        for hd in range(n_heads):
            lanes = slice(hd * HEAD_DIM, (hd + 1) * HEAD_DIM)
            q = q_ref[0, rows, lanes]
            ks = k_ref[0, pl.ds(start, width), lanes]
            vs = v_ref[0, pl.ds(start, width), lanes]
            z = _dot_nt(ks, q) * (scale * LOG2_E)
            neg_abs = lax.bitcast_convert_type(
                lax.bitcast_convert_type(z, jnp.uint32) | jnp.uint32(0x80000000), F32)
            sp = jnp.maximum(z, 0.0) + jnp.log2(1.0 + jnp.exp2(neg_abs))
            log2_beta = z - sp
            if diag:
                sp = mask_diag(sp)
            carry = jnp.zeros((1, bq), F32) if diag else carry_ref[hd, 0:1, :]
            n_chunks = width // LANES
            between = [None] * n_chunks
            for c in reversed(range(n_chunks)):
                sp_c = sp[c * LANES:(c + 1) * LANES]
                hi = sp_c.astype(BF16)
                lo = (sp_c - hi.astype(F32)).astype(BF16)
                between[c] = _dot(ll, jnp.concatenate([hi, lo], axis=0)) + carry
                carry = carry + jnp.sum(sp_c, axis=0, keepdims=True)
            w = jnp.exp2(log2_beta - jnp.concatenate(between, axis=0))
            if diag:
                w = mask_diag(w)
            pv = lax.dot_general(vs, w.astype(BF16), (((0,), (0,)), ((), ())), preferred_element_type=F32)
            acc_ref[hd] = pv if diag else acc_ref[hd] + pv
            carry_ref[hd] = jnp.broadcast_to(carry, (carry_ref.shape[1], bq))

    def carry_min():
        m = carry_ref[0]
        for hd in range(1, n_heads):
            m = jnp.minimum(m, carry_ref[hd])
        return jnp.min(m)

    @pl.when(i == 0)
    def _():
        block(0, bq, True)

    @pl.when(i >= 1)
    def _():
        block(pl.multiple_of((i - 1) * bq, bq), 2 * bq, True)

    def live(state):
        j, lowest = state
        return jnp.logical_and(j >= 0, lowest < -F32_EXP2_ZERO_BELOW)

    def step(state):
        j, _ = state
        block(pl.multiple_of(j * bq, bq), bq, False)
        return j - 1, carry_min()

    lax.while_loop(live, step, (i - 2, carry_min()))

    for hd in range(n_heads):
        out_t = acc_ref[hd]
        y_t = out_t * lax.rsqrt(jnp.mean(out_t * out_t, axis=0, keepdims=True) + RMS_EPS) * g_ref[...]
        o_ref[0, rows, hd * HEAD_DIM:(hd + 1) * HEAD_DIM] = y_t.T.astype(o_ref.dtype)


def _sb_call(proj, ll, g, n_sb, bq, n_heads, n_sub):
    b, s, _ = proj.shape
    width = n_heads * HEAD_DIM
    groups = n_sb // n_heads
    blk = (2 * n_sub * bq * width + 2 * s * width) * 2 + ll.size * 2 + HEAD_DIM * bq * 4
    kern = functools.partial(_sb_kernel, bq=bq, n_heads=n_heads, n_sub=n_sub, scale=1.0 / math.sqrt(HEAD_DIM))
    g_cols = jnp.broadcast_to(g.astype(F32)[:, None], (HEAD_DIM, bq))
    return pl.pallas_call(
        kern,
        grid=(b, groups, s // (n_sub * bq)),
        in_specs=[pl.BlockSpec((1, n_sub * bq, width), lambda bi, h, i: (bi, i, h)),
                  pl.BlockSpec((1, s, width), lambda bi, h, i: (bi, 0, groups + h)),
                  pl.BlockSpec((1, s, width), lambda bi, h, i: (bi, 0, 2 * groups + h)),
                  pl.BlockSpec(ll.shape, lambda bi, h, i: (0, 0)),
                  pl.BlockSpec((HEAD_DIM, bq), lambda bi, h, i: (0, 0))],
        out_specs=pl.BlockSpec((1, n_sub * bq, width), lambda bi, h, i: (bi, i, h)),
        out_shape=jax.ShapeDtypeStruct((b, s, n_sb * HEAD_DIM), BF16),
        scratch_shapes=[pltpu.VMEM((n_heads, HEAD_DIM, bq), F32), pltpu.VMEM((n_heads, 8, bq), F32)],
        compiler_params=_params(("parallel", "parallel", "arbitrary"),
                                _vmem_limit(blk, n_heads * (HEAD_DIM + 8) * bq * 4, 16 * n_heads * bq * bq * 4)),
        name="sb_attn",
    )(proj, proj, proj, ll, g_cols)


def _cumsum_matrix():
    s = jnp.arange(LANES)[:, None]
    j = jnp.arange(LANES)[None, :]
    later = (j > s).astype(BF16)
    return jnp.concatenate([later, later], axis=1)


def _bias_kernel(rb_ref, o_ref, *, bq):
    h = pl.program_id(0)
    t = lax.broadcasted_iota(jnp.int32, (bq, bq), 0)
    s = lax.broadcasted_iota(jnp.int32, (bq, bq), 1)
    max_exact = N_BUCKETS // 2
    far = rb_ref[N_BUCKETS - 1, h]
    for which in range(2):
        dist = t - s + which * bq
        n = jnp.maximum(dist, 0)
        nf = jnp.maximum(n, 1).astype(F32)
        large = max_exact + (jnp.log(nf / max_exact) / math.log(MAX_DISTANCE / max_exact)
                             * (N_BUCKETS - max_exact)).astype(jnp.int32)
        large = jnp.minimum(large, N_BUCKETS - 1)
        bucket = jnp.where(n < max_exact, n, large)
        bias = jnp.zeros((bq, bq), F32)
        for bkt in range(N_BUCKETS):
            bias = jnp.where(bucket == bkt, (rb_ref[bkt, h] - far) * LOG2_E, bias)
        o_ref[0, which] = bias


def _bias_call(rel_bias, bq):
    n_df = rel_bias.shape[1]
    return pl.pallas_call(
        functools.partial(_bias_kernel, bq=bq),
        grid=(n_df,),
        in_specs=[pl.BlockSpec(memory_space=pltpu.SMEM)],
        out_specs=pl.BlockSpec((1, 2, bq, bq), lambda h: (h, 0, 0, 0)),
        out_shape=jax.ShapeDtypeStruct((n_df, 2, bq, bq), F32),
        compiler_params=_params(("arbitrary",), _vmem_limit(2 * bq * bq * 4, 0, 8 * bq * bq * 4)),
        name="t5_bias",
    )(rel_bias)


def _df_kernel(lq1_ref, lk1_ref, lq2_ref, lk2_ref, q_ref, k_ref, v_ref, bias_ref, g_ref,
               o_ref, s_ref, mx_ref, ls_ref, acc_ref, *, bq, n_heads, scale, lam_init):
    i = pl.program_id(2)
    dv = 2 * HEAD_DIM
    mx_ref[...] = jnp.full_like(mx_ref, -jnp.inf)
    row = lax.broadcasted_iota(jnp.int32, (bq, bq), 0)
    col = lax.broadcasted_iota(jnp.int32, (bq, bq), 1)

    def scores(start, width, mode):
        for hd in range(n_heads):
            for mp in range(2):
                u = 2 * hd + mp
                lanes = slice(u * HEAD_DIM, (u + 1) * HEAD_DIM)
                s = _dot_nt(q_ref[0, :, lanes], k_ref[0, pl.ds(start, width), lanes]) * (scale * LOG2_E)
                if mode == "diag":
                    s = jnp.where(row >= col, s + bias_ref[hd, 0], NEG_BIG)
                elif mode == "near+diag":
                    s = jnp.concatenate(
                        [s[:, :bq] + bias_ref[hd, 1],
                         jnp.where(row >= col, s[:, bq:] + bias_ref[hd, 0], NEG_BIG)], axis=1)
                s_ref[u, :, pl.ds(start, width)] = s
                mx = mx_ref[u]
                for c in range(width // LANES):
                    mx = jnp.maximum(mx, s[:, c * LANES:(c + 1) * LANES])
                if mode == "far":
                    mx_ref[u] = mx
                else:
                    mx_ref[u] = jnp.broadcast_to(jnp.max(mx, axis=-1, keepdims=True), (bq, LANES))
                    ls_ref[u] = jnp.zeros((bq, LANES), F32)
                    acc_ref[u] = jnp.zeros((bq, dv), F32)

    def values(start, width):
        for hd in range(n_heads):
            vv = v_ref[0, pl.ds(start, width), hd * dv:(hd + 1) * dv]
            for mp in range(2):
                u = 2 * hd + mp
                s = s_ref[u, :, pl.ds(start, width)]
                row_max = mx_ref[u]
                ls = ls_ref[u]
                ps = []
                for c in range(width // LANES):
                    p = jnp.exp2(s[:, c * LANES:(c + 1) * LANES] - row_max)
                    ls = ls + p
                    ps.append(p.astype(BF16))
                ls_ref[u] = ls
                acc_ref[u] += _dot(jnp.concatenate(ps, axis=1), vv)

    def looped(n_blocks, fn):
        def body(t, _):
            fn(pl.multiple_of(t * 8 * bq, 8 * bq), 8 * bq)
            return 0

        lax.fori_loop(0, lax.shift_right_logical(n_blocks, 3), body, 0)
        for span in (4, 2, 1):
            @pl.when((n_blocks & span) == span)
            def _(span=span):
                done = n_blocks & ~(2 * span - 1)
                fn(pl.multiple_of(done * bq, span * bq), span * bq)

    looped(jnp.maximum(i - 1, 0), functools.partial(scores, mode="far"))

    @pl.when(i >= 1)
    def _():
        scores(pl.multiple_of((i - 1) * bq, bq), 2 * bq, "near+diag")

    @pl.when(i == 0)
    def _():
        scores(0, bq, "diag")

    looped(i + 1, values)

    lam = (jnp.exp(jnp.sum(lq1_ref[...] * lk1_ref[...], axis=-1, keepdims=True))
           - jnp.exp(jnp.sum(lq2_ref[...] * lk2_ref[...], axis=-1, keepdims=True)) + lam_init)
    for hd in range(n_heads):
        l0 = jnp.sum(ls_ref[2 * hd], axis=-1, keepdims=True)
        l1 = jnp.sum(ls_ref[2 * hd + 1], axis=-1, keepdims=True)
        out = acc_ref[2 * hd] / l0 - lam * (acc_ref[2 * hd + 1] / l1)
        y = out * lax.rsqrt(jnp.mean(out * out, axis=-1, keepdims=True) + RMS_EPS) * g_ref[...]
        o_ref[0, :, hd * dv:(hd + 1) * dv] = (y * (1.0 - lam_init)).astype(o_ref.dtype)


def _df_call(proj, lam_vecs, bias_tiles, g, n_sb, n_df, bq, n_heads, lam_init):
    b, s, _ = proj.shape
    dv = 2 * HEAD_DIM
    width = n_heads * dv
    groups = n_df // n_heads
    q_start = 3 * n_sb * HEAD_DIM
    assert q_start % width == 0 and n_df % n_heads == 0
    q_off = q_start // width
    k_off = q_off + groups
    v_off = k_off + groups
    blk = (2 * bq * width + 2 * s * width) * 2 + n_heads * 2 * bq * bq * 4
    kern = functools.partial(_df_kernel, bq=bq, n_heads=n_heads, scale=1.0 / math.sqrt(HEAD_DIM),
                             lam_init=lam_init)
    vec_spec = pl.BlockSpec((1, HEAD_DIM), lambda bi, h, i: (0, 0))
    n_soft = 2 * n_heads
    return pl.pallas_call(
        kern,
        grid=(b, groups, s // bq),
        in_specs=[vec_spec, vec_spec, vec_spec, vec_spec,
                  pl.BlockSpec((1, bq, width), lambda bi, h, i: (bi, i, q_off + h)),
                  pl.BlockSpec((1, s, width), lambda bi, h, i: (bi, 0, k_off + h)),
                  pl.BlockSpec((1, s, width), lambda bi, h, i: (bi, 0, v_off + h)),
                  pl.BlockSpec((n_heads, 2, bq, bq), lambda bi, h, i: (h, 0, 0, 0)),
                  pl.BlockSpec((1, dv), lambda bi, h, i: (0, 0))],
        out_specs=pl.BlockSpec((1, bq, width), lambda bi, h, i: (bi, i, h)),
        out_shape=jax.ShapeDtypeStruct((b, s, n_df * dv), BF16),
        scratch_shapes=[pltpu.VMEM((n_soft, bq, s), F32), pltpu.VMEM((n_soft, bq, LANES), F32),
                        pltpu.VMEM((n_soft, bq, LANES), F32), pltpu.VMEM((n_soft, bq, dv), F32)],
        compiler_params=_params(("parallel", "parallel", "arbitrary"),
                                _vmem_limit(blk, n_soft * bq * (s + dv + 2 * LANES) * 4,
                                            16 * n_heads * bq * bq * 4)),
        name="df_attn",
    )(*[v.reshape(1, HEAD_DIM) for v in lam_vecs], proj, proj, proj, bias_tiles,
      g.reshape(1, dv))


def _outproj_kernel(sb_ref, df_ref, w_ref, h_ref, g_ref, b_ref, o32_ref, o16_ref, *, alpha):
    tm, half = sb_ref.shape
    sub = tm // 4 if tm % 32 == 0 else tm
    for r in range(tm // sub):
        rows = slice(r * sub, (r + 1) * sub)
        mix = _dot(sb_ref[rows, :], w_ref[:half, :]) + _dot(df_ref[rows, :], w_ref[half:, :])
        y = _layernorm(alpha * h_ref[rows, :] + mix, g_ref[...], b_ref[...])
        o32_ref[rows, :] = y
        o16_ref[rows, :] = y.astype(BF16)


def _outproj_call(sb_o, df_o, w, h, g, b, tm, alpha):
    m, d = h.shape
    half = sb_o.shape[1]
    blk = tm * half * 2 * 2 + tm * d * (4 + 4 + 2) + w.size * 2
    row = lambda i: (i, 0)
    fixed = lambda i: (0, 0)
    return pl.pallas_call(
        functools.partial(_outproj_kernel, alpha=alpha),
        grid=(m // tm,),
        in_specs=[pl.BlockSpec((tm, half), row), pl.BlockSpec((tm, half), row),
                  pl.BlockSpec(w.shape, fixed),
                  pl.BlockSpec((tm, d), row),
                  pl.BlockSpec((1, d), fixed), pl.BlockSpec((1, d), fixed)],
        out_specs=[pl.BlockSpec((tm, d), row), pl.BlockSpec((tm, d), row)],
        out_shape=[jax.ShapeDtypeStruct((m, d), F32), jax.ShapeDtypeStruct((m, d), BF16)],
        compiler_params=_params(("parallel",), _vmem_limit(blk, 0, 4 * tm * d * 4)),
        name="out_proj_ln",
    )(sb_o, df_o, w, h, g.reshape(1, d), b.reshape(1, d))


def _mlp_kernel(x16_ref, x32_ref, wu_ref, wd_ref, g_ref, b_ref, o32_ref, o16_ref, acc0_ref, acc1_ref,
                *, alpha, n_tiles, n_f):
    i = pl.program_id(0)
    f = pl.program_id(1)
    accs = (acc0_ref, acc1_ref)
    rows_per_step = o32_ref.shape[0] // n_f

    @pl.when(jnp.logical_and(i == 0, f == 0))
    def _():
        acc0_ref[...] = jnp.zeros_like(acc0_ref)
        acc1_ref[...] = jnp.zeros_like(acc1_ref)

    def finish_slice(prev_ref):
        rows = pl.ds(pl.multiple_of(f * rows_per_step, rows_per_step), rows_per_step)
        y = _layernorm(alpha * x32_ref[rows, :] + prev_ref[rows, :], g_ref[...], b_ref[...])
        o32_ref[rows, :] = y
        o16_ref[rows, :] = y.astype(BF16)
        prev_ref[rows, :] = jnp.zeros((rows_per_step, prev_ref.shape[1]), F32)

    for parity in range(2):
        cur_ref, prev_ref = accs[parity], accs[1 - parity]
        mine = jnp.logical_and(i < n_tiles, lax.rem(i, 2) == parity)

        @pl.when(mine)
        def _(cur_ref=cur_ref, prev_ref=prev_ref):
            u = _dot(x16_ref[...], wu_ref[...])
            a = jnp.square(jnp.maximum(u, 0.0)).astype(BF16)
            cur_ref[...] += _dot(a, wd_ref[...])
            finish_slice(prev_ref)

    @pl.when(i == n_tiles)
    def _():
        finish_slice(accs[(n_tiles - 1) % 2])


def _mlp_call(x16, x32, wu, wd, g, b, tm, tf, alpha):
    m, d = x32.shape
    ff = wu.shape[1]
    n_tiles, n_f = m // tm, ff // tf
    assert tm % n_f == 0 and (tm // n_f) % BF16_SUBLANES == 0, (tm, n_f)
    blk = tm * d * (2 + 4 + 4 + 2) + 2 * d * tf * 2
    lagged = lambda i, f: (jnp.maximum(i - 1, 0), 0)
    fixed = lambda i, f: (0, 0)
    f_idx = lambda i, f: jnp.where(i < n_tiles, f, n_f - 1)
    return pl.pallas_call(
        functools.partial(_mlp_kernel, alpha=alpha, n_tiles=n_tiles, n_f=n_f),
        grid=(n_tiles + 1, n_f),
        in_specs=[pl.BlockSpec((tm, d), lambda i, f: (jnp.minimum(i, n_tiles - 1), 0)),
                  pl.BlockSpec((tm, d), lagged),
                  pl.BlockSpec((d, tf), lambda i, f: (0, f_idx(i, f))),
                  pl.BlockSpec((tf, d), lambda i, f: (f_idx(i, f), 0)),
                  pl.BlockSpec((1, d), fixed), pl.BlockSpec((1, d), fixed)],
        out_specs=[pl.BlockSpec((tm, d), lagged), pl.BlockSpec((tm, d), lagged)],
        out_shape=[jax.ShapeDtypeStruct((m, d), F32), jax.ShapeDtypeStruct((m, d), BF16)],
        scratch_shapes=[pltpu.VMEM((tm, d), F32), pltpu.VMEM((tm, d), F32)],
        compiler_params=_params(("arbitrary", "arbitrary"),
                                _vmem_limit(blk, 2 * tm * d * 4, 2 * tm * tf * 4 + 2 * tm * d * 4)),
        name="mlp_ln",
    )(x16, x32, wu, wd, g.reshape(1, d), b.reshape(1, d))


def _tile(n, want):
    t = min(n, want)
    while n % t or t % LANES:
        t -= LANES
        assert t > 0, (n, want)
    return t


def kernel(x, ln0_g, ln0_b, w_in, w_out, sb_norm_g, lam_q1, lam_k1, lam_q2, lam_k2, diff_norm_g,
           rel_bias, ln1_g, ln1_b, w_up, w_down, ln2_g, ln2_b):
    b, s, d = x.shape
    depth = w_in.shape[0]
    n_sb = d // (2 * HEAD_DIM)
    n_df = d // (4 * HEAD_DIM)
    assert w_in.shape[2] == 3 * n_sb * HEAD_DIM + 3 * n_df * 2 * HEAD_DIM
    assert rel_bias.shape == (N_BUCKETS, n_df)
    alpha = (2 * depth) ** 0.25
    m = b * s

    tm_ln = _tile(m, LN_ROWS)
    tm_proj, tn_proj = _tile(m, PROJ_ROWS), _tile(w_in.shape[2], PROJ_COLS)
    tm_out = _tile(m, OUT_ROWS)
    tm_mlp, tf_mlp = _tile(m, MLP_ROWS), _tile(w_up.shape[2], MLP_HIDDEN)
    bq_sb = _tile(s, SB_QUERY_ROWS)
    sb_heads = math.gcd(n_sb, SB_HEADS_PER_STEP)
    sb_blocks = math.gcd(s // bq_sb, SB_BLOCKS_PER_STEP)
    bq_df = _tile(s, DF_QUERY_ROWS)
    df_heads = math.gcd(n_df, DF_HEADS_PER_STEP)
    assert bq_df >= MAX_DISTANCE

    uu = _cumsum_matrix()
    bias_tiles = _bias_call(rel_bias, bq_df)

    h32, h16 = _ln_call(x.reshape(m, d), ln0_g, ln0_b, tm_ln)
    for l in range(depth):
        lam_init = 0.8 - 0.6 * math.exp(-0.3 * l)
        proj, (w_out16, w_up16, w_down16) = _matmul_call(h16, w_in, l, tm_proj, tn_proj,
                                                         (w_out, w_up, w_down))
        proj = proj.reshape(b, s, -1)
        sb_o = _sb_call(proj, uu, sb_norm_g[l], n_sb, bq_sb, sb_heads, sb_blocks)
        df_o = _df_call(proj, (lam_q1[l], lam_k1[l], lam_q2[l], lam_k2[l]), bias_tiles,
                        diff_norm_g[l], n_sb, n_df, bq_df, df_heads, lam_init)
        h32, h16 = _outproj_call(sb_o.reshape(m, -1), df_o.reshape(m, -1), w_out16, h32,
                                 ln1_g[l], ln1_b[l], tm_out, alpha)
        h32, h16 = _mlp_call(h16, h32, w_up16, w_down16, ln2_g[l], ln2_b[l],
                             tm_mlp, tf_mlp, alpha)
    return h32.reshape(b, s, d)
```

```python
import functools
import math

import jax
import jax.numpy as jnp
from jax import lax
from jax.experimental import pallas as pl
from jax.experimental.pallas import tpu as pltpu

HEAD_DIM = 128
N_BUCKETS = 32
MAX_DISTANCE = 128
LN_EPS = 1e-5
RMS_EPS = 1e-5
NEG_BIG = -1e30
F32_EXP2_ZERO_BELOW = -150.5
LOG2_E = 1.4426950408889634

LANES = 128
BF16_SUBLANES = 16
V7X_VMEM_BYTES = 64 * 1024 * 1024
VMEM_LIMIT_FLOOR = 16 * 1024 * 1024
VMEM_COMPILER_RESERVE = 6 * 1024 * 1024

LN_ROWS = 512
PROJ_ROWS, PROJ_COLS = 1024, 1536
OUT_ROWS = 512
MLP_ROWS, MLP_HIDDEN = 512, 1024
SB_QUERY_ROWS, SB_HEADS_PER_STEP = 256, 8
DF_QUERY_ROWS, DF_HEADS_PER_STEP = 512, 1

F32 = jnp.float32
BF16 = jnp.bfloat16


def _vmem_limit(block_bytes, scratch_bytes, temp_bytes):
    need = 2 * block_bytes + scratch_bytes + temp_bytes
    return int(min(max(need, VMEM_LIMIT_FLOOR), V7X_VMEM_BYTES - VMEM_COMPILER_RESERVE))


def _params(semantics, vmem_bytes):
    return pltpu.CompilerParams(dimension_semantics=semantics, vmem_limit_bytes=vmem_bytes)


def _layernorm(v, g, b):
    mu = jnp.mean(v, axis=-1, keepdims=True)
    c = v - mu
    var = jnp.mean(c * c, axis=-1, keepdims=True)
    return c * lax.rsqrt(var + LN_EPS) * g + b


def _dot(a, b):
    return jnp.dot(a, b, preferred_element_type=F32)


def _dot_nt(a, b):
    return lax.dot_general(a, b, (((1,), (1,)), ((), ())), preferred_element_type=F32)


def _ln_kernel(x_ref, g_ref, b_ref, o32_ref, o16_ref):
    y = _layernorm(x_ref[...], g_ref[...], b_ref[...])
    o32_ref[...] = y
    o16_ref[...] = y.astype(BF16)


def _ln_call(x, g, b, tm):
    m, d = x.shape
    blk = tm * d * (4 + 4 + 2)
    return pl.pallas_call(
        _ln_kernel,
        grid=(m // tm,),
        in_specs=[pl.BlockSpec((tm, d), lambda i: (i, 0)),
                  pl.BlockSpec((1, d), lambda i: (0, 0)),
                  pl.BlockSpec((1, d), lambda i: (0, 0))],
        out_specs=[pl.BlockSpec((tm, d), lambda i: (i, 0)),
                   pl.BlockSpec((tm, d), lambda i: (i, 0))],
        out_shape=[jax.ShapeDtypeStruct((m, d), F32), jax.ShapeDtypeStruct((m, d), BF16)],
        compiler_params=_params(("parallel",), _vmem_limit(blk, 0, 4 * tm * d * 4)),
        name="ln_in",
    )(x, g.reshape(1, d), b.reshape(1, d))


def _matmul_kernel(*refs, n_side):
    x_ref, w_ref = refs[:2]
    side_in = refs[2:2 + n_side]
    o_ref = refs[2 + n_side]
    side_out = refs[3 + n_side:3 + 2 * n_side]
    w16_ref = refs[3 + 2 * n_side]

    @pl.when(pl.program_id(1) == 0)
    def _():
        w16_ref[...] = w_ref[...].astype(BF16)

    o_ref[...] = _dot(x_ref[...], w16_ref[...]).astype(o_ref.dtype)
    for src_ref, dst_ref in zip(side_in, side_out):
        dst_ref[...] = src_ref[...].astype(BF16)


def _slab_count(rows, steps):
    units = rows // BF16_SUBLANES
    return max(c for c in range(1, min(units, steps) + 1) if units % c == 0)


def _matmul_call(x, w, layer, tm, tn, side):
    m, k = x.shape
    n = w.shape[2]
    n_j, n_i = n // tn, m // tm
    blk = (tm * k + tm * tn) * 2 + k * tn * 4
    side_in_specs, side_out_specs, side_shapes = [], [], []
    for arr in side:
        rows, cols = arr.shape[1:]
        slabs = _slab_count(rows, n_j * n_i)
        slab = lambda j, i, slabs=slabs: jnp.minimum(j * n_i + i, slabs - 1)
        side_in_specs.append(pl.BlockSpec((None, rows // slabs, cols),
                                          lambda j, i, slab=slab: (layer, slab(j, i), 0)))
        side_out_specs.append(pl.BlockSpec((rows // slabs, cols), lambda j, i, slab=slab: (slab(j, i), 0)))
        side_shapes.append(jax.ShapeDtypeStruct((rows, cols), BF16))
        blk += (rows // slabs) * cols * 6
    outs = pl.pallas_call(
        functools.partial(_matmul_kernel, n_side=len(side)),
        grid=(n_j, n_i),
        in_specs=[pl.BlockSpec((tm, k), lambda j, i: (i, 0)),
                  pl.BlockSpec((None, k, tn), lambda j, i: (layer, 0, j))] + side_in_specs,
        out_specs=[pl.BlockSpec((tm, tn), lambda j, i: (i, j))] + side_out_specs,
        out_shape=[jax.ShapeDtypeStruct((m, n), BF16)] + side_shapes,
        scratch_shapes=[pltpu.VMEM((k, tn), BF16)],
        compiler_params=_params(("arbitrary", "arbitrary"),
                                _vmem_limit(blk, k * tn * 2, 2 * tm * tn * 4)),
        name="in_proj",
    )(x, w, *side)
    return outs[0], outs[1:]


def _sb_kernel(q_ref, k_ref, v_ref, ll_ref, g_ref, o_ref, acc_ref, carry_ref, *, bq, n_heads, scale):
    i = pl.program_id(2)
    ll = ll_ref[...]
    causal_t = (lax.broadcasted_iota(jnp.int32, (bq, bq), 0)
                < lax.broadcasted_iota(jnp.int32, (bq, bq), 1))

    def mask_diag(t):
        lead = t.shape[0] - bq
        tail = jnp.where(causal_t, t[lead:], 0.0)
        return tail if lead == 0 else jnp.concatenate([t[:lead], tail], axis=0)

    def block(start, width, diag):
        for hd in range(n_heads):
            lanes = slice(hd * HEAD_DIM, (hd + 1) * HEAD_DIM)
            q = q_ref[0, :, lanes]
            ks = k_ref[0, pl.ds(start, width), lanes]
            vs = v_ref[0, pl.ds(start, width), lanes]
            z = _dot_nt(ks, q) * (scale * LOG2_E)
            neg_abs = lax.bitcast_convert_type(
                lax.bitcast_convert_type(z, jnp.uint32) | jnp.uint32(0x80000000), F32)
            sp = jnp.maximum(z, 0.0) + jnp.log2(1.0 + jnp.exp2(neg_abs))
            log2_beta = z - sp
            if diag:
                sp = mask_diag(sp)
            carry = jnp.zeros((1, bq), F32) if diag else carry_ref[hd, 0:1, :]
            n_chunks = width // LANES
            between = [None] * n_chunks
            for c in reversed(range(n_chunks)):
                sp_c = sp[c * LANES:(c + 1) * LANES]
                hi = sp_c.astype(BF16)
                lo = (sp_c - hi.astype(F32)).astype(BF16)
                between[c] = _dot(ll, jnp.concatenate([hi, lo], axis=0)) + carry
                carry = carry + jnp.sum(sp_c, axis=0, keepdims=True)
            w = jnp.exp2(log2_beta - jnp.concatenate(between, axis=0))
            if diag:
                w = mask_diag(w)
            pv = lax.dot_general(vs, w.astype(BF16), (((0,), (0,)), ((), ())), preferred_element_type=F32)
            acc_ref[hd] = pv if diag else acc_ref[hd] + pv
            carry_ref[hd] = jnp.broadcast_to(carry, (carry_ref.shape[1], bq))

    def carry_min():
        m = carry_ref[0]
        for hd in range(1, n_heads):
            m = jnp.minimum(m, carry_ref[hd])
        return jnp.min(m)

    @pl.when(i == 0)
    def _():
        block(0, bq, True)

    @pl.when(i >= 1)
    def _():
        block(pl.multiple_of((i - 1) * bq, bq), 2 * bq, True)

    def live(state):
        j, lowest = state
        return jnp.logical_and(j >= 0, lowest < -F32_EXP2_ZERO_BELOW)

    def step(state):
        j, _ = state
        block(pl.multiple_of(j * bq, bq), bq, False)
        return j - 1, carry_min()

    lax.while_loop(live, step, (i - 2, carry_min()))

    for hd in range(n_heads):
        out_t = acc_ref[hd]
        y_t = out_t * lax.rsqrt(jnp.mean(out_t * out_t, axis=0, keepdims=True) + RMS_EPS) * g_ref[...]
        o_ref[0, :, hd * HEAD_DIM:(hd + 1) * HEAD_DIM] = y_t.T.astype(o_ref.dtype)


def _sb_call(proj, ll, g, n_sb, bq, n_heads):
    b, s, _ = proj.shape
    width = n_heads * HEAD_DIM
    groups = n_sb // n_heads
    blk = (2 * bq * width + 2 * s * width) * 2 + ll.size * 2 + HEAD_DIM * bq * 4
    kern = functools.partial(_sb_kernel, bq=bq, n_heads=n_heads, scale=1.0 / math.sqrt(HEAD_DIM))
    g_cols = jnp.broadcast_to(g.astype(F32)[:, None], (HEAD_DIM, bq))
    return pl.pallas_call(
        kern,
        grid=(b, groups, s // bq),
        in_specs=[pl.BlockSpec((1, bq, width), lambda bi, h, i: (bi, i, h)),
                  pl.BlockSpec((1, s, width), lambda bi, h, i: (bi, 0, groups + h)),
                  pl.BlockSpec((1, s, width), lambda bi, h, i: (bi, 0, 2 * groups + h)),
                  pl.BlockSpec(ll.shape, lambda bi, h, i: (0, 0)),
                  pl.BlockSpec((HEAD_DIM, bq), lambda bi, h, i: (0, 0))],
        out_specs=pl.BlockSpec((1, bq, width), lambda bi, h, i: (bi, i, h)),
        out_shape=jax.ShapeDtypeStruct((b, s, n_sb * HEAD_DIM), BF16),
        scratch_shapes=[pltpu.VMEM((n_heads, HEAD_DIM, bq), F32), pltpu.VMEM((n_heads, 8, bq), F32)],
        compiler_params=_params(("parallel", "parallel", "arbitrary"),
                                _vmem_limit(blk, n_heads * (HEAD_DIM + 8) * bq * 4, 16 * n_heads * bq * bq * 4)),
        name="sb_attn",
    )(proj, proj, proj, ll, g_cols)


def _cumsum_matrix():
    s = jnp.arange(LANES)[:, None]
    j = jnp.arange(LANES)[None, :]
    later = (j > s).astype(BF16)
    return jnp.concatenate([later, later], axis=1)


def _bias_kernel(rb_ref, o_ref, *, bq):
    h = pl.program_id(0)
    t = lax.broadcasted_iota(jnp.int32, (bq, bq), 0)
    s = lax.broadcasted_iota(jnp.int32, (bq, bq), 1)
    max_exact = N_BUCKETS // 2
    far = rb_ref[N_BUCKETS - 1, h]
    for which in range(2):
        dist = t - s + which * bq
        n = jnp.maximum(dist, 0)
        nf = jnp.maximum(n, 1).astype(F32)
        large = max_exact + (jnp.log(nf / max_exact) / math.log(MAX_DISTANCE / max_exact)
                             * (N_BUCKETS - max_exact)).astype(jnp.int32)
        large = jnp.minimum(large, N_BUCKETS - 1)
        bucket = jnp.where(n < max_exact, n, large)
        bias = jnp.zeros((bq, bq), F32)
        for bkt in range(N_BUCKETS):
            bias = jnp.where(bucket == bkt, (rb_ref[bkt, h] - far) * LOG2_E, bias)
        o_ref[0, which] = bias


def _bias_call(rel_bias, bq):
    n_df = rel_bias.shape[1]
    return pl.pallas_call(
        functools.partial(_bias_kernel, bq=bq),
        grid=(n_df,),
        in_specs=[pl.BlockSpec(memory_space=pltpu.SMEM)],
        out_specs=pl.BlockSpec((1, 2, bq, bq), lambda h: (h, 0, 0, 0)),
        out_shape=jax.ShapeDtypeStruct((n_df, 2, bq, bq), F32),
        compiler_params=_params(("arbitrary",), _vmem_limit(2 * bq * bq * 4, 0, 8 * bq * bq * 4)),
        name="t5_bias",
    )(rel_bias)


def _df_kernel(lq1_ref, lk1_ref, lq2_ref, lk2_ref, q_ref, k_ref, v_ref, bias_ref, g_ref,
               o_ref, s_ref, mx_ref, ls_ref, acc_ref, *, bq, n_heads, scale, lam_init):
    i = pl.program_id(2)
    dv = 2 * HEAD_DIM
    mx_ref[...] = jnp.full_like(mx_ref, -jnp.inf)
    row = lax.broadcasted_iota(jnp.int32, (bq, bq), 0)
    col = lax.broadcasted_iota(jnp.int32, (bq, bq), 1)

    def scores(start, width, mode):
        for hd in range(n_heads):
            for mp in range(2):
                u = 2 * hd + mp
                lanes = slice(u * HEAD_DIM, (u + 1) * HEAD_DIM)
                s = _dot_nt(q_ref[0, :, lanes], k_ref[0, pl.ds(start, width), lanes]) * (scale * LOG2_E)
                if mode == "diag":
                    s = jnp.where(row >= col, s + bias_ref[hd, 0], NEG_BIG)
                elif mode == "near+diag":
                    s = jnp.concatenate(
                        [s[:, :bq] + bias_ref[hd, 1],
                         jnp.where(row >= col, s[:, bq:] + bias_ref[hd, 0], NEG_BIG)], axis=1)
                s_ref[u, :, pl.ds(start, width)] = s
                mx = mx_ref[u]
                for c in range(width // LANES):
                    mx = jnp.maximum(mx, s[:, c * LANES:(c + 1) * LANES])
                if mode == "far":
                    mx_ref[u] = mx
                else:
                    mx_ref[u] = jnp.broadcast_to(jnp.max(mx, axis=-1, keepdims=True), (bq, LANES))
                    ls_ref[u] = jnp.zeros((bq, LANES), F32)
                    acc_ref[u] = jnp.zeros((bq, dv), F32)

    def values(start, width):
        for hd in range(n_heads):
            vv = v_ref[0, pl.ds(start, width), hd * dv:(hd + 1) * dv]
            for mp in range(2):
                u = 2 * hd + mp
                s = s_ref[u, :, pl.ds(start, width)]
                row_max = mx_ref[u]
                ls = ls_ref[u]
                ps = []
                for c in range(width // LANES):
                    p = jnp.exp2(s[:, c * LANES:(c + 1) * LANES] - row_max)
                    ls = ls + p
                    ps.append(p.astype(BF16))
                ls_ref[u] = ls
                acc_ref[u] += _dot(jnp.concatenate(ps, axis=1), vv)

    def looped(n_blocks, fn):
        def body(t, _):
            fn(pl.multiple_of(t * 4 * bq, 4 * bq), 4 * bq)
            return 0

        lax.fori_loop(0, lax.shift_right_logical(n_blocks, 2), body, 0)
        for span in (2, 1):
            @pl.when((n_blocks & span) == span)
            def _(span=span):
                done = n_blocks & ~(2 * span - 1)
                fn(pl.multiple_of(done * bq, span * bq), span * bq)

    looped(jnp.maximum(i - 1, 0), functools.partial(scores, mode="far"))

    @pl.when(i >= 1)
    def _():
        scores(pl.multiple_of((i - 1) * bq, bq), 2 * bq, "near+diag")

    @pl.when(i == 0)
    def _():
        scores(0, bq, "diag")

    looped(i + 1, values)

    lam = (jnp.exp(jnp.sum(lq1_ref[...] * lk1_ref[...], axis=-1, keepdims=True))
           - jnp.exp(jnp.sum(lq2_ref[...] * lk2_ref[...], axis=-1, keepdims=True)) + lam_init)
    for hd in range(n_heads):
        l0 = jnp.sum(ls_ref[2 * hd], axis=-1, keepdims=True)
        l1 = jnp.sum(ls_ref[2 * hd + 1], axis=-1, keepdims=True)
        out = acc_ref[2 * hd] / l0 - lam * (acc_ref[2 * hd + 1] / l1)
        y = out * lax.rsqrt(jnp.mean(out * out, axis=-1, keepdims=True) + RMS_EPS) * g_ref[...]
        o_ref[0, :, hd * dv:(hd + 1) * dv] = (y * (1.0 - lam_init)).astype(o_ref.dtype)


def _df_call(proj, lam_vecs, bias_tiles, g, n_sb, n_df, bq, n_heads, lam_init):
    b, s, _ = proj.shape
    dv = 2 * HEAD_DIM
    width = n_heads * dv
    groups = n_df // n_heads
    q_start = 3 * n_sb * HEAD_DIM
    assert q_start % width == 0 and n_df % n_heads == 0
    q_off = q_start // width
    k_off = q_off + groups
    v_off = k_off + groups
    blk = (2 * bq * width + 2 * s * width) * 2 + n_heads * 2 * bq * bq * 4
    kern = functools.partial(_df_kernel, bq=bq, n_heads=n_heads, scale=1.0 / math.sqrt(HEAD_DIM),
                             lam_init=lam_init)
    vec_spec = pl.BlockSpec((1, HEAD_DIM), lambda bi, h, i: (0, 0))
    n_soft = 2 * n_heads
    return pl.pallas_call(
        kern,
        grid=(b, groups, s // bq),
        in_specs=[vec_spec, vec_spec, vec_spec, vec_spec,
                  pl.BlockSpec((1, bq, width), lambda bi, h, i: (bi, i, q_off + h)),
                  pl.BlockSpec((1, s, width), lambda bi, h, i: (bi, 0, k_off + h)),
                  pl.BlockSpec((1, s, width), lambda bi, h, i: (bi, 0, v_off + h)),
                  pl.BlockSpec((n_heads, 2, bq, bq), lambda bi, h, i: (h, 0, 0, 0)),
                  pl.BlockSpec((1, dv), lambda bi, h, i: (0, 0))],
        out_specs=pl.BlockSpec((1, bq, width), lambda bi, h, i: (bi, i, h)),
        out_shape=jax.ShapeDtypeStruct((b, s, n_df * dv), BF16),
        scratch_shapes=[pltpu.VMEM((n_soft, bq, s), F32), pltpu.VMEM((n_soft, bq, LANES), F32),
                        pltpu.VMEM((n_soft, bq, LANES), F32), pltpu.VMEM((n_soft, bq, dv), F32)],
        compiler_params=_params(("parallel", "parallel", "arbitrary"),
                                _vmem_limit(blk, n_soft * bq * (s + dv + 2 * LANES) * 4,
                                            16 * n_heads * bq * bq * 4)),
        name="df_attn",
    )(*[v.reshape(1, HEAD_DIM) for v in lam_vecs], proj, proj, proj, bias_tiles,
      g.reshape(1, dv))


def _outproj_kernel(sb_ref, df_ref, w_ref, h_ref, g_ref, b_ref, o32_ref, o16_ref, *, alpha):
    tm, half = sb_ref.shape
    sub = tm // 4 if tm % 32 == 0 else tm
    for r in range(tm // sub):
        rows = slice(r * sub, (r + 1) * sub)
        mix = _dot(sb_ref[rows, :], w_ref[:half, :]) + _dot(df_ref[rows, :], w_ref[half:, :])
        y = _layernorm(alpha * h_ref[rows, :] + mix, g_ref[...], b_ref[...])
        o32_ref[rows, :] = y
        o16_ref[rows, :] = y.astype(BF16)


def _outproj_call(sb_o, df_o, w, h, g, b, tm, alpha):
    m, d = h.shape
    half = sb_o.shape[1]
    blk = tm * half * 2 * 2 + tm * d * (4 + 4 + 2) + w.size * 2
    row = lambda i: (i, 0)
    fixed = lambda i: (0, 0)
    return pl.pallas_call(
        functools.partial(_outproj_kernel, alpha=alpha),
        grid=(m // tm,),
        in_specs=[pl.BlockSpec((tm, half), row), pl.BlockSpec((tm, half), row),
                  pl.BlockSpec(w.shape, fixed),
                  pl.BlockSpec((tm, d), row),
                  pl.BlockSpec((1, d), fixed), pl.BlockSpec((1, d), fixed)],
        out_specs=[pl.BlockSpec((tm, d), row), pl.BlockSpec((tm, d), row)],
        out_shape=[jax.ShapeDtypeStruct((m, d), F32), jax.ShapeDtypeStruct((m, d), BF16)],
        compiler_params=_params(("parallel",), _vmem_limit(blk, 0, 4 * tm * d * 4)),
        name="out_proj_ln",
    )(sb_o, df_o, w, h, g.reshape(1, d), b.reshape(1, d))


def _mlp_kernel(x16_ref, x32_ref, wu_ref, wd_ref, g_ref, b_ref, o32_ref, o16_ref, acc0_ref, acc1_ref,
                *, alpha, n_tiles, n_f):
    i = pl.program_id(0)
    f = pl.program_id(1)
    accs = (acc0_ref, acc1_ref)
    rows_per_step = o32_ref.shape[0] // n_f

    @pl.when(jnp.logical_and(i == 0, f == 0))
    def _():
        acc0_ref[...] = jnp.zeros_like(acc0_ref)
        acc1_ref[...] = jnp.zeros_like(acc1_ref)

    def finish_slice(prev_ref):
        rows = pl.ds(pl.multiple_of(f * rows_per_step, rows_per_step), rows_per_step)
        y = _layernorm(alpha * x32_ref[rows, :] + prev_ref[rows, :], g_ref[...], b_ref[...])
        o32_ref[rows, :] = y
        o16_ref[rows, :] = y.astype(BF16)
        prev_ref[rows, :] = jnp.zeros((rows_per_step, prev_ref.shape[1]), F32)

    for parity in range(2):
        cur_ref, prev_ref = accs[parity], accs[1 - parity]
        mine = jnp.logical_and(i < n_tiles, lax.rem(i, 2) == parity)

        @pl.when(mine)
        def _(cur_ref=cur_ref, prev_ref=prev_ref):
            u = _dot(x16_ref[...], wu_ref[...])
            a = jnp.square(jnp.maximum(u, 0.0)).astype(BF16)
            cur_ref[...] += _dot(a, wd_ref[...])
            finish_slice(prev_ref)

    @pl.when(i == n_tiles)
    def _():
        finish_slice(accs[(n_tiles - 1) % 2])


def _mlp_call(x16, x32, wu, wd, g, b, tm, tf, alpha):
    m, d = x32.shape
    ff = wu.shape[1]
    n_tiles, n_f = m // tm, ff // tf
    assert tm % n_f == 0 and (tm // n_f) % BF16_SUBLANES == 0, (tm, n_f)
    blk = tm * d * (2 + 4 + 4 + 2) + 2 * d * tf * 2
    lagged = lambda i, f: (jnp.maximum(i - 1, 0), 0)
    fixed = lambda i, f: (0, 0)
    f_idx = lambda i, f: jnp.where(i < n_tiles, f, n_f - 1)
    return pl.pallas_call(
        functools.partial(_mlp_kernel, alpha=alpha, n_tiles=n_tiles, n_f=n_f),
        grid=(n_tiles + 1, n_f),
        in_specs=[pl.BlockSpec((tm, d), lambda i, f: (jnp.minimum(i, n_tiles - 1), 0)),
                  pl.BlockSpec((tm, d), lagged),
                  pl.BlockSpec((d, tf), lambda i, f: (0, f_idx(i, f))),
                  pl.BlockSpec((tf, d), lambda i, f: (f_idx(i, f), 0)),
                  pl.BlockSpec((1, d), fixed), pl.BlockSpec((1, d), fixed)],
        out_specs=[pl.BlockSpec((tm, d), lagged), pl.BlockSpec((tm, d), lagged)],
        out_shape=[jax.ShapeDtypeStruct((m, d), F32), jax.ShapeDtypeStruct((m, d), BF16)],
        scratch_shapes=[pltpu.VMEM((tm, d), F32), pltpu.VMEM((tm, d), F32)],
        compiler_params=_params(("arbitrary", "arbitrary"),
                                _vmem_limit(blk, 2 * tm * d * 4, 2 * tm * tf * 4 + 2 * tm * d * 4)),
        name="mlp_ln",
    )(x16, x32, wu, wd, g.reshape(1, d), b.reshape(1, d))


def _tile(n, want):
    t = min(n, want)
    while n % t or t % LANES:
        t -= LANES
        assert t > 0, (n, want)
    return t


def kernel(x, ln0_g, ln0_b, w_in, w_out, sb_norm_g, lam_q1, lam_k1, lam_q2, lam_k2, diff_norm_g,
           rel_bias, ln1_g, ln1_b, w_up, w_down, ln2_g, ln2_b):
    b, s, d = x.shape
    depth = w_in.shape[0]
    n_sb = d // (2 * HEAD_DIM)
    n_df = d // (4 * HEAD_DIM)
    assert w_in.shape[2] == 3 * n_sb * HEAD_DIM + 3 * n_df * 2 * HEAD_DIM
    assert rel_bias.shape == (N_BUCKETS, n_df)
    alpha = (2 * depth) ** 0.25
    m = b * s

    tm_ln = _tile(m, LN_ROWS)
    tm_proj, tn_proj = _tile(m, PROJ_ROWS), _tile(w_in.shape[2], PROJ_COLS)
    tm_out = _tile(m, OUT_ROWS)
    tm_mlp, tf_mlp = _tile(m, MLP_ROWS), _tile(w_up.shape[2], MLP_HIDDEN)
    bq_sb = _tile(s, SB_QUERY_ROWS)
    sb_heads = math.gcd(n_sb, SB_HEADS_PER_STEP)
    bq_df = _tile(s, DF_QUERY_ROWS)
    df_heads = math.gcd(n_df, DF_HEADS_PER_STEP)
    assert bq_df >= MAX_DISTANCE

    uu = _cumsum_matrix()
    bias_tiles = _bias_call(rel_bias, bq_df)

    h32, h16 = _ln_call(x.reshape(m, d), ln0_g, ln0_b, tm_ln)
    for l in range(depth):
        lam_init = 0.8 - 0.6 * math.exp(-0.3 * l)
        proj, (w_out16, w_up16, w_down16) = _matmul_call(h16, w_in, l, tm_proj, tn_proj,
                                                         (w_out, w_up, w_down))
        proj = proj.reshape(b, s, -1)
        sb_o = _sb_call(proj, uu, sb_norm_g[l], n_sb, bq_sb, sb_heads)
        df_o = _df_call(proj, (lam_q1[l], lam_k1[l], lam_q2[l], lam_k2[l]), bias_tiles,
                        diff_norm_g[l], n_sb, n_df, bq_df, df_heads, lam_init)
        h32, h16 = _outproj_call(sb_o.reshape(m, -1), df_o.reshape(m, -1), w_out16, h32,
                                 ln1_g[l], ln1_b[l], tm_out, alpha)
        h32, h16 = _mlp_call(h16, h32, w_up16, w_down16, ln2_g[l], ln2_b[l],
                             tm_mlp, tf_mlp, alpha)
    return h32.reshape(b, s, d)
```
